```python
import jax, jax.numpy as jnp
from jax import lax
import numpy as np

D_MODEL = 2048
BATCH = 8
SEQ = 2048
DEPTH = 2

GRID_W = 64
CTX_LEN = 256
N_MIXERS = 2
EPS = 1e-6
ADA_CHUNKS = 6
MLSTM_HEADS = 8
MLSTM_DQK = D_MODEL // (2 * MLSTM_HEADS)
MLSTM_DV = D_MODEL // MLSTM_HEADS
QK_W = MLSTM_HEADS * MLSTM_DQK
V_W = MLSTM_HEADS * MLSTM_DV
MLSTM_IN_W = 2 * QK_W + 2 * V_W + 4 * MLSTM_HEADS
MLSTM_CHUNK = 64
GATE_CAP = 15.0
M_INIT = -1e30
N_EXPERTS = 64
TOP_K = 6
EXPERT_DIM = D_MODEL // 4
ROUTED_SCALE = 2.5
MOE_BLOCK = 128

kernel_name = 'hybrid_conv_mlstm_moe_prefix_dit'


def rmsnorm(x, w):
    xf = x.astype(jnp.float32)
    y = xf * lax.rsqrt(jnp.mean(xf * xf, axis=-1, keepdims=True) + EPS)
    return (y * w.astype(jnp.float32)).astype(x.dtype)


def ada_mod(cond, w, b):
    return jnp.split(jax.nn.silu(cond) @ w + b, ADA_CHUNKS, axis=-1)


def modulate(h, shift, scale):
    return h * (1 + scale) + shift


def dwconv3(u, w):
    pad = [(0, 0)] * (u.ndim - 2) + [(1, 1), (0, 0)]
    up = jnp.pad(u, pad)
    return up[..., :-2, :] * w[0] + up[..., 1:-1, :] * w[1] + up[..., 2:, :] * w[2]


def conv_mixer(h, w_in, w_dw, w_out, on_grid):
    b_gate, c_gate, h_in = jnp.split(h @ w_in, 3, axis=-1)
    u = c_gate * h_in
    if on_grid:
        bsz, n_tok, d = u.shape
        rows = n_tok // GRID_W
        y = dwconv3(u.reshape(bsz, rows, GRID_W, d), w_dw).reshape(bsz, n_tok, d)
    else:
        y = dwconv3(u, w_dw)
    return (b_gate * y) @ w_out


def mlstm_project(h, w_in, gate_b):
    bsz, n_tok, _ = h.shape
    q, k, v, o, g = jnp.split(h @ w_in, [QK_W, 2 * QK_W, 2 * QK_W + V_W, 2 * QK_W + 2 * V_W], axis=-1)

    def heads(t, d):
        return t.reshape(bsz, n_tok, MLSTM_HEADS, d).transpose(0, 2, 1, 3).astype(jnp.float32)

    q = heads(q, MLSTM_DQK)
    k = heads(k, MLSTM_DQK) * (MLSTM_DQK ** -0.5)
    v = heads(v, MLSTM_DV)
    g = (g + gate_b).astype(jnp.float32)
    g = GATE_CAP * jnp.tanh(g / GATE_CAP)
    g = g.reshape(bsz, n_tok, 4, MLSTM_HEADS).transpose(2, 0, 3, 1)
    gates = (g[0], jax.nn.log_sigmoid(g[1]), g[2], jax.nn.log_sigmoid(g[3]))
    return q, k, v, o, gates


def mlstm_chunkwise(q, k, v, log_i, log_f, state):
    bsz, n_head, n_tok, dqk = q.shape
    dv = v.shape[-1]
    n_chunk = n_tok // MLSTM_CHUNK

    def to_chunks(a):
        return jnp.moveaxis(a.reshape(bsz, n_head, n_chunk, MLSTM_CHUNK, *a.shape[3:]), 2, 0)

    xs = (to_chunks(q), to_chunks(k), to_chunks(v), to_chunks(log_i), to_chunks(log_f))
    tri = jnp.tril(jnp.ones((MLSTM_CHUNK, MLSTM_CHUNK), dtype=bool))

    def step(carry, inp):
        c_st, n_st, m_st = carry
        qc, kc, vc, li, lf = inp
        b = jnp.cumsum(lf, axis=-1)
        a = b + m_st[..., None]
        dmat = jnp.where(tri, b[..., :, None] - b[..., None, :] + li[..., None, :], -jnp.inf)
        m_row = jnp.maximum(a, jnp.max(dmat, axis=-1))
        w_intra = jnp.exp(dmat - m_row[..., None])
        w_inter = jnp.exp(a - m_row)
        s = jnp.einsum('bhtd,bhsd->bhts', qc, kc) * w_intra
        num = jnp.einsum('bhts,bhsv->bhtv', s, vc) + w_inter[..., None] * jnp.einsum('bhtd,bhdv->bhtv', qc, c_st)
        den = jnp.sum(s, axis=-1) + w_inter * jnp.einsum('bhtd,bhd->bht', qc, n_st)
        h = num / jnp.maximum(jnp.abs(den), jnp.exp(-m_row))[..., None]
        b_last = b[..., -1]
        g = b_last[..., None] - b + li
        m_new = jnp.maximum(b_last + m_st, jnp.max(g, axis=-1))
        decay = jnp.exp(b_last + m_st - m_new)
        wk = jnp.exp(g - m_new[..., None])
        c_new = decay[..., None, None] * c_st + jnp.einsum('bhs,bhsd,bhsv->bhdv', wk, kc, vc)
        n_new = decay[..., None] * n_st + jnp.einsum('bhs,bhsd->bhd', wk, kc)
        return (c_new, n_new, m_new), h

    state, hs = lax.scan(step, state, xs)
    h = jnp.moveaxis(hs, 0, 2).reshape(bsz, n_head, n_tok, dv)
    return h, state


def mlstm_out(h, o, norm_w, w_out, dtype):
    bsz, n_head, n_tok, dv = h.shape
    hn = h * lax.rsqrt(jnp.mean(h * h, axis=-1, keepdims=True) + EPS)
    hn = hn.transpose(0, 2, 1, 3).reshape(bsz, n_tok, n_head * dv)
    y = hn * norm_w.astype(jnp.float32) * jax.nn.sigmoid(o.astype(jnp.float32))
    return y.astype(dtype) @ w_out


def rev(a):
    return jnp.flip(a, axis=2)


def mlstm_mixer(hl, hc, w_in, gate_b, norm_w, w_out, need_ctx):
    ql, kl, vl, ol, (lif, lff, lib, lfb) = mlstm_project(hl, w_in, gate_b)
    qc, kc, vc, oc, (cif, cff, cib, cfb) = mlstm_project(hc, w_in, gate_b)
    bsz = hl.shape[0]
    state0 = (jnp.zeros((bsz, MLSTM_HEADS, MLSTM_DQK, MLSTM_DV), jnp.float32),
              jnp.zeros((bsz, MLSTM_HEADS, MLSTM_DQK), jnp.float32),
              jnp.full((bsz, MLSTM_HEADS), M_INIT, jnp.float32))
    hcf, st_f = mlstm_chunkwise(qc, kc, vc, cif, cff, state0)
    hlf, _ = mlstm_chunkwise(ql, kl, vl, lif, lff, st_f)
    hcb, st_b = mlstm_chunkwise(rev(qc), rev(kc), rev(vc), rev(cib), rev(cfb), state0)
    hlb, _ = mlstm_chunkwise(rev(ql), rev(kl), rev(vl), rev(lib), rev(lfb), st_b)
    yl = mlstm_out(hlf + rev(hlb), ol, norm_w, w_out, hl.dtype)
    yc = mlstm_out(hcf + rev(hcb), oc, norm_w, w_out, hc.dtype) if need_ctx else None
    return yl, yc


def moe(h, router_w, router_bias, w_gate, w_up, w_down, sh_gate, sh_up, sh_down):
    n_tok, d = h.shape
    scores = jax.nn.sigmoid((h @ router_w).astype(jnp.float32))
    _, idx = lax.top_k(scores + router_bias.astype(jnp.float32), TOP_K)
    sel = jnp.take_along_axis(scores, idx, axis=-1)
    gates = sel / jnp.sum(sel, axis=-1, keepdims=True) * ROUTED_SCALE
    n_assign = n_tok * TOP_K
    expert_ids = idx.reshape(n_assign)
    token_ids = jnp.repeat(jnp.arange(n_tok, dtype=jnp.int32), TOP_K)
    order = jnp.argsort(expert_ids)
    e_sorted = expert_ids[order]
    counts = jnp.bincount(expert_ids, length=N_EXPERTS)
    padded = (counts + MOE_BLOCK - 1) // MOE_BLOCK * MOE_BLOCK
    start = jnp.cumsum(counts) - counts
    pend = jnp.cumsum(padded)
    pstart = pend - padded
    dest = pstart[e_sorted] + jnp.arange(n_assign, dtype=jnp.int32) - start[e_sorted]
    n_blocks = (n_assign + N_EXPERTS * (MOE_BLOCK - 1) + MOE_BLOCK - 1) // MOE_BLOCK
    n_slots = n_blocks * MOE_BLOCK
    buf_tok = jnp.full((n_slots,), n_tok, jnp.int32).at[dest].set(token_ids[order])
    buf_gate = jnp.zeros((n_slots,), h.dtype).at[dest].set(gates.reshape(n_assign)[order].astype(h.dtype))
    blk_expert = jnp.minimum(
        jnp.searchsorted(pend, jnp.arange(n_blocks, dtype=jnp.int32) * MOE_BLOCK, side='right'), N_EXPERTS - 1)
    h_pad = jnp.concatenate([h, jnp.zeros((1, d), h.dtype)], axis=0)

    def expert_block(args):
        tok, g, e = args
        xb = h_pad[tok]
        a = jax.nn.silu(xb @ w_gate[e]) * (xb @ w_up[e])
        return (a @ w_down[e]) * g[:, None]

    yb = lax.map(expert_block, (buf_tok.reshape(n_blocks, MOE_BLOCK),
                                buf_gate.reshape(n_blocks, MOE_BLOCK), blk_expert))
    routed = jax.ops.segment_sum(yb.reshape(n_slots, d), buf_tok, num_segments=n_tok + 1)[:n_tok]
    shared = (jax.nn.silu(h @ sh_gate) * (h @ sh_up)) @ sh_down
    return routed + shared


def setup_inputs(seed: int = 0) -> dict:
    key = jax.random.key(seed)
    ks = jax.random.split(key, 24)
    d = D_MODEL
    f32 = jnp.float32
    n_a = (DEPTH + 1) // 2
    n_b = DEPTH // 2

    def nrm(k, shape, s):
        return jax.random.normal(k, shape, f32) * s

    gate_off = jnp.array([0.0, 3.0, 0.0, 3.0], f32)[None, :, None]
    return {
        'x': nrm(ks[0], (BATCH, SEQ, d), 1.0),
        'c': nrm(ks[1], (BATCH, d), 1.0),
        'ctx': nrm(ks[2], (BATCH, CTX_LEN, d), 1.0),
        'c_ctx': nrm(ks[3], (d,), 1.0),
        'ada_w': nrm(ks[4], (DEPTH, d, ADA_CHUNKS * d), 0.5 * d ** -0.5),
        'ada_b': nrm(ks[5], (DEPTH, ADA_CHUNKS * d), 0.02),
        'norm_mix_w': 1.0 + nrm(ks[6], (DEPTH, d), 0.02),
        'norm_ffn_w': 1.0 + nrm(ks[7], (DEPTH, d), 0.02),
        'conv_in_w': nrm(ks[8], (n_a, d, 3 * d), d ** -0.5),
        'conv_dw_w': nrm(ks[9], (n_a, 3, d), 3 ** -0.5),
        'conv_out_w': nrm(ks[10], (n_a, d, d), d ** -0.5),
        'mlstm_in_w': nrm(ks[11], (n_b, d, MLSTM_IN_W), d ** -0.5),
        'mlstm_gate_b': (gate_off + nrm(ks[12], (n_b, 4, MLSTM_HEADS), 0.3)).reshape(n_b, 4 * MLSTM_HEADS),
        'mlstm_norm_w': 1.0 + nrm(ks[13], (n_b, V_W), 0.02),
        'mlstm_out_w': nrm(ks[14], (n_b, V_W, d), V_W ** -0.5),
        'router_w': nrm(ks[15], (DEPTH, d, N_EXPERTS), d ** -0.5),
        'router_bias': nrm(ks[16], (DEPTH, N_EXPERTS), 0.01),
        'exp_gate_w': nrm(ks[17], (DEPTH, N_EXPERTS, d, EXPERT_DIM), d ** -0.5),
        'exp_up_w': nrm(ks[18], (DEPTH, N_EXPERTS, d, EXPERT_DIM), d ** -0.5),
        'exp_down_w': nrm(ks[19], (DEPTH, N_EXPERTS, EXPERT_DIM, d), EXPERT_DIM ** -0.5),
        'shared_gate_w': nrm(ks[20], (DEPTH, d, EXPERT_DIM), d ** -0.5),
        'shared_up_w': nrm(ks[21], (DEPTH, d, EXPERT_DIM), d ** -0.5),
        'shared_down_w': nrm(ks[22], (DEPTH, EXPERT_DIM, d), EXPERT_DIM ** -0.5),
        'final_norm_w': 1.0 + nrm(ks[23], (d,), 0.02),
    }


def reference(x, c, ctx, c_ctx, ada_w, ada_b, norm_mix_w, norm_ffn_w, conv_in_w, conv_dw_w, conv_out_w,
              mlstm_in_w, mlstm_gate_b, mlstm_norm_w, mlstm_out_w, router_w, router_bias,
              exp_gate_w, exp_up_w, exp_down_w, shared_gate_w, shared_up_w, shared_down_w, final_norm_w):
    d = x.shape[-1]
    bsz, n_ctx = ctx.shape[0], ctx.shape[1]
    xl, xc = x, ctx
    for i in range(DEPTH):
        last = i == DEPTH - 1
        j = i // N_MIXERS
        sh1l, sc1l, g1l, sh2l, sc2l, g2l = [m[:, None, :] for m in ada_mod(c, ada_w[i], ada_b[i])]
        sh1c, sc1c, g1c, sh2c, sc2c, g2c = ada_mod(c_ctx, ada_w[i], ada_b[i])
        hl = modulate(rmsnorm(xl, norm_mix_w[i]), sh1l, sc1l)
        hc = modulate(rmsnorm(xc, norm_mix_w[i]), sh1c, sc1c)
        if i % N_MIXERS == 0:
            yl = conv_mixer(hl, conv_in_w[j], conv_dw_w[j], conv_out_w[j], True)
            yc = None if last else conv_mixer(hc, conv_in_w[j], conv_dw_w[j], conv_out_w[j], False)
        else:
            yl, yc = mlstm_mixer(hl, hc, mlstm_in_w[j], mlstm_gate_b[j], mlstm_norm_w[j], mlstm_out_w[j], not last)
        xl = xl + g1l * yl
        hl2 = modulate(rmsnorm(xl, norm_ffn_w[i]), sh2l, sc2l)
        moe_w = (router_w[i], router_bias[i], exp_gate_w[i], exp_up_w[i], exp_down_w[i],
                 shared_gate_w[i], shared_up_w[i], shared_down_w[i])
        if last:
            xl = xl + g2l * moe(hl2.reshape(-1, d), *moe_w).reshape(xl.shape)
        else:
            xc = xc + g1c * yc
            hc2 = modulate(rmsnorm(xc, norm_ffn_w[i]), sh2c, sc2c)
            f = moe(jnp.concatenate([hc2.reshape(-1, d), hl2.reshape(-1, d)], axis=0), *moe_w)
            n_c = bsz * n_ctx
            xc = xc + g2c * f[:n_c].reshape(xc.shape)
            xl = xl + g2l * f[n_c:].reshape(xl.shape)
    return rmsnorm(xl, final_norm_w)
```

```python
import functools

import jax
import jax.numpy as jnp
from jax import lax
from jax.experimental import pallas as pl
from jax.experimental.pallas import tpu as pltpu

F32 = jnp.float32
BF16 = jnp.bfloat16
I32 = jnp.int32

TILE = 256
GRID_W = 64
CHUNK = 64
TOP_K = 6
MOE_BLK = 256
IDX_W = 8
ADA_CHUNKS = 6
ADA_ROWS = 16
EPS = 1e-6
GATE_CAP = 15.0
M_INIT = -1e30
ROUTED_SCALE = 2.5
V7X_VMEM_LIMIT = 56 * 1024 * 1024


def _cparams(*sem):
    return pltpu.CompilerParams(dimension_semantics=sem, vmem_limit_bytes=V7X_VMEM_LIMIT)


def _resident(shape, index_map):
    return pl.BlockSpec(shape, index_map, pipeline_mode=pl.Buffered(1))


def _sigmoid(x):
    return 1.0 / (1.0 + jnp.exp(-x))


def _silu(x):
    return x * _sigmoid(x)


def _split3(a):
    hi = a.astype(BF16)
    r1 = a - hi.astype(F32)
    mid = r1.astype(BF16)
    lo = (r1 - mid.astype(F32)).astype(BF16)
    return hi, mid, lo


def _dot(a, b):
    return jnp.dot(a, b, preferred_element_type=F32)


def _dot_hp(a, b):
    a0, a1, a2 = _split3(a)
    b0, b1, b2 = _split3(b)
    return (_dot(a0, b0) + (_dot(a0, b1) + _dot(a1, b0))
            + (_dot(a0, b2) + _dot(a1, b1) + _dot(a2, b0)))


def _norm_mod(x, w, shift, scale):
    y = x * lax.rsqrt(jnp.mean(x * x, axis=-1, keepdims=True) + EPS)
    return (y * w) * (1.0 + scale) + shift


def _ada_kernel(cond_ref, w_ref, b_ref, o_ref):
    a = _silu(cond_ref[...]).astype(BF16)
    o_ref[0] = _dot(a, w_ref[0].astype(BF16)) + b_ref[0]


def _ada_mod(cond, ada_w, ada_b):
    n_layer, d, n_out = ada_w.shape
    tn = 1024 if n_out % 1024 == 0 else n_out
    return pl.pallas_call(
        _ada_kernel,
        grid=(n_layer, n_out // tn),
        in_specs=[
            pl.BlockSpec((ADA_ROWS, d), lambda l, j: (0, 0)),
            pl.BlockSpec((1, d, tn), lambda l, j: (l, 0, j)),
            pl.BlockSpec((1, 1, tn), lambda l, j: (l, 0, j)),
        ],
        out_specs=pl.BlockSpec((1, ADA_ROWS, tn), lambda l, j: (l, 0, j)),
        out_shape=jax.ShapeDtypeStruct((n_layer, ADA_ROWS, n_out), F32),
        compiler_params=_cparams("parallel", "parallel"),
        name="ada_mod",
    )(cond, ada_w, ada_b.reshape(n_layer, 1, n_out))


def _conv_in_kernel(x_ref, mod_ref, nw_ref, win_ref, wdw_ref, z_ref, *, d, cn, tiles_per_b):
    i = pl.program_id(0)
    mod = mod_ref[0]
    h = _norm_mod(x_ref[...], nw_ref[...], mod[:, 0:d], mod[:, d:2 * d]).astype(BF16)
    t = lax.broadcasted_iota(I32, (TILE, 1), 0)
    pos_mask = jnp.where((i % tiles_per_b) == 0, TILE - 1, GRID_W - 1)
    pos = jnp.bitwise_and(t, pos_mask)
    first = pos == 0
    last = pos == pos_mask
    for j in range(d // cn):
        c0 = j * cn
        bg = _dot(h, win_ref[:, c0:c0 + cn])
        cg = _dot(h, win_ref[:, d + c0:d + c0 + cn])
        hi = _dot(h, win_ref[:, 2 * d + c0:2 * d + c0 + cn])
        u = cg * hi
        u_prev = jnp.where(first, 0.0, pltpu.roll(u, 1, 0))
        u_next = jnp.where(last, 0.0, pltpu.roll(u, TILE - 1, 0))
        w = wdw_ref[:, c0:c0 + cn]
        y = u_prev * w[0:1] + u * w[1:2] + u_next * w[2:3]
        z_ref[:, c0:c0 + cn] = (bg * y).astype(BF16)


def _conv_in(xt, mod, nw, w_in, w_dw, *, n_batch, tiles_per_b):
    n_tok, d = xt.shape
    n_tiles = n_tok // TILE
    cn = min(512, d)

    def mod_map(i):
        return (jnp.where(i % tiles_per_b == 0, n_batch, i // tiles_per_b), 0, 0)

    return pl.pallas_call(
        functools.partial(_conv_in_kernel, d=d, cn=cn, tiles_per_b=tiles_per_b),
        grid=(n_tiles,),
        in_specs=[
            pl.BlockSpec((TILE, d), lambda i: (i, 0)),
            pl.BlockSpec((1, 1, ADA_CHUNKS * d), mod_map),
            pl.BlockSpec((1, d), lambda i: (0, 0)),
            _resident((d, 3 * d), lambda i: (0, 0)),
            pl.BlockSpec((3, d), lambda i: (0, 0)),
        ],
        out_specs=pl.BlockSpec((TILE, d), lambda i: (i, 0)),
        out_shape=jax.ShapeDtypeStruct((n_tok, d), BF16),
        compiler_params=_cparams("parallel"),
        name="conv_in",
    )(xt, mod, nw, w_in, w_dw)


def _post_kernel(z_ref, x_ref, mod_ref, nw_ref, wout_ref, rw_ref, rb_ref,
                 xn_ref, h2_ref, eidx_ref, gate_ref, rank_ref, cnt_ref, carry_ref, *, d, n_exp):
    i = pl.program_id(0)

    @pl.when(i == 0)
    def _():
        carry_ref[...] = jnp.zeros_like(carry_ref)

    mod = mod_ref[0]
    y = _dot(z_ref[...], wout_ref[...])
    xn = x_ref[...] + mod[:, 2 * d:3 * d] * y
    xn_ref[...] = xn
    h2 = _norm_mod(xn, nw_ref[...], mod[:, 3 * d:4 * d], mod[:, 4 * d:5 * d])
    h2_ref[...] = h2

    scores = _sigmoid(_dot_hp(h2, rw_ref[...]))
    lane = lax.broadcasted_iota(I32, (TILE, n_exp), 1)
    lane_f = lane.astype(F32)
    work = scores + rb_ref[...]
    onehots, picks = [], []
    for _ in range(TOP_K):
        mx = jnp.max(work, axis=1, keepdims=True)
        first_max = jnp.min(jnp.where(work == mx, lane_f, float(n_exp)), axis=1, keepdims=True)
        oh = lane_f == first_max
        onehots.append(oh)
        picks.append(first_max)
        work = jnp.where(oh, -jnp.inf, work)
    sel = onehots[0]
    for oh in onehots[1:]:
        sel = jnp.logical_or(sel, oh)
    picked = jnp.where(sel, scores, 0.0)
    gates = picked / jnp.sum(picked, axis=1, keepdims=True) * ROUTED_SCALE
    sel_f = jnp.where(sel, 1.0, 0.0)
    r_i = lax.broadcasted_iota(I32, (TILE, TILE), 0)
    c_i = lax.broadcasted_iota(I32, (TILE, TILE), 1)
    before = jnp.where(c_i < r_i, 1.0, 0.0).astype(BF16)
    cum = _dot(before, sel_f.astype(BF16)) + carry_ref[...]

    eidx_ref[...] = jnp.zeros_like(eidx_ref)
    gate_ref[...] = jnp.zeros_like(gate_ref)
    rank_ref[...] = jnp.zeros_like(rank_ref)
    for k, oh in enumerate(onehots):
        eidx_ref[:, k:k + 1] = picks[k].astype(I32)
        gate_ref[:, k:k + 1] = jnp.sum(jnp.where(oh, gates, 0.0), axis=1, keepdims=True)
        rank_ref[:, k:k + 1] = jnp.sum(jnp.where(oh, cum, 0.0), axis=1, keepdims=True).astype(I32)

    total = carry_ref[...] + jnp.sum(sel_f, axis=0, keepdims=True)
    carry_ref[...] = total
    cnt_ref[...] = total


def _post(z, xt, mod, nw, w_out, router_w, router_b, *, x_tile_map, mod_row_map):
    n_tok, d = z.shape
    n_tiles = n_tok // TILE
    n_exp = router_w.shape[1]
    outs = pl.pallas_call(
        functools.partial(_post_kernel, d=d, n_exp=n_exp),
        grid=(n_tiles,),
        in_specs=[
            pl.BlockSpec((TILE, d), lambda i: (i, 0)),
            pl.BlockSpec((TILE, d), lambda i: (x_tile_map(i), 0)),
            pl.BlockSpec((1, 1, ADA_CHUNKS * d), lambda i: (mod_row_map(i), 0, 0)),
            pl.BlockSpec((1, d), lambda i: (0, 0)),
            _resident((d, d), lambda i: (0, 0)),
            pl.BlockSpec((d, n_exp), lambda i: (0, 0)),
            pl.BlockSpec((1, n_exp), lambda i: (0, 0)),
        ],
        out_specs=[
            pl.BlockSpec((TILE, d), lambda i: (i, 0)),
            pl.BlockSpec((TILE, d), lambda i: (i, 0)),
            pl.BlockSpec((TILE, IDX_W), lambda i: (i, 0)),
            pl.BlockSpec((TILE, IDX_W), lambda i: (i, 0)),
            pl.BlockSpec((TILE, IDX_W), lambda i: (i, 0)),
            pl.BlockSpec((1, n_exp), lambda i: (0, 0)),
        ],
        out_shape=[
            jax.ShapeDtypeStruct((n_tok, d), F32),
            jax.ShapeDtypeStruct((n_tok, d), F32),
            jax.ShapeDtypeStruct((n_tok, IDX_W), I32),
            jax.ShapeDtypeStruct((n_tok, IDX_W), F32),
            jax.ShapeDtypeStruct((n_tok, IDX_W), I32),
            jax.ShapeDtypeStruct((1, n_exp), F32),
        ],
        scratch_shapes=[pltpu.VMEM((1, n_exp), F32)],
        compiler_params=_cparams("arbitrary"),
        name="post_mixer",
    )(z, xt, mod, nw, w_out, router_w, router_b)
    return outs


def _dispatch_kernel(slot_ref, h2_ref, xs_in_ref, xs_ref, sem):
    del xs_in_ref

    def row_copy(t, k):
        s = slot_ref[0, 0, t * TOP_K + k]
        return pltpu.make_async_copy(h2_ref.at[pl.ds(t, 1)], xs_ref.at[pl.ds(s, 1)], sem)

    def issue(t, c):
        for k in range(TOP_K):
            row_copy(t, k).start()
        return c

    def drain(t, c):
        for k in range(TOP_K):
            row_copy(t, k).wait()
        return c

    lax.fori_loop(0, TILE, issue, 0)
    lax.fori_loop(0, TILE, drain, 0)


def _dispatch(slots, h2, n_slots):
    n_tok, d = h2.shape
    n_tiles = n_tok // TILE
    xs0 = jnp.zeros((n_slots, d), F32)
    return pl.pallas_call(
        _dispatch_kernel,
        grid=(n_tiles,),
        in_specs=[
            pl.BlockSpec((1, 1, TILE * TOP_K), lambda i: (i, 0, 0), memory_space=pltpu.SMEM),
            pl.BlockSpec((TILE, d), lambda i: (i, 0)),
            pl.BlockSpec(memory_space=pl.ANY),
        ],
        out_specs=pl.BlockSpec(memory_space=pl.ANY),
        out_shape=jax.ShapeDtypeStruct((n_slots, d), F32),
        scratch_shapes=[pltpu.SemaphoreType.DMA(())],
        input_output_aliases={2: 0},
        compiler_params=_cparams("arbitrary"),
        name="moe_dispatch",
    )(slots.reshape(n_tiles, 1, TILE * TOP_K), h2, xs0)


def _grouped_kernel(be_ref, meta_ref, xs_ref, wg_ref, wu_ref, wd_ref, ys_ref):
    del be_ref
    used = pl.program_id(0) < meta_ref[0]

    @pl.when(used)
    def _():
        xb = xs_ref[...].astype(BF16)
        a = _silu(_dot(xb, wg_ref[0])) * _dot(xb, wu_ref[0])
        ys_ref[...] = _dot(a.astype(BF16), wd_ref[0])

    @pl.when(jnp.logical_not(used))
    def _():
        ys_ref[...] = jnp.zeros_like(ys_ref)


def _grouped(blk_expert, n_used, xs, w_gate, w_up, w_down):
    n_slots, d = xs.shape
    n_blk = n_slots // MOE_BLK
    f = w_gate.shape[2]

    def row_map(i, be, meta):
        return (jnp.minimum(i, jnp.maximum(meta[0] - 1, 0)), 0)

    grid_spec = pltpu.PrefetchScalarGridSpec(
        num_scalar_prefetch=2,
        grid=(n_blk,),
        in_specs=[
            pl.BlockSpec((MOE_BLK, d), row_map),
            pl.BlockSpec((1, d, f), lambda i, be, meta: (be[i], 0, 0)),
            pl.BlockSpec((1, d, f), lambda i, be, meta: (be[i], 0, 0)),
            pl.BlockSpec((1, f, d), lambda i, be, meta: (be[i], 0, 0)),
        ],
        out_specs=pl.BlockSpec((MOE_BLK, d), lambda i, be, meta: (i, 0)),
    )
    return pl.pallas_call(
        _grouped_kernel,
        grid_spec=grid_spec,
        out_shape=jax.ShapeDtypeStruct((n_slots, d), F32),
        compiler_params=_cparams("arbitrary"),
        name="moe_experts",
    )(blk_expert, n_used, xs, w_gate, w_up, w_down)


def _combine_kernel(slot_ref, gate_ref, h2_ref, xn_ref, mod_ref, sg_ref, su_ref, sd_ref, fw_ref, ys_ref,
                    o_ref, gbuf, sem, *, d, final):
    def row_copy(t, k):
        s = slot_ref[0, 0, t * TOP_K + k]
        return pltpu.make_async_copy(ys_ref.at[pl.ds(s, 1)], gbuf.at[k, pl.ds(t, 1)], sem)

    def issue(t, c):
        for k in range(TOP_K):
            row_copy(t, k).start()
        return c

    def drain(t, c):
        for k in range(TOP_K):
            row_copy(t, k).wait()
        return c

    lax.fori_loop(0, TILE, issue, 0)
    hb = h2_ref[...].astype(BF16)
    a = _silu(_dot(hb, sg_ref[...])) * _dot(hb, su_ref[...])
    acc = _dot(a.astype(BF16), sd_ref[...])
    lax.fori_loop(0, TILE, drain, 0)
    gate = gate_ref[...]
    for k in range(TOP_K):
        acc = acc + gate[:, k:k + 1] * gbuf[k]
    x2 = xn_ref[...] + mod_ref[0][:, 5 * d:6 * d] * acc
    if final:
        x2 = x2 * lax.rsqrt(jnp.mean(x2 * x2, axis=-1, keepdims=True) + EPS) * fw_ref[...]
    o_ref[...] = x2


def _combine(slots, gates, h2, xn, mod, sh_gate, sh_up, sh_down, fw, ys, *, mod_row_map, final):
    n_tok, d = h2.shape
    n_tiles = n_tok // TILE
    f = sh_gate.shape[1]
    return pl.pallas_call(
        functools.partial(_combine_kernel, d=d, final=final),
        grid=(n_tiles,),
        in_specs=[
            pl.BlockSpec((1, 1, TILE * TOP_K), lambda i: (i, 0, 0), memory_space=pltpu.SMEM),
            pl.BlockSpec((TILE, IDX_W), lambda i: (i, 0)),
            pl.BlockSpec((TILE, d), lambda i: (i, 0)),
            pl.BlockSpec((TILE, d), lambda i: (i, 0)),
            pl.BlockSpec((1, 1, ADA_CHUNKS * d), lambda i: (mod_row_map(i), 0, 0)),
            pl.BlockSpec((d, f), lambda i: (0, 0)),
            pl.BlockSpec((d, f), lambda i: (0, 0)),
            pl.BlockSpec((f, d), lambda i: (0, 0)),
            pl.BlockSpec((1, d), lambda i: (0, 0)),
            pl.BlockSpec(memory_space=pl.ANY),
        ],
        out_specs=pl.BlockSpec((TILE, d), lambda i: (i, 0)),
        out_shape=jax.ShapeDtypeStruct((n_tok, d), F32),
        scratch_shapes=[pltpu.VMEM((TOP_K, TILE, d), F32), pltpu.SemaphoreType.DMA(())],
        compiler_params=_cparams("arbitrary"),
        name="moe_combine",
    )(slots.reshape(n_tiles, 1, TILE * TOP_K), gates, h2, xn, mod, sh_gate, sh_up, sh_down, fw, ys)


def _moe(h2, xn, eidx, gates, rank, counts, mod, w_gate, w_up, w_down, sh_gate, sh_up, sh_down, fw,
         *, mod_row_map, final):
    n_tok = h2.shape[0]
    n_exp = w_gate.shape[0]
    n_blk = (n_tok * TOP_K + n_exp * (MOE_BLK - 1) + MOE_BLK - 1) // MOE_BLK
    cnt = counts[0].astype(I32)
    padded = (cnt + MOE_BLK - 1) // MOE_BLK * MOE_BLK
    pend = jnp.cumsum(padded)
    pstart = pend - padded
    slots = pstart[eidx[:, :TOP_K]] + rank[:, :TOP_K]
    n_used = pend[-1] // MOE_BLK
    blk = jnp.arange(n_blk, dtype=I32)
    be = jnp.minimum(jnp.searchsorted(pend, blk * MOE_BLK, side="right"), n_exp - 1).astype(I32)
    be = jnp.where(blk < n_used, be, be[jnp.maximum(n_used - 1, 0)])
    xs = _dispatch(slots, h2, n_blk * MOE_BLK)
    ys = _grouped(be, n_used.reshape(1).astype(I32), xs, w_gate, w_up, w_down)
    return _combine(slots, gates, h2, xn, mod, sh_gate, sh_up, sh_down, fw, ys,
                    mod_row_map=mod_row_map, final=final)


def _log_sigmoid(x):
    return jnp.minimum(x, 0.0) - jnp.log1p(jnp.exp(-jnp.abs(x)))


def _mlstm_in_kernel(x_ref, mod_ref, nw_ref, w_ref, wg_ref, gb_ref, p_ref, g_ref, *, d, cn, qk_w, n_head, k_scale):
    mod = mod_ref[0]
    hf = _norm_mod(x_ref[...], nw_ref[...], mod[:, 0:d], mod[:, d:2 * d])
    h = hf.astype(BF16)
    for j in range(3 * d // cn):
        c0 = j * cn
        p = _dot(h, w_ref[:, c0:c0 + cn])
        if qk_w <= c0 < 2 * qk_w:
            p = p * k_scale
        p_ref[:, c0:c0 + cn] = p.astype(BF16)
    g = _dot_hp(hf, wg_ref[...]) + gb_ref[...]
    g = GATE_CAP * jnp.tanh(g / GATE_CAP)
    col = lax.broadcasted_iota(I32, g.shape, 1)
    is_forget = (col // n_head) % 2 == 1
    g_ref[...] = jnp.where(is_forget, _log_sigmoid(g), g)


def _mlstm_in(xt, mod, nw, w_qkvo, w_g, gate_b, *, n_batch, tiles_per_b, n_head):
    n_tok, d = xt.shape
    n_tiles = n_tok // TILE
    qk_w = d // 2
    cn = min(512, qk_w)
    n_g = w_g.shape[1]
    k_scale = float((qk_w // n_head) ** -0.5)

    def mod_map(i):
        return (jnp.where(i % tiles_per_b == 0, n_batch, i // tiles_per_b), 0, 0)

    return pl.pallas_call(
        functools.partial(_mlstm_in_kernel, d=d, cn=cn, qk_w=qk_w, n_head=n_head, k_scale=k_scale),
        grid=(n_tiles,),
        in_specs=[
            pl.BlockSpec((TILE, d), lambda i: (i, 0)),
            pl.BlockSpec((1, 1, ADA_CHUNKS * d), mod_map),
            pl.BlockSpec((1, d), lambda i: (0, 0)),
            _resident((d, 3 * d), lambda i: (0, 0)),
            pl.BlockSpec((d, n_g), lambda i: (0, 0)),
            pl.BlockSpec((1, n_g), lambda i: (0, 0)),
        ],
        out_specs=[
            pl.BlockSpec((TILE, 3 * d), lambda i: (i, 0)),
            pl.BlockSpec((TILE, n_g), lambda i: (i, 0)),
        ],
        out_shape=[
            jax.ShapeDtypeStruct((n_tok, 3 * d), BF16),
            jax.ShapeDtypeStruct((n_tok, n_g), F32),
        ],
        compiler_params=_cparams("parallel"),
        name="mlstm_in",
    )(xt, mod, nw, w_qkvo, w_g, gate_b)


def _mlstm_chunk(q, k, v, li_r, lf_r, c_st, n_st, m_st, *, backward):
    n_t = q.shape[0]
    tt = lax.broadcasted_iota(I32, (n_t, n_t), 0)
    ss = lax.broadcasted_iota(I32, (n_t, n_t), 1)
    diag = tt == ss
    if backward:
        seen = ss >= tt
        seen_t = tt >= ss
    else:
        seen = ss <= tt
        seen_t = tt <= ss
    lf_b = jnp.broadcast_to(lf_r, (n_t, n_t))
    li_b = jnp.broadcast_to(li_r, (n_t, n_t))
    b_c = jnp.sum(jnp.where(seen, lf_b, 0.0), axis=1, keepdims=True)
    lf_c = jnp.sum(jnp.where(diag, lf_b, 0.0), axis=1, keepdims=True)
    li_c = jnp.sum(jnp.where(diag, li_b, 0.0), axis=1, keepdims=True)
    b_r = jnp.sum(jnp.where(seen_t, lf_c, 0.0), axis=0, keepdims=True)
    b_last = jnp.sum(lf_r, axis=1, keepdims=True)
    dmat = jnp.where(seen, b_c - b_r + li_b, -jnp.inf)
    a = b_c + m_st
    m_row = jnp.maximum(a, jnp.max(dmat, axis=1, keepdims=True))
    w_intra = jnp.exp(dmat - m_row)
    w_inter = jnp.exp(a - m_row)
    s = lax.dot_general(q, k, (((1,), (1,)), ((), ())), preferred_element_type=F32) * w_intra
    inter = _dot(q, c_st.astype(BF16))
    num = _dot(s.astype(BF16), v) + w_inter * inter
    qn = jnp.sum(q.astype(F32) * n_st, axis=1, keepdims=True)
    den = jnp.sum(s, axis=1, keepdims=True) + w_inter * qn
    h = num / jnp.maximum(jnp.abs(den), jnp.exp(-m_row))
    g = b_last - b_c + li_c
    m_new = jnp.maximum(b_last + m_st, jnp.max(g, axis=0, keepdims=True))
    decay = jnp.exp(b_last + m_st - m_new)
    kw = k.astype(F32) * jnp.exp(g - m_new)
    c_new = decay * c_st + lax.dot_general(kw.astype(BF16), v, (((0,), (0,)), ((), ())),
                                           preferred_element_type=F32)
    n_new = decay * n_st + jnp.sum(kw, axis=0, keepdims=True)
    return h, c_new, n_new, m_new


def _mlstm_scan_kernel(q_ref, k_ref, v_ref, o_ref, gr_ref, nw_ref, z_ref, hf_ref, hb_ref, c_ref,
                       *, n_chunk, n_ctx_chunk, dqk, dv):
    c_ref[...] = jnp.zeros_like(c_ref)
    n0 = jnp.zeros((1, dqk), F32)
    m0 = jnp.full((1, 1), M_INIT, F32)

    def step(i, carry):
        n_f, m_f, n_b, m_b = carry
        jf = i
        jb = jnp.where(i < n_ctx_chunk, n_ctx_chunk - 1 - i, n_chunk - 1 - (i - n_ctx_chunk))
        rf = pl.ds(pl.multiple_of(jf * CHUNK, CHUNK), CHUNK)
        rb = pl.ds(pl.multiple_of(jb * CHUNK, CHUNK), CHUNK)
        h_f, c_f, n_f, m_f = _mlstm_chunk(q_ref[rf, :], k_ref[rf, :], v_ref[rf, :],
                                          gr_ref[0, 0, 0, jf], gr_ref[0, 1, 0, jf],
                                          c_ref[0], n_f, m_f, backward=False)
        c_ref[0] = c_f
        hf_ref[rf, :] = h_f
        h_b, c_b, n_b, m_b = _mlstm_chunk(q_ref[rb, :], k_ref[rb, :], v_ref[rb, :],
                                          gr_ref[0, 2, 0, jb], gr_ref[0, 3, 0, jb],
                                          c_ref[1], n_b, m_b, backward=True)
        c_ref[1] = c_b
        hb_ref[rb, :] = h_b
        return n_f, m_f, n_b, m_b

    lax.fori_loop(0, n_chunk, step, (n0, m0, n0, m0))
    n_ctx = n_ctx_chunk * CHUNK
    n_lat = (n_chunk - n_ctx_chunk) * CHUNK
    lat = pl.ds(n_ctx, n_lat)
    h = hf_ref[lat, :] + hb_ref[lat, :]
    hn = h * lax.rsqrt(jnp.mean(h * h, axis=-1, keepdims=True) + EPS)
    y = hn * nw_ref[...] * _sigmoid(o_ref[lat, :].astype(F32))
    z_ref[...] = y.astype(BF16)


def _mlstm_scan(p, gr, norm_w, *, n_batch, n_head, seq_all, n_ctx, d):
    dqk = d // (2 * n_head)
    dv = d // n_head
    n_chunk = seq_all // CHUNK
    n_lat = seq_all - n_ctx
    qk_blocks = (d // 2) // dqk
    v_blocks = d // dv
    return pl.pallas_call(
        functools.partial(_mlstm_scan_kernel, n_chunk=n_chunk, n_ctx_chunk=n_ctx // CHUNK, dqk=dqk, dv=dv),
        grid=(n_batch, n_head),
        in_specs=[
            pl.BlockSpec((seq_all, dqk), lambda b, h: (b, h)),
            pl.BlockSpec((seq_all, dqk), lambda b, h: (b, qk_blocks + h)),
            pl.BlockSpec((seq_all, dv), lambda b, h: (b, v_blocks + h)),
            pl.BlockSpec((seq_all, dv), lambda b, h: (b, 2 * v_blocks + h)),
            pl.BlockSpec((1, 4, 1, n_chunk, 1, CHUNK), lambda b, h: (b, 0, h, 0, 0, 0)),
            pl.BlockSpec((1, dv), lambda b, h: (0, h)),
        ],
        out_specs=pl.BlockSpec((n_lat, dv), lambda b, h: (b, h)),
        out_shape=jax.ShapeDtypeStruct((n_batch * n_lat, d), BF16),
        scratch_shapes=[
            pltpu.VMEM((seq_all, dv), F32),
            pltpu.VMEM((seq_all, dv), F32),
            pltpu.VMEM((2, dqk, dv), F32),
        ],
        compiler_params=_cparams("parallel", "parallel"),
        name="mlstm_scan",
    )(p, p, p, p, gr, norm_w)


def kernel(x, c, ctx, c_ctx, ada_w, ada_b, norm_mix_w, norm_ffn_w, conv_in_w, conv_dw_w, conv_out_w,
           mlstm_in_w, mlstm_gate_b, mlstm_norm_w, mlstm_out_w, router_w, router_bias,
           exp_gate_w, exp_up_w, exp_down_w, shared_gate_w, shared_up_w, shared_down_w, final_norm_w):
    n_batch, seq, d = x.shape
    n_ctx = ctx.shape[1]
    assert ada_w.shape[0] == 2 and n_ctx == TILE and seq % TILE == 0 and n_batch + 1 <= ADA_ROWS
    seq_all = n_ctx + seq
    tiles_per_b = seq_all // TILE
    lat_tiles_per_b = seq // TILE
    n_head = (mlstm_in_w.shape[2] - 3 * d) // 4

    cond = jnp.zeros((ADA_ROWS, d), F32).at[:n_batch].set(c).at[n_batch].set(c_ctx)
    mod = _ada_mod(cond, ada_w, ada_b)
    mod0 = mod[0].reshape(ADA_ROWS, 1, ADA_CHUNKS * d)
    mod1 = mod[1].reshape(ADA_ROWS, 1, ADA_CHUNKS * d)

    def all_mod_row(i):
        return jnp.where(i % tiles_per_b == 0, n_batch, i // tiles_per_b)

    def lat_mod_row(i):
        return i // lat_tiles_per_b

    def lat_tile(i):
        return (i // lat_tiles_per_b) * tiles_per_b + 1 + i % lat_tiles_per_b

    row = lambda w: w.reshape(1, -1)
    bf = lambda w: w.astype(BF16)
    x0 = jnp.concatenate([ctx, x], axis=1).reshape(n_batch * seq_all, d)

    z0 = _conv_in(x0, mod0, row(norm_mix_w[0]), bf(conv_in_w[0]), conv_dw_w[0],
                  n_batch=n_batch, tiles_per_b=tiles_per_b)
    xn0, h20, eidx0, gate0, rank0, cnt0 = _post(
        z0, x0, mod0, row(norm_ffn_w[0]), bf(conv_out_w[0]), router_w[0], row(router_bias[0]),
        x_tile_map=lambda i: i, mod_row_map=all_mod_row)
    x1 = _moe(h20, xn0, eidx0, gate0, rank0, cnt0, mod0, bf(exp_gate_w[0]), bf(exp_up_w[0]), bf(exp_down_w[0]),
              bf(shared_gate_w[0]), bf(shared_up_w[0]), bf(shared_down_w[0]), row(final_norm_w),
              mod_row_map=all_mod_row, final=False)

    w_in = mlstm_in_w[0]
    p, g = _mlstm_in(x1, mod1, row(norm_mix_w[1]), bf(w_in[:, :3 * d]), w_in[:, 3 * d:], row(mlstm_gate_b[0]),
                     n_batch=n_batch, tiles_per_b=tiles_per_b, n_head=n_head)
    n_chunk = seq_all // CHUNK
    gr = g.reshape(n_batch, n_chunk, CHUNK, 4, n_head).transpose(0, 3, 4, 1, 2)
    gr = gr.reshape(n_batch, 4, n_head, n_chunk, 1, CHUNK)
    z1 = _mlstm_scan(p, gr, row(mlstm_norm_w[0]), n_batch=n_batch, n_head=n_head, seq_all=seq_all,
                     n_ctx=n_ctx, d=d)
    xn1, h21, eidx1, gate1, rank1, cnt1 = _post(
        z1, x1, mod1, row(norm_ffn_w[1]), bf(mlstm_out_w[0]), router_w[1], row(router_bias[1]),
        x_tile_map=lat_tile, mod_row_map=lat_mod_row)
    out = _moe(h21, xn1, eidx1, gate1, rank1, cnt1, mod1, bf(exp_gate_w[1]), bf(exp_up_w[1]), bf(exp_down_w[1]),
               bf(shared_gate_w[1]), bf(shared_up_w[1]), bf(shared_down_w[1]), row(final_norm_w),
               mod_row_map=lat_mod_row, final=True)
    return out.reshape(n_batch, seq, d)
```

```python
import functools

import jax
import jax.numpy as jnp
from jax import lax
from jax.experimental import pallas as pl
from jax.experimental.pallas import tpu as pltpu

F32 = jnp.float32
BF16 = jnp.bfloat16
I32 = jnp.int32

TILE = 256
GRID_W = 64
CHUNK = 64
TOP_K = 6
MOE_BLK = 256
IDX_W = 8
SUBLANES = 8
ADA_CHUNKS = 6
ADA_ROWS = 16
EPS = 1e-6
GATE_CAP = 15.0
M_INIT = -1e30
ROUTED_SCALE = 2.5
V7X_VMEM_LIMIT = 56 * 1024 * 1024


def _cparams(*sem):
    return pltpu.CompilerParams(dimension_semantics=sem, vmem_limit_bytes=V7X_VMEM_LIMIT)


def _resident(shape, index_map):
    return pl.BlockSpec(shape, index_map, pipeline_mode=pl.Buffered(1))


def _sigmoid(x):
    return 1.0 / (1.0 + jnp.exp(-x))


def _silu(x):
    return x * _sigmoid(x)


def _split3(a):
    hi = a.astype(BF16)
    r1 = a - hi.astype(F32)
    mid = r1.astype(BF16)
    lo = (r1 - mid.astype(F32)).astype(BF16)
    return hi, mid, lo


def _dot(a, b):
    return jnp.dot(a, b, preferred_element_type=F32)


def _dot_hp(a, b):
    a0, a1, a2 = _split3(a)
    b0, b1, b2 = _split3(b)
    return (_dot(a0, b0) + (_dot(a0, b1) + _dot(a1, b0))
            + (_dot(a0, b2) + _dot(a1, b1) + _dot(a2, b0)))


def _norm_mod(x, w, shift, scale):
    y = x * lax.rsqrt(jnp.mean(x * x, axis=-1, keepdims=True) + EPS)
    return (y * w) * (1.0 + scale) + shift


def _ada_kernel(cond_ref, w_ref, b_ref, o_ref):
    a = _silu(cond_ref[...]).astype(BF16)
    o_ref[0] = _dot(a, w_ref[0].astype(BF16)) + b_ref[0]


def _ada_mod(cond, ada_w, ada_b):
    n_layer, d, n_out = ada_w.shape
    tn = 1024 if n_out % 1024 == 0 else n_out
    return pl.pallas_call(
        _ada_kernel,
        grid=(n_layer, n_out // tn),
        in_specs=[
            pl.BlockSpec((ADA_ROWS, d), lambda l, j: (0, 0)),
            pl.BlockSpec((1, d, tn), lambda l, j: (l, 0, j)),
            pl.BlockSpec((1, 1, tn), lambda l, j: (l, 0, j)),
        ],
        out_specs=pl.BlockSpec((1, ADA_ROWS, tn), lambda l, j: (l, 0, j)),
        out_shape=jax.ShapeDtypeStruct((n_layer, ADA_ROWS, n_out), F32),
        compiler_params=_cparams("parallel", "parallel"),
        name="ada_mod",
    )(cond, ada_w, ada_b.reshape(n_layer, 1, n_out))


def _conv_in_kernel(x_ref, mod_ref, nw_ref, win_ref, wdw_ref, z_ref, *, d, cn, tiles_per_b):
    i = pl.program_id(0)
    mod = mod_ref[0]
    h = _norm_mod(x_ref[...], nw_ref[...], mod[:, 0:d], mod[:, d:2 * d]).astype(BF16)
    t = lax.broadcasted_iota(I32, (TILE, 1), 0)
    pos_mask = jnp.where((i % tiles_per_b) == 0, TILE - 1, GRID_W - 1)
    pos = jnp.bitwise_and(t, pos_mask)
    first = pos == 0
    last = pos == pos_mask
    for j in range(d // cn):
        c0 = j * cn
        bg = _dot(h, win_ref[:, c0:c0 + cn])
        cg = _dot(h, win_ref[:, d + c0:d + c0 + cn])
        hi = _dot(h, win_ref[:, 2 * d + c0:2 * d + c0 + cn])
        u = cg * hi
        u_prev = jnp.where(first, 0.0, pltpu.roll(u, 1, 0))
        u_next = jnp.where(last, 0.0, pltpu.roll(u, TILE - 1, 0))
        w = wdw_ref[:, c0:c0 + cn]
        y = u_prev * w[0:1] + u * w[1:2] + u_next * w[2:3]
        z_ref[:, c0:c0 + cn] = (bg * y).astype(BF16)


def _conv_in(xt, mod, nw, w_in, w_dw, *, n_batch, tiles_per_b):
    n_tok, d = xt.shape
    n_tiles = n_tok // TILE
    cn = min(512, d)

    def mod_map(i):
        return (jnp.where(i % tiles_per_b == 0, n_batch, i // tiles_per_b), 0, 0)

    return pl.pallas_call(
        functools.partial(_conv_in_kernel, d=d, cn=cn, tiles_per_b=tiles_per_b),
        grid=(n_tiles,),
        in_specs=[
            pl.BlockSpec((TILE, d), lambda i: (i, 0)),
            pl.BlockSpec((1, 1, ADA_CHUNKS * d), mod_map),
            pl.BlockSpec((1, d), lambda i: (0, 0)),
            _resident((d, 3 * d), lambda i: (0, 0)),
            pl.BlockSpec((3, d), lambda i: (0, 0)),
        ],
        out_specs=pl.BlockSpec((TILE, d), lambda i: (i, 0)),
        out_shape=jax.ShapeDtypeStruct((n_tok, d), BF16),
        compiler_params=_cparams("parallel"),
        name="conv_in",
    )(xt, mod, nw, w_in, w_dw)


def _post_kernel(z_ref, x_ref, mod_ref, nw_ref, wout_ref, rwcat_ref, rwhi_ref, rb_ref,
                 xn_ref, h2_ref, eidx_ref, gate_ref, rank_ref, cnt_ref, carry_ref, *, d, n_exp):
    i = pl.program_id(0)

    @pl.when(i == 0)
    def _():
        carry_ref[...] = jnp.zeros_like(carry_ref)

    mod = mod_ref[0]
    y = _dot(z_ref[...], wout_ref[...])
    xn = x_ref[...] + mod[:, 2 * d:3 * d] * y
    xn_ref[...] = xn
    h2 = _norm_mod(xn, nw_ref[...], mod[:, 3 * d:4 * d], mod[:, 4 * d:5 * d])
    h2_ref[...] = h2

    h2_hi = h2.astype(BF16)
    h2_lo = (h2 - h2_hi.astype(F32)).astype(BF16)
    p_hi = _dot(h2_hi, rwcat_ref[...])
    logits = p_hi[:, :n_exp] + (p_hi[:, n_exp:] + _dot(h2_lo, rwhi_ref[...]))
    scores = _sigmoid(logits)
    lane = lax.broadcasted_iota(I32, (TILE, n_exp), 1)
    lane_f = lane.astype(F32)
    work = scores + rb_ref[...]
    onehots, picks = [], []
    for _ in range(TOP_K):
        mx = jnp.max(work, axis=1, keepdims=True)
        first_max = jnp.min(jnp.where(work == mx, lane_f, float(n_exp)), axis=1, keepdims=True)
        oh = lane_f == first_max
        onehots.append(oh)
        picks.append(first_max)
        work = jnp.where(oh, -jnp.inf, work)
    sel = onehots[0]
    for oh in onehots[1:]:
        sel = jnp.logical_or(sel, oh)
    picked = jnp.where(sel, scores, 0.0)
    gates = picked / jnp.sum(picked, axis=1, keepdims=True) * ROUTED_SCALE
    sel_f = jnp.where(sel, 1.0, 0.0)
    r_i = lax.broadcasted_iota(I32, (TILE, TILE), 0)
    c_i = lax.broadcasted_iota(I32, (TILE, TILE), 1)
    before = jnp.where(c_i < r_i, 1.0, 0.0).astype(BF16)
    cum = _dot(before, sel_f.astype(BF16)) + carry_ref[...]

    eidx_ref[...] = jnp.zeros_like(eidx_ref)
    gate_ref[...] = jnp.zeros_like(gate_ref)
    rank_ref[...] = jnp.zeros_like(rank_ref)
    for k, oh in enumerate(onehots):
        eidx_ref[:, k:k + 1] = picks[k].astype(I32)
        gate_ref[:, k:k + 1] = jnp.sum(jnp.where(oh, gates, 0.0), axis=1, keepdims=True)
        rank_ref[:, k:k + 1] = jnp.sum(jnp.where(oh, cum, 0.0), axis=1, keepdims=True).astype(I32)

    total = carry_ref[...] + jnp.sum(sel_f, axis=0, keepdims=True)
    carry_ref[...] = total
    cnt_ref[...] = total


def _post(z, xt, mod, nw, w_out, router_w, router_b, *, x_tile_map, mod_row_map):
    rw_hi = router_w.astype(BF16)
    rw_lo = (router_w - rw_hi.astype(F32)).astype(BF16)
    rw_cat = jnp.concatenate([rw_hi, rw_lo], axis=1)
    n_tok, d = z.shape
    n_tiles = n_tok // TILE
    n_exp = router_w.shape[1]
    outs = pl.pallas_call(
        functools.partial(_post_kernel, d=d, n_exp=n_exp),
        grid=(n_tiles,),
        in_specs=[
            pl.BlockSpec((TILE, d), lambda i: (i, 0)),
            pl.BlockSpec((TILE, d), lambda i: (x_tile_map(i), 0)),
            pl.BlockSpec((1, 1, ADA_CHUNKS * d), lambda i: (mod_row_map(i), 0, 0)),
            pl.BlockSpec((1, d), lambda i: (0, 0)),
            _resident((d, d), lambda i: (0, 0)),
            pl.BlockSpec((d, 2 * n_exp), lambda i: (0, 0)),
            pl.BlockSpec((d, n_exp), lambda i: (0, 0)),
            pl.BlockSpec((1, n_exp), lambda i: (0, 0)),
        ],
        out_specs=[
            pl.BlockSpec((TILE, d), lambda i: (i, 0)),
            pl.BlockSpec((TILE, d), lambda i: (i, 0)),
            pl.BlockSpec((TILE, IDX_W), lambda i: (i, 0)),
            pl.BlockSpec((TILE, IDX_W), lambda i: (i, 0)),
            pl.BlockSpec((TILE, IDX_W), lambda i: (i, 0)),
            pl.BlockSpec((1, n_exp), lambda i: (0, 0)),
        ],
        out_shape=[
            jax.ShapeDtypeStruct((n_tok, d), F32),
            jax.ShapeDtypeStruct((n_tok, d), F32),
            jax.ShapeDtypeStruct((n_tok, IDX_W), I32),
            jax.ShapeDtypeStruct((n_tok, IDX_W), F32),
            jax.ShapeDtypeStruct((n_tok, IDX_W), I32),
            jax.ShapeDtypeStruct((1, n_exp), F32),
        ],
        scratch_shapes=[pltpu.VMEM((1, n_exp), F32)],
        compiler_params=_cparams("arbitrary"),
        name="post_mixer",
    )(z, xt, mod, nw, w_out, rw_cat, rw_hi, router_b)
    return outs


def _dispatch_kernel(pstart_ref, cnt_ref, slot_ref, h2_ref, xs_ref, zbuf, sem, zsem, *, n_exp):
    def pad_fill(e, wait):
        rem = cnt_ref[e] % MOE_BLK
        pad = jnp.where(rem == 0, 0, MOE_BLK - rem)
        base = pstart_ref[e] + cnt_ref[e]
        head = jnp.minimum(pad, jnp.bitwise_and(-base, SUBLANES - 1))

        def fill(off, size, cond):
            copy = pltpu.make_async_copy(zbuf.at[pl.ds(0, size)], xs_ref.at[pl.ds(off, size)], zsem)

            @pl.when(cond)
            def _():
                copy.wait() if wait else copy.start()

        for r in range(SUBLANES - 1):
            fill(base + r, 1, r < head)
        off = base + head
        rest = pad - head
        for bit in reversed(range(SUBLANES.bit_length() - 1, MOE_BLK.bit_length() - 1)):
            size = 1 << bit
            take = (rest >> bit) & 1
            fill(pl.multiple_of(off, SUBLANES), size, take == 1)
            off = off + take * size

    @pl.when(pl.program_id(0) == 0)
    def _():
        zbuf[...] = jnp.zeros_like(zbuf)

        def fill(e, c):
            pad_fill(e, False)
            return c

        def fill_wait(e, c):
            pad_fill(e, True)
            return c

        lax.fori_loop(0, n_exp, fill, 0)
        lax.fori_loop(0, n_exp, fill_wait, 0)

    def row_copy(t, k):
        s = slot_ref[0, 0, t * TOP_K + k]
        return pltpu.make_async_copy(h2_ref.at[pl.ds(t, 1)], xs_ref.at[pl.ds(s, 1)], sem)

    def issue(t, c):
        for k in range(TOP_K):
            row_copy(t, k).start()
        return c

    def drain(t, c):
        for k in range(TOP_K):
            row_copy(t, k).wait()
        return c

    lax.fori_loop(0, TILE, issue, 0)
    lax.fori_loop(0, TILE, drain, 0)


def _dispatch(pstart, cnt, slots, h2, n_slots):
    n_tok, d = h2.shape
    n_tiles = n_tok // TILE
    grid_spec = pltpu.PrefetchScalarGridSpec(
        num_scalar_prefetch=2,
        grid=(n_tiles,),
        in_specs=[
            pl.BlockSpec((1, 1, TILE * TOP_K), lambda i, ps, ct: (i, 0, 0), memory_space=pltpu.SMEM),
            pl.BlockSpec((TILE, d), lambda i, ps, ct: (i, 0)),
        ],
        out_specs=pl.BlockSpec(memory_space=pl.ANY),
        scratch_shapes=[pltpu.VMEM((MOE_BLK // 2, d), F32), pltpu.SemaphoreType.DMA(()),
                        pltpu.SemaphoreType.DMA(())],
    )
    return pl.pallas_call(
        functools.partial(_dispatch_kernel, n_exp=pstart.shape[0]),
        grid_spec=grid_spec,
        out_shape=jax.ShapeDtypeStruct((n_slots, d), F32),
        compiler_params=_cparams("arbitrary"),
        name="moe_dispatch",
    )(pstart, cnt, slots.reshape(n_tiles, 1, TILE * TOP_K), h2)


def _grouped_kernel(be_ref, meta_ref, xs_ref, wg_ref, wu_ref, wd_ref, ys_ref, wg_s, wu_s, wd_s):
    i = pl.program_id(0)
    used = i < meta_ref[0]
    new_expert = jnp.logical_or(i == 0, be_ref[i] != be_ref[jnp.maximum(i - 1, 0)])

    @pl.when(jnp.logical_and(used, new_expert))
    def _():
        wg_s[...] = wg_ref[0].astype(BF16)
        wu_s[...] = wu_ref[0].astype(BF16)
        wd_s[...] = wd_ref[0].astype(BF16)

    @pl.when(used)
    def _():
        xb = xs_ref[...].astype(BF16)
        a = _silu(_dot(xb, wg_s[...])) * _dot(xb, wu_s[...])
        ys_ref[...] = _dot(a.astype(BF16), wd_s[...])

    @pl.when(jnp.logical_not(used))
    def _():
        ys_ref[...] = jnp.zeros_like(ys_ref)


def _grouped(blk_expert, n_used, xs, w_gate, w_up, w_down):
    n_slots, d = xs.shape
    n_blk = n_slots // MOE_BLK
    f = w_gate.shape[2]

    def row_map(i, be, meta):
        return (jnp.minimum(i, jnp.maximum(meta[0] - 1, 0)), 0)

    grid_spec = pltpu.PrefetchScalarGridSpec(
        num_scalar_prefetch=2,
        grid=(n_blk,),
        in_specs=[
            pl.BlockSpec((MOE_BLK, d), row_map),
            pl.BlockSpec((1, d, f), lambda i, be, meta: (be[i], 0, 0)),
            pl.BlockSpec((1, d, f), lambda i, be, meta: (be[i], 0, 0)),
            pl.BlockSpec((1, f, d), lambda i, be, meta: (be[i], 0, 0)),
        ],
        out_specs=pl.BlockSpec((MOE_BLK, d), lambda i, be, meta: (i, 0)),
        scratch_shapes=[pltpu.VMEM((d, f), BF16), pltpu.VMEM((d, f), BF16), pltpu.VMEM((f, d), BF16)],
    )
    return pl.pallas_call(
        _grouped_kernel,
        grid_spec=grid_spec,
        out_shape=jax.ShapeDtypeStruct((n_slots, d), F32),
        compiler_params=_cparams("arbitrary"),
        name="moe_experts",
    )(blk_expert, n_used, xs, w_gate, w_up, w_down)


def _combine_kernel(slot_ref, gate_ref, h2_ref, xn_ref, mod_ref, sg_ref, su_ref, sd_ref, fw_ref, ys_ref,
                    o_ref, gbuf, sem, *, d, final):
    def row_copy(t, k):
        s = slot_ref[0, 0, t * TOP_K + k]
        return pltpu.make_async_copy(ys_ref.at[pl.ds(s, 1)], gbuf.at[k, pl.ds(t, 1)], sem)

    def issue(t, c):
        for k in range(TOP_K):
            row_copy(t, k).start()
        return c

    def drain(t, c):
        for k in range(TOP_K):
            row_copy(t, k).wait()
        return c

    lax.fori_loop(0, TILE, issue, 0)
    hb = h2_ref[...].astype(BF16)
    a = _silu(_dot(hb, sg_ref[...])) * _dot(hb, su_ref[...])
    acc = _dot(a.astype(BF16), sd_ref[...])
    lax.fori_loop(0, TILE, drain, 0)
    gate = gate_ref[...]
    for k in range(TOP_K):
        acc = acc + gate[:, k:k + 1] * gbuf[k]
    x2 = xn_ref[...] + mod_ref[0][:, 5 * d:6 * d] * acc
    if final:
        x2 = x2 * lax.rsqrt(jnp.mean(x2 * x2, axis=-1, keepdims=True) + EPS) * fw_ref[...]
    o_ref[...] = x2


def _combine(slots, gates, h2, xn, mod, sh_gate, sh_up, sh_down, fw, ys, *, mod_row_map, final):
    n_tok, d = h2.shape
    n_tiles = n_tok // TILE
    f = sh_gate.shape[1]
    return pl.pallas_call(
        functools.partial(_combine_kernel, d=d, final=final),
        grid=(n_tiles,),
        in_specs=[
            pl.BlockSpec((1, 1, TILE * TOP_K), lambda i: (i, 0, 0), memory_space=pltpu.SMEM),
            pl.BlockSpec((TILE, IDX_W), lambda i: (i, 0)),
            pl.BlockSpec((TILE, d), lambda i: (i, 0)),
            pl.BlockSpec((TILE, d), lambda i: (i, 0)),
            pl.BlockSpec((1, 1, ADA_CHUNKS * d), lambda i: (mod_row_map(i), 0, 0)),
            pl.BlockSpec((d, f), lambda i: (0, 0)),
            pl.BlockSpec((d, f), lambda i: (0, 0)),
            pl.BlockSpec((f, d), lambda i: (0, 0)),
            pl.BlockSpec((1, d), lambda i: (0, 0)),
            pl.BlockSpec(memory_space=pl.ANY),
        ],
        out_specs=pl.BlockSpec((TILE, d), lambda i: (i, 0)),
        out_shape=jax.ShapeDtypeStruct((n_tok, d), F32),
        scratch_shapes=[pltpu.VMEM((TOP_K, TILE, d), F32), pltpu.SemaphoreType.DMA(())],
        compiler_params=_cparams("arbitrary"),
        name="moe_combine",
    )(slots.reshape(n_tiles, 1, TILE * TOP_K), gates, h2, xn, mod, sh_gate, sh_up, sh_down, fw, ys)


def _moe(h2, xn, eidx, gates, rank, counts, mod, w_gate, w_up, w_down, sh_gate, sh_up, sh_down, fw,
         *, mod_row_map, final):
    n_tok = h2.shape[0]
    n_exp = w_gate.shape[0]
    n_blk = (n_tok * TOP_K + n_exp * (MOE_BLK - 1) + MOE_BLK - 1) // MOE_BLK
    cnt = counts[0].astype(I32)
    padded = (cnt + MOE_BLK - 1) // MOE_BLK * MOE_BLK
    pend = jnp.cumsum(padded)
    pstart = pend - padded
    expert_ids = jnp.arange(n_exp, dtype=I32)
    slots = jnp.sum(jnp.where(eidx[:, :TOP_K, None] == expert_ids, pstart, 0), axis=-1) + rank[:, :TOP_K]
    n_used = pend[-1] // MOE_BLK
    blk = jnp.arange(n_blk, dtype=I32)
    be = jnp.sum((pend[None, :] <= blk[:, None] * MOE_BLK).astype(I32), axis=1)
    last_used = jnp.sum(jnp.where(blk == n_used - 1, be, 0))
    be = jnp.minimum(jnp.where(blk < n_used, be, last_used), n_exp - 1)
    xs = _dispatch(pstart, cnt, slots, h2, n_blk * MOE_BLK)
    ys = _grouped(be, n_used.reshape(1).astype(I32), xs, w_gate, w_up, w_down)
    return _combine(slots, gates, h2, xn, mod, sh_gate, sh_up, sh_down, fw, ys,
                    mod_row_map=mod_row_map, final=final)


def _log_sigmoid(x):
    return jnp.minimum(x, 0.0) - jnp.log1p(jnp.exp(-jnp.abs(x)))


def _mlstm_in_kernel(x_ref, mod_ref, nw_ref, w_ref, wg_ref, gb_ref, p_ref, g_ref, *, d, cn, qk_w, n_head, k_scale):
    mod = mod_ref[0]
    hf = _norm_mod(x_ref[...], nw_ref[...], mod[:, 0:d], mod[:, d:2 * d])
    h = hf.astype(BF16)
    for j in range(3 * d // cn):
        c0 = j * cn
        p = _dot(h, w_ref[:, c0:c0 + cn])
        if qk_w <= c0 < 2 * qk_w:
            p = p * k_scale
        p_ref[:, c0:c0 + cn] = p.astype(BF16)
    g = _dot_hp(hf, wg_ref[...]) + gb_ref[...]
    g = GATE_CAP * jnp.tanh(g / GATE_CAP)
    col = lax.broadcasted_iota(I32, g.shape, 1)
    is_forget = (col // n_head) % 2 == 1
    g_ref[...] = jnp.where(is_forget, _log_sigmoid(g), g)


def _mlstm_in(xt, mod, nw, w_qkvo, w_g, gate_b, *, n_batch, tiles_per_b, n_head):
    n_tok, d = xt.shape
    n_tiles = n_tok // TILE
    qk_w = d // 2
    cn = min(512, qk_w)
    n_g = w_g.shape[1]
    k_scale = float((qk_w // n_head) ** -0.5)

    def mod_map(i):
        return (jnp.where(i % tiles_per_b == 0, n_batch, i // tiles_per_b), 0, 0)

    return pl.pallas_call(
        functools.partial(_mlstm_in_kernel, d=d, cn=cn, qk_w=qk_w, n_head=n_head, k_scale=k_scale),
        grid=(n_tiles,),
        in_specs=[
            pl.BlockSpec((TILE, d), lambda i: (i, 0)),
            pl.BlockSpec((1, 1, ADA_CHUNKS * d), mod_map),
            pl.BlockSpec((1, d), lambda i: (0, 0)),
            _resident((d, 3 * d), lambda i: (0, 0)),
            pl.BlockSpec((d, n_g), lambda i: (0, 0)),
            pl.BlockSpec((1, n_g), lambda i: (0, 0)),
        ],
        out_specs=[
            pl.BlockSpec((TILE, 3 * d), lambda i: (i, 0)),
            pl.BlockSpec((TILE, n_g), lambda i: (i, 0)),
        ],
        out_shape=[
            jax.ShapeDtypeStruct((n_tok, 3 * d), BF16),
            jax.ShapeDtypeStruct((n_tok, n_g), F32),
        ],
        compiler_params=_cparams("parallel"),
        name="mlstm_in",
    )(xt, mod, nw, w_qkvo, w_g, gate_b)


def _mlstm_chunk(q, k, v, li_r, lf_r, c_st, n_st, m_st, *, backward):
    n_t = q.shape[0]
    tt = lax.broadcasted_iota(I32, (n_t, n_t), 0)
    ss = lax.broadcasted_iota(I32, (n_t, n_t), 1)
    diag = tt == ss
    if backward:
        seen = ss >= tt
        seen_t = tt >= ss
    else:
        seen = ss <= tt
        seen_t = tt <= ss
    lf_b = jnp.broadcast_to(lf_r, (n_t, n_t))
    li_b = jnp.broadcast_to(li_r, (n_t, n_t))
    b_c = jnp.sum(jnp.where(seen, lf_b, 0.0), axis=1, keepdims=True)
    lf_c = jnp.sum(jnp.where(diag, lf_b, 0.0), axis=1, keepdims=True)
    li_c = jnp.sum(jnp.where(diag, li_b, 0.0), axis=1, keepdims=True)
    b_r = jnp.sum(jnp.where(seen_t, lf_c, 0.0), axis=0, keepdims=True)
    b_last = jnp.sum(lf_r, axis=1, keepdims=True)
    dmat = jnp.where(seen, b_c - b_r + li_b, -jnp.inf)
    a = b_c + m_st
    m_row = jnp.maximum(a, jnp.max(dmat, axis=1, keepdims=True))
    w_intra = jnp.exp(dmat - m_row)
    w_inter = jnp.exp(a - m_row)
    s = lax.dot_general(q, k, (((1,), (1,)), ((), ())), preferred_element_type=F32) * w_intra
    inter = _dot(q, c_st.astype(BF16))
    num = _dot(s.astype(BF16), v) + w_inter * inter
    qn = jnp.sum(q.astype(F32) * n_st, axis=1, keepdims=True)
    den = jnp.sum(s, axis=1, keepdims=True) + w_inter * qn
    h = num / jnp.maximum(jnp.abs(den), jnp.exp(-m_row))
    g = b_last - b_c + li_c
    m_new = jnp.maximum(b_last + m_st, jnp.max(g, axis=0, keepdims=True))
    decay = jnp.exp(b_last + m_st - m_new)
    kw = k.astype(F32) * jnp.exp(g - m_new)
    c_new = decay * c_st + lax.dot_general(kw.astype(BF16), v, (((0,), (0,)), ((), ())),
                                           preferred_element_type=F32)
    n_new = decay * n_st + jnp.sum(kw, axis=0, keepdims=True)
    return h, c_new, n_new, m_new


def _mlstm_scan_kernel(q_ref, k_ref, v_ref, o_ref, gr_ref, nw_ref, z_ref, hf_ref, hb_ref, c_ref,
                       *, n_chunk, n_ctx_chunk, dqk, dv):
    c_ref[...] = jnp.zeros_like(c_ref)
    n0 = jnp.zeros((1, dqk), F32)
    m0 = jnp.full((1, 1), M_INIT, F32)

    def step(i, carry):
        n_f, m_f, n_b, m_b = carry
        jf = i
        jb = jnp.where(i < n_ctx_chunk, n_ctx_chunk - 1 - i, n_chunk - 1 - (i - n_ctx_chunk))
        rf = pl.ds(pl.multiple_of(jf * CHUNK, CHUNK), CHUNK)
        rb = pl.ds(pl.multiple_of(jb * CHUNK, CHUNK), CHUNK)
        h_f, c_f, n_f, m_f = _mlstm_chunk(q_ref[rf, :], k_ref[rf, :], v_ref[rf, :],
                                          gr_ref[0, 0, 0, jf], gr_ref[0, 1, 0, jf],
                                          c_ref[0], n_f, m_f, backward=False)
        c_ref[0] = c_f
        hf_ref[rf, :] = h_f
        h_b, c_b, n_b, m_b = _mlstm_chunk(q_ref[rb, :], k_ref[rb, :], v_ref[rb, :],
                                          gr_ref[0, 2, 0, jb], gr_ref[0, 3, 0, jb],
                                          c_ref[1], n_b, m_b, backward=True)
        c_ref[1] = c_b
        hb_ref[rb, :] = h_b
        return n_f, m_f, n_b, m_b

    lax.fori_loop(0, n_chunk, step, (n0, m0, n0, m0))
    n_ctx = n_ctx_chunk * CHUNK
    n_lat = (n_chunk - n_ctx_chunk) * CHUNK
    lat = pl.ds(n_ctx, n_lat)
    h = hf_ref[lat, :] + hb_ref[lat, :]
    hn = h * lax.rsqrt(jnp.mean(h * h, axis=-1, keepdims=True) + EPS)
    y = hn * nw_ref[...] * _sigmoid(o_ref[lat, :].astype(F32))
    z_ref[...] = y.astype(BF16)


def _mlstm_scan(p, gr, norm_w, *, n_batch, n_head, seq_all, n_ctx, d):
    dqk = d // (2 * n_head)
    dv = d // n_head
    n_chunk = seq_all // CHUNK
    n_lat = seq_all - n_ctx
    qk_blocks = (d // 2) // dqk
    v_blocks = d // dv
    return pl.pallas_call(
        functools.partial(_mlstm_scan_kernel, n_chunk=n_chunk, n_ctx_chunk=n_ctx // CHUNK, dqk=dqk, dv=dv),
        grid=(n_batch, n_head),
        in_specs=[
            pl.BlockSpec((seq_all, dqk), lambda b, h: (b, h)),
            pl.BlockSpec((seq_all, dqk), lambda b, h: (b, qk_blocks + h)),
            pl.BlockSpec((seq_all, dv), lambda b, h: (b, v_blocks + h)),
            pl.BlockSpec((seq_all, dv), lambda b, h: (b, 2 * v_blocks + h)),
            pl.BlockSpec((1, 4, 1, n_chunk, 1, CHUNK), lambda b, h: (b, 0, h, 0, 0, 0)),
            pl.BlockSpec((1, dv), lambda b, h: (0, h)),
        ],
        out_specs=pl.BlockSpec((n_lat, dv), lambda b, h: (b, h)),
        out_shape=jax.ShapeDtypeStruct((n_batch * n_lat, d), BF16),
        scratch_shapes=[
            pltpu.VMEM((seq_all, dv), F32),
            pltpu.VMEM((seq_all, dv), F32),
            pltpu.VMEM((2, dqk, dv), F32),
        ],
        compiler_params=_cparams("parallel", "parallel"),
        name="mlstm_scan",
    )(p, p, p, p, gr, norm_w)


def kernel(x, c, ctx, c_ctx, ada_w, ada_b, norm_mix_w, norm_ffn_w, conv_in_w, conv_dw_w, conv_out_w,
           mlstm_in_w, mlstm_gate_b, mlstm_norm_w, mlstm_out_w, router_w, router_bias,
           exp_gate_w, exp_up_w, exp_down_w, shared_gate_w, shared_up_w, shared_down_w, final_norm_w):
    n_batch, seq, d = x.shape
    n_ctx = ctx.shape[1]
    assert ada_w.shape[0] == 2 and n_ctx == TILE and seq % TILE == 0 and n_batch + 1 <= ADA_ROWS
    seq_all = n_ctx + seq
    tiles_per_b = seq_all // TILE
    lat_tiles_per_b = seq // TILE
    n_head = (mlstm_in_w.shape[2] - 3 * d) // 4

    cond = jnp.zeros((ADA_ROWS, d), F32).at[:n_batch].set(c).at[n_batch].set(c_ctx)
    mod = _ada_mod(cond, ada_w, ada_b)
    mod0 = mod[0].reshape(ADA_ROWS, 1, ADA_CHUNKS * d)
    mod1 = mod[1].reshape(ADA_ROWS, 1, ADA_CHUNKS * d)

    def all_mod_row(i):
        return jnp.where(i % tiles_per_b == 0, n_batch, i // tiles_per_b)

    def lat_mod_row(i):
        return i // lat_tiles_per_b

    def lat_tile(i):
        return (i // lat_tiles_per_b) * tiles_per_b + 1 + i % lat_tiles_per_b

    row = lambda w: w.reshape(1, -1)
    bf = lambda w: w.astype(BF16)
    x0 = jnp.concatenate([ctx, x], axis=1).reshape(n_batch * seq_all, d)

    z0 = _conv_in(x0, mod0, row(norm_mix_w[0]), bf(conv_in_w[0]), conv_dw_w[0],
                  n_batch=n_batch, tiles_per_b=tiles_per_b)
    xn0, h20, eidx0, gate0, rank0, cnt0 = _post(
        z0, x0, mod0, row(norm_ffn_w[0]), bf(conv_out_w[0]), router_w[0], row(router_bias[0]),
        x_tile_map=lambda i: i, mod_row_map=all_mod_row)
    x1 = _moe(h20, xn0, eidx0, gate0, rank0, cnt0, mod0, exp_gate_w[0], exp_up_w[0], exp_down_w[0],
              bf(shared_gate_w[0]), bf(shared_up_w[0]), bf(shared_down_w[0]), row(final_norm_w),
              mod_row_map=all_mod_row, final=False)

    w_in = mlstm_in_w[0]
    p, g = _mlstm_in(x1, mod1, row(norm_mix_w[1]), bf(w_in[:, :3 * d]), w_in[:, 3 * d:], row(mlstm_gate_b[0]),
                     n_batch=n_batch, tiles_per_b=tiles_per_b, n_head=n_head)
    n_chunk = seq_all // CHUNK
    gr = g.reshape(n_batch, n_chunk, CHUNK, 4, n_head).transpose(0, 3, 4, 1, 2)
    gr = gr.reshape(n_batch, 4, n_head, n_chunk, 1, CHUNK)
    z1 = _mlstm_scan(p, gr, row(mlstm_norm_w[0]), n_batch=n_batch, n_head=n_head, seq_all=seq_all,
                     n_ctx=n_ctx, d=d)
    xn1, h21, eidx1, gate1, rank1, cnt1 = _post(
        z1, x1, mod1, row(norm_ffn_w[1]), bf(mlstm_out_w[0]), router_w[1], row(router_bias[1]),
        x_tile_map=lat_tile, mod_row_map=lat_mod_row)
    out = _moe(h21, xn1, eidx1, gate1, rank1, cnt1, mod1, exp_gate_w[1], exp_up_w[1], exp_down_w[1],
               bf(shared_gate_w[1]), bf(shared_up_w[1]), bf(shared_down_w[1]), row(final_norm_w),
               mod_row_map=lat_mod_row, final=True)
    return out.reshape(n_batch, seq, d)
```

```python
import functools

import jax
import jax.numpy as jnp
from jax import lax
from jax.experimental import pallas as pl
from jax.experimental.pallas import tpu as pltpu

F32 = jnp.float32
BF16 = jnp.bfloat16
I32 = jnp.int32

TILE = 256
GRID_W = 64
CHUNK = 64
TOP_K = 6
MOE_BLK = 256
IDX_W = 8
SUBLANES = 8
ADA_CHUNKS = 6
ADA_ROWS = 16
EPS = 1e-6
GATE_CAP = 15.0
M_INIT = -1e30
ROUTED_SCALE = 2.5
V7X_VMEM_LIMIT = 56 * 1024 * 1024


def _cparams(*sem):
    return pltpu.CompilerParams(dimension_semantics=sem, vmem_limit_bytes=V7X_VMEM_LIMIT)


def _resident(shape, index_map):
    return pl.BlockSpec(shape, index_map, pipeline_mode=pl.Buffered(1))


def _sigmoid(x):
    return 1.0 / (1.0 + jnp.exp(-x))


def _silu(x):
    return x * _sigmoid(x)


def _split3(a):
    hi = a.astype(BF16)
    r1 = a - hi.astype(F32)
    mid = r1.astype(BF16)
    lo = (r1 - mid.astype(F32)).astype(BF16)
    return hi, mid, lo


def _dot(a, b):
    return jnp.dot(a, b, preferred_element_type=F32)


def _dot_hp(a, b):
    a0, a1, a2 = _split3(a)
    b0, b1, b2 = _split3(b)
    return (_dot(a0, b0) + (_dot(a0, b1) + _dot(a1, b0))
            + (_dot(a0, b2) + _dot(a1, b1) + _dot(a2, b0)))


def _norm_mod(x, w, shift, scale):
    y = x * lax.rsqrt(jnp.mean(x * x, axis=-1, keepdims=True) + EPS)
    return (y * w) * (1.0 + scale) + shift


def _ada_kernel(cond_ref, w_ref, b_ref, o_ref):
    a = _silu(cond_ref[...]).astype(BF16)
    o_ref[0] = _dot(a, w_ref[0].astype(BF16)) + b_ref[0]


def _ada_mod(cond, ada_w, ada_b):
    n_layer, d, n_out = ada_w.shape
    tn = 1024 if n_out % 1024 == 0 else n_out
    return pl.pallas_call(
        _ada_kernel,
        grid=(n_layer, n_out // tn),
        in_specs=[
            pl.BlockSpec((ADA_ROWS, d), lambda l, j: (0, 0)),
            pl.BlockSpec((1, d, tn), lambda l, j: (l, 0, j)),
            pl.BlockSpec((1, 1, tn), lambda l, j: (l, 0, j)),
        ],
        out_specs=pl.BlockSpec((1, ADA_ROWS, tn), lambda l, j: (l, 0, j)),
        out_shape=jax.ShapeDtypeStruct((n_layer, ADA_ROWS, n_out), F32),
        compiler_params=_cparams("parallel", "parallel"),
        name="ada_mod",
    )(cond, ada_w, ada_b.reshape(n_layer, 1, n_out))


def _conv_in_kernel(x_ref, mod_ref, nw_ref, win_ref, wdw_ref, z_ref, *, d, cn, tiles_per_b):
    i = pl.program_id(0)
    mod = mod_ref[0]
    h = _norm_mod(x_ref[...], nw_ref[...], mod[:, 0:d], mod[:, d:2 * d]).astype(BF16)
    t = lax.broadcasted_iota(I32, (TILE, 1), 0)
    pos_mask = jnp.where((i % tiles_per_b) == 0, TILE - 1, GRID_W - 1)
    pos = jnp.bitwise_and(t, pos_mask)
    first = pos == 0
    last = pos == pos_mask
    for j in range(d // cn):
        c0 = j * cn
        bg = _dot(h, win_ref[:, c0:c0 + cn])
        cg = _dot(h, win_ref[:, d + c0:d + c0 + cn])
        hi = _dot(h, win_ref[:, 2 * d + c0:2 * d + c0 + cn])
        u = cg * hi
        u_prev = jnp.where(first, 0.0, pltpu.roll(u, 1, 0))
        u_next = jnp.where(last, 0.0, pltpu.roll(u, TILE - 1, 0))
        w = wdw_ref[:, c0:c0 + cn]
        y = u_prev * w[0:1] + u * w[1:2] + u_next * w[2:3]
        z_ref[:, c0:c0 + cn] = (bg * y).astype(BF16)


def _conv_in(xt, mod, nw, w_in, w_dw, *, n_batch, tiles_per_b):
    n_tok, d = xt.shape
    n_tiles = n_tok // TILE
    cn = min(512, d)

    def mod_map(i):
        return (jnp.where(i % tiles_per_b == 0, n_batch, i // tiles_per_b), 0, 0)

    return pl.pallas_call(
        functools.partial(_conv_in_kernel, d=d, cn=cn, tiles_per_b=tiles_per_b),
        grid=(n_tiles,),
        in_specs=[
            pl.BlockSpec((TILE, d), lambda i: (i, 0)),
            pl.BlockSpec((1, 1, ADA_CHUNKS * d), mod_map),
            pl.BlockSpec((1, d), lambda i: (0, 0)),
            _resident((d, 3 * d), lambda i: (0, 0)),
            pl.BlockSpec((3, d), lambda i: (0, 0)),
        ],
        out_specs=pl.BlockSpec((TILE, d), lambda i: (i, 0)),
        out_shape=jax.ShapeDtypeStruct((n_tok, d), BF16),
        compiler_params=_cparams("parallel"),
        name="conv_in",
    )(xt, mod, nw, w_in, w_dw)


def _post_kernel(z_ref, x_ref, mod_ref, nw_ref, wout_ref, rwcat_ref, rwhi_ref, rb_ref,
                 xn_ref, h2_ref, eidx_ref, gate_ref, rank_ref, cnt_ref, carry_ref, *, d, n_exp):
    i = pl.program_id(0)

    @pl.when(i == 0)
    def _():
        carry_ref[...] = jnp.zeros_like(carry_ref)

    mod = mod_ref[0]
    y = _dot(z_ref[...], wout_ref[...])
    xn = x_ref[...] + mod[:, 2 * d:3 * d] * y
    xn_ref[...] = xn
    h2 = _norm_mod(xn, nw_ref[...], mod[:, 3 * d:4 * d], mod[:, 4 * d:5 * d])
    h2_ref[...] = h2

    h2_hi = h2.astype(BF16)
    h2_lo = (h2 - h2_hi.astype(F32)).astype(BF16)
    p_hi = _dot(h2_hi, rwcat_ref[...])
    logits = p_hi[:, :n_exp] + (p_hi[:, n_exp:] + _dot(h2_lo, rwhi_ref[...]))
    scores = _sigmoid(logits)
    lane = lax.broadcasted_iota(I32, (TILE, n_exp), 1)
    lane_f = lane.astype(F32)
    work = scores + rb_ref[...]
    onehots, picks = [], []
    for _ in range(TOP_K):
        mx = jnp.max(work, axis=1, keepdims=True)
        first_max = jnp.min(jnp.where(work == mx, lane_f, float(n_exp)), axis=1, keepdims=True)
        oh = lane_f == first_max
        onehots.append(oh)
        picks.append(first_max)
        work = jnp.where(oh, -jnp.inf, work)
    sel = onehots[0]
    for oh in onehots[1:]:
        sel = jnp.logical_or(sel, oh)
    picked = jnp.where(sel, scores, 0.0)
    gates = picked / jnp.sum(picked, axis=1, keepdims=True) * ROUTED_SCALE
    sel_f = jnp.where(sel, 1.0, 0.0)
    r_i = lax.broadcasted_iota(I32, (TILE, TILE), 0)
    c_i = lax.broadcasted_iota(I32, (TILE, TILE), 1)
    before = jnp.where(c_i < r_i, 1.0, 0.0).astype(BF16)
    cum = _dot(before, sel_f.astype(BF16)) + carry_ref[...]

    eidx_ref[...] = jnp.zeros_like(eidx_ref)
    gate_ref[...] = jnp.zeros_like(gate_ref)
    rank_ref[...] = jnp.zeros_like(rank_ref)
    for k, oh in enumerate(onehots):
        eidx_ref[:, k:k + 1] = picks[k].astype(I32)
        gate_ref[:, k:k + 1] = jnp.sum(jnp.where(oh, gates, 0.0), axis=1, keepdims=True)
        rank_ref[:, k:k + 1] = jnp.sum(jnp.where(oh, cum, 0.0), axis=1, keepdims=True).astype(I32)

    total = carry_ref[...] + jnp.sum(sel_f, axis=0, keepdims=True)
    carry_ref[...] = total
    cnt_ref[...] = total


def _post(z, xt, mod, nw, w_out, router_w, router_b, *, x_tile_map, mod_row_map):
    rw_hi = router_w.astype(BF16)
    rw_lo = (router_w - rw_hi.astype(F32)).astype(BF16)
    rw_cat = jnp.concatenate([rw_hi, rw_lo], axis=1)
    n_tok, d = z.shape
    n_tiles = n_tok // TILE
    n_exp = router_w.shape[1]
    outs = pl.pallas_call(
        functools.partial(_post_kernel, d=d, n_exp=n_exp),
        grid=(n_tiles,),
        in_specs=[
            pl.BlockSpec((TILE, d), lambda i: (i, 0)),
            pl.BlockSpec((TILE, d), lambda i: (x_tile_map(i), 0)),
            pl.BlockSpec((1, 1, ADA_CHUNKS * d), lambda i: (mod_row_map(i), 0, 0)),
            pl.BlockSpec((1, d), lambda i: (0, 0)),
            _resident((d, d), lambda i: (0, 0)),
            pl.BlockSpec((d, 2 * n_exp), lambda i: (0, 0)),
            pl.BlockSpec((d, n_exp), lambda i: (0, 0)),
            pl.BlockSpec((1, n_exp), lambda i: (0, 0)),
        ],
        out_specs=[
            pl.BlockSpec((TILE, d), lambda i: (i, 0)),
            pl.BlockSpec((TILE, d), lambda i: (i, 0)),
            pl.BlockSpec((TILE, IDX_W), lambda i: (i, 0)),
            pl.BlockSpec((TILE, IDX_W), lambda i: (i, 0)),
            pl.BlockSpec((TILE, IDX_W), lambda i: (i, 0)),
            pl.BlockSpec((1, n_exp), lambda i: (0, 0)),
        ],
        out_shape=[
            jax.ShapeDtypeStruct((n_tok, d), F32),
            jax.ShapeDtypeStruct((n_tok, d), F32),
            jax.ShapeDtypeStruct((n_tok, IDX_W), I32),
            jax.ShapeDtypeStruct((n_tok, IDX_W), F32),
            jax.ShapeDtypeStruct((n_tok, IDX_W), I32),
            jax.ShapeDtypeStruct((1, n_exp), F32),
        ],
        scratch_shapes=[pltpu.VMEM((1, n_exp), F32)],
        compiler_params=_cparams("arbitrary"),
        name="post_mixer",
    )(z, xt, mod, nw, w_out, rw_cat, rw_hi, router_b)
    return outs


def _dispatch_kernel(pstart_ref, cnt_ref, slot_ref, h2_ref, xs_ref, zbuf, sem, zsem, *, n_exp):
    def pad_fill(e, wait):
        rem = cnt_ref[e] % MOE_BLK
        pad = jnp.where(rem == 0, 0, MOE_BLK - rem)
        base = pstart_ref[e] + cnt_ref[e]
        head = jnp.minimum(pad, jnp.bitwise_and(-base, SUBLANES - 1))

        def fill(off, size, cond):
            copy = pltpu.make_async_copy(zbuf.at[pl.ds(0, size)], xs_ref.at[pl.ds(off, size)], zsem)

            @pl.when(cond)
            def _():
                copy.wait() if wait else copy.start()

        for r in range(SUBLANES - 1):
            fill(base + r, 1, r < head)
        off = base + head
        rest = pad - head
        for bit in reversed(range(SUBLANES.bit_length() - 1, MOE_BLK.bit_length() - 1)):
            size = 1 << bit
            take = (rest >> bit) & 1
            fill(pl.multiple_of(off, SUBLANES), size, take == 1)
            off = off + take * size

    @pl.when(pl.program_id(0) == 0)
    def _():
        zbuf[...] = jnp.zeros_like(zbuf)

        def fill(e, c):
            pad_fill(e, False)
            return c

        def fill_wait(e, c):
            pad_fill(e, True)
            return c

        lax.fori_loop(0, n_exp, fill, 0)
        lax.fori_loop(0, n_exp, fill_wait, 0)

    def row_copy(t, k):
        s = slot_ref[0, 0, t * TOP_K + k]
        return pltpu.make_async_copy(h2_ref.at[pl.ds(t, 1)], xs_ref.at[pl.ds(s, 1)], sem)

    def issue(t, c):
        for k in range(TOP_K):
            row_copy(t, k).start(priority=k % 2)
        return c

    def drain(t, c):
        for k in range(TOP_K):
            row_copy(t, k).wait()
        return c

    lax.fori_loop(0, TILE, issue, 0)
    lax.fori_loop(0, TILE, drain, 0)


def _dispatch(pstart, cnt, slots, h2, n_slots):
    n_tok, d = h2.shape
    n_tiles = n_tok // TILE
    grid_spec = pltpu.PrefetchScalarGridSpec(
        num_scalar_prefetch=2,
        grid=(n_tiles,),
        in_specs=[
            pl.BlockSpec((1, 1, TILE * TOP_K), lambda i, ps, ct: (i, 0, 0), memory_space=pltpu.SMEM),
            pl.BlockSpec((TILE, d), lambda i, ps, ct: (i, 0)),
        ],
        out_specs=pl.BlockSpec(memory_space=pl.ANY),
        scratch_shapes=[pltpu.VMEM((MOE_BLK // 2, d), F32), pltpu.SemaphoreType.DMA(()),
                        pltpu.SemaphoreType.DMA(())],
    )
    return pl.pallas_call(
        functools.partial(_dispatch_kernel, n_exp=pstart.shape[0]),
        grid_spec=grid_spec,
        out_shape=jax.ShapeDtypeStruct((n_slots, d), F32),
        compiler_params=_cparams("arbitrary"),
        name="moe_dispatch",
    )(pstart, cnt, slots.reshape(n_tiles, 1, TILE * TOP_K), h2)


def _grouped_kernel(be_ref, meta_ref, xs_ref, wg_ref, wu_ref, wd_ref, ys_ref, wg_s, wu_s, wd_s):
    i = pl.program_id(0)
    used = i < meta_ref[0]
    new_expert = jnp.logical_or(i == 0, be_ref[i] != be_ref[jnp.maximum(i - 1, 0)])

    @pl.when(jnp.logical_and(used, new_expert))
    def _():
        wg_s[...] = wg_ref[0, 0].astype(BF16)
        wu_s[...] = wu_ref[0, 0].astype(BF16)
        wd_s[...] = wd_ref[0, 0].astype(BF16)

    @pl.when(used)
    def _():
        xb = xs_ref[...].astype(BF16)
        a = _silu(_dot(xb, wg_s[...])) * _dot(xb, wu_s[...])
        ys_ref[...] = _dot(a.astype(BF16), wd_s[...])

    @pl.when(jnp.logical_not(used))
    def _():
        ys_ref[...] = jnp.zeros_like(ys_ref)


def _grouped(blk_expert, n_used, xs, w_gate, w_up, w_down, layer):
    n_slots, d = xs.shape
    n_blk = n_slots // MOE_BLK
    f = w_gate.shape[3]

    def row_map(i, be, meta):
        return (jnp.minimum(i, jnp.maximum(meta[0] - 1, 0)), 0)

    grid_spec = pltpu.PrefetchScalarGridSpec(
        num_scalar_prefetch=2,
        grid=(n_blk,),
        in_specs=[
            pl.BlockSpec((MOE_BLK, d), row_map),
            pl.BlockSpec((1, 1, d, f), lambda i, be, meta: (layer, be[i], 0, 0)),
            pl.BlockSpec((1, 1, d, f), lambda i, be, meta: (layer, be[i], 0, 0)),
            pl.BlockSpec((1, 1, f, d), lambda i, be, meta: (layer, be[i], 0, 0)),
        ],
        out_specs=pl.BlockSpec((MOE_BLK, d), lambda i, be, meta: (i, 0)),
        scratch_shapes=[pltpu.VMEM((d, f), BF16), pltpu.VMEM((d, f), BF16), pltpu.VMEM((f, d), BF16)],
    )
    return pl.pallas_call(
        _grouped_kernel,
        grid_spec=grid_spec,
        out_shape=jax.ShapeDtypeStruct((n_slots, d), F32),
        compiler_params=_cparams("arbitrary"),
        name="moe_experts",
    )(blk_expert, n_used, xs, w_gate, w_up, w_down)


def _combine_kernel(slot_ref, gate_ref, h2_ref, xn_ref, mod_ref, sg_ref, su_ref, sd_ref, fw_ref, ys_ref,
                    o_ref, gbuf, sem, *, d, final):
    def row_copy(t, k):
        s = slot_ref[0, 0, t * TOP_K + k]
        return pltpu.make_async_copy(ys_ref.at[pl.ds(s, 1)], gbuf.at[k, pl.ds(t, 1)], sem)

    def issue(t, c):
        for k in range(TOP_K):
            row_copy(t, k).start(priority=k % 2)
        return c

    def drain(t, c):
        for k in range(TOP_K):
            row_copy(t, k).wait()
        return c

    lax.fori_loop(0, TILE, issue, 0)
    hb = h2_ref[...].astype(BF16)
    a = _silu(_dot(hb, sg_ref[...])) * _dot(hb, su_ref[...])
    acc = _dot(a.astype(BF16), sd_ref[...])
    lax.fori_loop(0, TILE, drain, 0)
    gate = gate_ref[...]
    for k in range(TOP_K):
        acc = acc + gate[:, k:k + 1] * gbuf[k]
    x2 = xn_ref[...] + mod_ref[0][:, 5 * d:6 * d] * acc
    if final:
        x2 = x2 * lax.rsqrt(jnp.mean(x2 * x2, axis=-1, keepdims=True) + EPS) * fw_ref[...]
    o_ref[...] = x2


def _combine(slots, gates, h2, xn, mod, sh_gate, sh_up, sh_down, fw, ys, *, mod_row_map, final):
    n_tok, d = h2.shape
    n_tiles = n_tok // TILE
    f = sh_gate.shape[1]
    return pl.pallas_call(
        functools.partial(_combine_kernel, d=d, final=final),
        grid=(n_tiles,),
        in_specs=[
            pl.BlockSpec((1, 1, TILE * TOP_K), lambda i: (i, 0, 0), memory_space=pltpu.SMEM),
            pl.BlockSpec((TILE, IDX_W), lambda i: (i, 0)),
            pl.BlockSpec((TILE, d), lambda i: (i, 0)),
            pl.BlockSpec((TILE, d), lambda i: (i, 0)),
            pl.BlockSpec((1, 1, ADA_CHUNKS * d), lambda i: (mod_row_map(i), 0, 0)),
            pl.BlockSpec((d, f), lambda i: (0, 0)),
            pl.BlockSpec((d, f), lambda i: (0, 0)),
            pl.BlockSpec((f, d), lambda i: (0, 0)),
            pl.BlockSpec((1, d), lambda i: (0, 0)),
            pl.BlockSpec(memory_space=pl.ANY),
        ],
        out_specs=pl.BlockSpec((TILE, d), lambda i: (i, 0)),
        out_shape=jax.ShapeDtypeStruct((n_tok, d), F32),
        scratch_shapes=[pltpu.VMEM((TOP_K, TILE, d), F32), pltpu.SemaphoreType.DMA(())],
        compiler_params=_cparams("arbitrary"),
        name="moe_combine",
    )(slots.reshape(n_tiles, 1, TILE * TOP_K), gates, h2, xn, mod, sh_gate, sh_up, sh_down, fw, ys)


def _moe(h2, xn, eidx, gates, rank, counts, mod, w_gate, w_up, w_down, sh_gate, sh_up, sh_down, fw,
         *, layer, mod_row_map, final):
    n_tok = h2.shape[0]
    n_exp = w_gate.shape[1]
    n_blk = (n_tok * TOP_K + n_exp * (MOE_BLK - 1) + MOE_BLK - 1) // MOE_BLK
    cnt = counts[0].astype(I32)
    padded = (cnt + MOE_BLK - 1) // MOE_BLK * MOE_BLK
    pend = jnp.cumsum(padded)
    pstart = pend - padded
    expert_ids = jnp.arange(n_exp, dtype=I32)
    slots = jnp.sum(jnp.where(eidx[:, :TOP_K, None] == expert_ids, pstart, 0), axis=-1) + rank[:, :TOP_K]
    n_used = pend[-1] // MOE_BLK
    blk = jnp.arange(n_blk, dtype=I32)
    be = jnp.sum((pend[None, :] <= blk[:, None] * MOE_BLK).astype(I32), axis=1)
    last_used = jnp.sum(jnp.where(blk == n_used - 1, be, 0))
    be = jnp.minimum(jnp.where(blk < n_used, be, last_used), n_exp - 1)
    xs = _dispatch(pstart, cnt, slots, h2, n_blk * MOE_BLK)
    ys = _grouped(be, n_used.reshape(1).astype(I32), xs, w_gate, w_up, w_down, layer)
    return _combine(slots, gates, h2, xn, mod, sh_gate, sh_up, sh_down, fw, ys,
                    mod_row_map=mod_row_map, final=final)


def _log_sigmoid(x):
    return jnp.minimum(x, 0.0) - jnp.log1p(jnp.exp(-jnp.abs(x)))


def _mlstm_in_kernel(x_ref, mod_ref, nw_ref, w_ref, wg_ref, gb_ref, p_ref, g_ref, *, d, cn, qk_w, n_head, k_scale):
    mod = mod_ref[0]
    hf = _norm_mod(x_ref[...], nw_ref[...], mod[:, 0:d], mod[:, d:2 * d])
    h = hf.astype(BF16)
    for j in range(3 * d // cn):
        c0 = j * cn
        p = _dot(h, w_ref[:, c0:c0 + cn])
        if qk_w <= c0 < 2 * qk_w:
            p = p * k_scale
        p_ref[:, c0:c0 + cn] = p.astype(BF16)
    g = _dot_hp(hf, wg_ref[...]) + gb_ref[...]
    g = GATE_CAP * jnp.tanh(g / GATE_CAP)
    col = lax.broadcasted_iota(I32, g.shape, 1)
    is_forget = (col // n_head) % 2 == 1
    g_ref[...] = jnp.where(is_forget, _log_sigmoid(g), g)


def _mlstm_in(xt, mod, nw, w_qkvo, w_g, gate_b, *, n_batch, tiles_per_b, n_head):
    n_tok, d = xt.shape
    n_tiles = n_tok // TILE
    qk_w = d // 2
    cn = min(512, qk_w)
    n_g = w_g.shape[1]
    k_scale = float((qk_w // n_head) ** -0.5)

    def mod_map(i):
        return (jnp.where(i % tiles_per_b == 0, n_batch, i // tiles_per_b), 0, 0)

    return pl.pallas_call(
        functools.partial(_mlstm_in_kernel, d=d, cn=cn, qk_w=qk_w, n_head=n_head, k_scale=k_scale),
        grid=(n_tiles,),
        in_specs=[
            pl.BlockSpec((TILE, d), lambda i: (i, 0)),
            pl.BlockSpec((1, 1, ADA_CHUNKS * d), mod_map),
            pl.BlockSpec((1, d), lambda i: (0, 0)),
            _resident((d, 3 * d), lambda i: (0, 0)),
            pl.BlockSpec((d, n_g), lambda i: (0, 0)),
            pl.BlockSpec((1, n_g), lambda i: (0, 0)),
        ],
        out_specs=[
            pl.BlockSpec((TILE, 3 * d), lambda i: (i, 0)),
            pl.BlockSpec((TILE, n_g), lambda i: (i, 0)),
        ],
        out_shape=[
            jax.ShapeDtypeStruct((n_tok, 3 * d), BF16),
            jax.ShapeDtypeStruct((n_tok, n_g), F32),
        ],
        compiler_params=_cparams("parallel"),
        name="mlstm_in",
    )(xt, mod, nw, w_qkvo, w_g, gate_b)


def _mlstm_chunk(q, k, v, li_r, lf_r, c_st, n_st, m_st, *, backward):
    n_t = q.shape[0]
    tt = lax.broadcasted_iota(I32, (n_t, n_t), 0)
    ss = lax.broadcasted_iota(I32, (n_t, n_t), 1)
    diag = tt == ss
    if backward:
        seen = ss >= tt
        seen_t = tt >= ss
    else:
        seen = ss <= tt
        seen_t = tt <= ss
    lf_b = jnp.broadcast_to(lf_r, (n_t, n_t))
    li_b = jnp.broadcast_to(li_r, (n_t, n_t))
    b_c = jnp.sum(jnp.where(seen, lf_b, 0.0), axis=1, keepdims=True)
    lf_c = jnp.sum(jnp.where(diag, lf_b, 0.0), axis=1, keepdims=True)
    li_c = jnp.sum(jnp.where(diag, li_b, 0.0), axis=1, keepdims=True)
    b_r = jnp.sum(jnp.where(seen_t, lf_c, 0.0), axis=0, keepdims=True)
    b_last = jnp.sum(lf_r, axis=1, keepdims=True)
    dmat = jnp.where(seen, b_c - b_r + li_b, -jnp.inf)
    a = b_c + m_st
    m_row = jnp.maximum(a, jnp.max(dmat, axis=1, keepdims=True))
    w_intra = jnp.exp(dmat - m_row)
    w_inter = jnp.exp(a - m_row)
    s = lax.dot_general(q, k, (((1,), (1,)), ((), ())), preferred_element_type=F32) * w_intra
    inter = _dot(q, c_st.astype(BF16))
    num = _dot(s.astype(BF16), v) + w_inter * inter
    qn = jnp.sum(q.astype(F32) * n_st, axis=1, keepdims=True)
    den = jnp.sum(s, axis=1, keepdims=True) + w_inter * qn
    h = num / jnp.maximum(jnp.abs(den), jnp.exp(-m_row))
    g = b_last - b_c + li_c
    m_new = jnp.maximum(b_last + m_st, jnp.max(g, axis=0, keepdims=True))
    decay = jnp.exp(b_last + m_st - m_new)
    kw = k.astype(F32) * jnp.exp(g - m_new)
    c_new = decay * c_st + lax.dot_general(kw.astype(BF16), v, (((0,), (0,)), ((), ())),
                                           preferred_element_type=F32)
    n_new = decay * n_st + jnp.sum(kw, axis=0, keepdims=True)
    return h, c_new, n_new, m_new


def _mlstm_scan_kernel(q_ref, k_ref, v_ref, o_ref, gr_ref, nw_ref, z_ref, hf_ref, hb_ref, c_ref,
                       *, n_chunk, n_ctx_chunk, dqk, dv, n_hp):
    c_ref[...] = jnp.zeros_like(c_ref)
    n0 = jnp.zeros((1, dqk), F32)
    m0 = jnp.full((1, 1), M_INIT, F32)

    def step(i, carry):
        jf = i
        jb = jnp.where(i < n_ctx_chunk, n_ctx_chunk - 1 - i, n_chunk - 1 - (i - n_ctx_chunk))
        rows = (pl.ds(pl.multiple_of(jf * CHUNK, CHUNK), CHUNK), pl.ds(pl.multiple_of(jb * CHUNK, CHUNK), CHUNK))
        chunk = (jf, jb)
        scans = [(hp, direction) for hp in range(n_hp) for direction in range(2)]
        loaded = []
        for hp, direction in scans:
            r, j = rows[direction], chunk[direction]
            qc = slice(hp * dqk, (hp + 1) * dqk)
            vc = slice(hp * dv, (hp + 1) * dv)
            loaded.append((q_ref[r, qc], k_ref[r, qc], v_ref[r, vc],
                           gr_ref[0, 2 * direction, hp, j], gr_ref[0, 2 * direction + 1, hp, j],
                           c_ref[2 * hp + direction]))
        results = [_mlstm_chunk(*loaded[s], *carry[s], backward=direction == 1)
                   for s, (hp, direction) in enumerate(scans)]
        for s, (hp, direction) in enumerate(scans):
            h, c_new, _, _ = results[s]
            c_ref[s] = c_new
            (hf_ref, hb_ref)[direction][rows[direction], hp * dv:(hp + 1) * dv] = h
        return tuple((n_new, m_new) for _, _, n_new, m_new in results)

    lax.fori_loop(0, n_chunk, step, ((n0, m0),) * (2 * n_hp))
    n_ctx = n_ctx_chunk * CHUNK
    n_lat = (n_chunk - n_ctx_chunk) * CHUNK
    lat = pl.ds(n_ctx, n_lat)
    for hp in range(n_hp):
        vc = slice(hp * dv, (hp + 1) * dv)
        h = hf_ref[lat, vc] + hb_ref[lat, vc]
        hn = h * lax.rsqrt(jnp.mean(h * h, axis=-1, keepdims=True) + EPS)
        y = hn * nw_ref[:, vc] * _sigmoid(o_ref[lat, vc].astype(F32))
        z_ref[:, vc] = y.astype(BF16)


def _mlstm_scan(p, gr, norm_w, *, n_batch, n_head, seq_all, n_ctx, d):
    dqk = d // (2 * n_head)
    dv = d // n_head
    n_hp = 2 if n_head % 2 == 0 else 1
    n_chunk = seq_all // CHUNK
    n_lat = seq_all - n_ctx
    qk_blocks = (d // 2) // (n_hp * dqk)
    v_blocks = d // (n_hp * dv)
    return pl.pallas_call(
        functools.partial(_mlstm_scan_kernel, n_chunk=n_chunk, n_ctx_chunk=n_ctx // CHUNK, dqk=dqk, dv=dv,
                          n_hp=n_hp),
        grid=(n_batch, n_head // n_hp),
        in_specs=[
            pl.BlockSpec((seq_all, n_hp * dqk), lambda b, h: (b, h)),
            pl.BlockSpec((seq_all, n_hp * dqk), lambda b, h: (b, qk_blocks + h)),
            pl.BlockSpec((seq_all, n_hp * dv), lambda b, h: (b, v_blocks + h)),
            pl.BlockSpec((seq_all, n_hp * dv), lambda b, h: (b, 2 * v_blocks + h)),
            pl.BlockSpec((1, 4, n_hp, n_chunk, 1, CHUNK), lambda b, h: (b, 0, h, 0, 0, 0)),
            pl.BlockSpec((1, n_hp * dv), lambda b, h: (0, h)),
        ],
        out_specs=pl.BlockSpec((n_lat, n_hp * dv), lambda b, h: (b, h)),
        out_shape=jax.ShapeDtypeStruct((n_batch * n_lat, d), BF16),
        scratch_shapes=[
            pltpu.VMEM((seq_all, n_hp * dv), F32),
            pltpu.VMEM((seq_all, n_hp * dv), F32),
            pltpu.VMEM((2 * n_hp, dqk, dv), F32),
        ],
        compiler_params=_cparams("parallel", "parallel"),
        name="mlstm_scan",
    )(p, p, p, p, gr, norm_w)


def kernel(x, c, ctx, c_ctx, ada_w, ada_b, norm_mix_w, norm_ffn_w, conv_in_w, conv_dw_w, conv_out_w,
           mlstm_in_w, mlstm_gate_b, mlstm_norm_w, mlstm_out_w, router_w, router_bias,
           exp_gate_w, exp_up_w, exp_down_w, shared_gate_w, shared_up_w, shared_down_w, final_norm_w):
    n_batch, seq, d = x.shape
    n_ctx = ctx.shape[1]
    assert ada_w.shape[0] == 2 and n_ctx == TILE and seq % TILE == 0 and n_batch + 1 <= ADA_ROWS
    seq_all = n_ctx + seq
    tiles_per_b = seq_all // TILE
    lat_tiles_per_b = seq // TILE
    n_head = (mlstm_in_w.shape[2] - 3 * d) // 4

    cond = jnp.zeros((ADA_ROWS, d), F32).at[:n_batch].set(c).at[n_batch].set(c_ctx)
    mod = _ada_mod(cond, ada_w, ada_b)
    mod0 = mod[0].reshape(ADA_ROWS, 1, ADA_CHUNKS * d)
    mod1 = mod[1].reshape(ADA_ROWS, 1, ADA_CHUNKS * d)

    def all_mod_row(i):
        return jnp.where(i % tiles_per_b == 0, n_batch, i // tiles_per_b)

    def lat_mod_row(i):
        return i // lat_tiles_per_b

    def lat_tile(i):
        return (i // lat_tiles_per_b) * tiles_per_b + 1 + i % lat_tiles_per_b

    row = lambda w: w.reshape(1, -1)
    bf = lambda w: w.astype(BF16)
    x0 = jnp.concatenate([ctx, x], axis=1).reshape(n_batch * seq_all, d)

    z0 = _conv_in(x0, mod0, row(norm_mix_w[0]), bf(conv_in_w[0]), conv_dw_w[0],
                  n_batch=n_batch, tiles_per_b=tiles_per_b)
    xn0, h20, eidx0, gate0, rank0, cnt0 = _post(
        z0, x0, mod0, row(norm_ffn_w[0]), bf(conv_out_w[0]), router_w[0], row(router_bias[0]),
        x_tile_map=lambda i: i, mod_row_map=all_mod_row)
    x1 = _moe(h20, xn0, eidx0, gate0, rank0, cnt0, mod0, exp_gate_w, exp_up_w, exp_down_w,
              bf(shared_gate_w[0]), bf(shared_up_w[0]), bf(shared_down_w[0]), row(final_norm_w),
              layer=0, mod_row_map=all_mod_row, final=False)

    w_in = mlstm_in_w[0]
    p, g = _mlstm_in(x1, mod1, row(norm_mix_w[1]), bf(w_in[:, :3 * d]), w_in[:, 3 * d:], row(mlstm_gate_b[0]),
                     n_batch=n_batch, tiles_per_b=tiles_per_b, n_head=n_head)
    n_chunk = seq_all // CHUNK
    gr = g.reshape(n_batch, n_chunk, CHUNK, 4, n_head).transpose(0, 3, 4, 1, 2)
    gr = gr.reshape(n_batch, 4, n_head, n_chunk, 1, CHUNK)
    z1 = _mlstm_scan(p, gr, row(mlstm_norm_w[0]), n_batch=n_batch, n_head=n_head, seq_all=seq_all,
                     n_ctx=n_ctx, d=d)
    xn1, h21, eidx1, gate1, rank1, cnt1 = _post(
        z1, x1, mod1, row(norm_ffn_w[1]), bf(mlstm_out_w[0]), router_w[1], row(router_bias[1]),
        x_tile_map=lat_tile, mod_row_map=lat_mod_row)
    out = _moe(h21, xn1, eidx1, gate1, rank1, cnt1, mod1, exp_gate_w, exp_up_w, exp_down_w,
               bf(shared_gate_w[1]), bf(shared_up_w[1]), bf(shared_down_w[1]), row(final_norm_w),
               layer=1, mod_row_map=lat_mod_row, final=True)
    return out.reshape(n_batch, seq, d)
```

```python
import functools

import jax
import jax.numpy as jnp
from jax import lax
from jax.experimental import pallas as pl
from jax.experimental.pallas import tpu as pltpu

F32 = jnp.float32
BF16 = jnp.bfloat16
I32 = jnp.int32

TILE = 256
GRID_W = 64
CHUNK = 64
TOP_K = 6
MOE_BLK = 256
IDX_W = 8
ADA_CHUNKS = 6
ADA_ROWS = 16
EPS = 1e-6
GATE_CAP = 15.0
M_INIT = -1e30
ROUTED_SCALE = 2.5
V7X_VMEM_LIMIT = 56 * 1024 * 1024


def _cparams(*sem):
    return pltpu.CompilerParams(dimension_semantics=sem, vmem_limit_bytes=V7X_VMEM_LIMIT)


def _resident(shape, index_map):
    return pl.BlockSpec(shape, index_map, pipeline_mode=pl.Buffered(1))


def _sigmoid(x):
    return 1.0 / (1.0 + jnp.exp(-x))


def _silu(x):
    return x * _sigmoid(x)


def _split3(a):
    hi = a.astype(BF16)
    r1 = a - hi.astype(F32)
    mid = r1.astype(BF16)
    lo = (r1 - mid.astype(F32)).astype(BF16)
    return hi, mid, lo


def _dot(a, b):
    return jnp.dot(a, b, preferred_element_type=F32)


def _dot_hp(a, b):
    a0, a1, a2 = _split3(a)
    b0, b1, b2 = _split3(b)
    return (_dot(a0, b0) + (_dot(a0, b1) + _dot(a1, b0))
            + (_dot(a0, b2) + _dot(a1, b1) + _dot(a2, b0)))


def _norm_mod(x, w, shift, scale):
    y = x * lax.rsqrt(jnp.mean(x * x, axis=-1, keepdims=True) + EPS)
    return (y * w) * (1.0 + scale) + shift


def _ada_kernel(cond_ref, w_ref, b_ref, o_ref):
    a = _silu(cond_ref[...]).astype(BF16)
    o_ref[0] = _dot(a, w_ref[0].astype(BF16)) + b_ref[0]


def _ada_mod(cond, ada_w, ada_b):
    n_layer, d, n_out = ada_w.shape
    tn = 1024 if n_out % 1024 == 0 else n_out
    return pl.pallas_call(
        _ada_kernel,
        grid=(n_layer, n_out // tn),
        in_specs=[
            pl.BlockSpec((ADA_ROWS, d), lambda l, j: (0, 0)),
            pl.BlockSpec((1, d, tn), lambda l, j: (l, 0, j)),
            pl.BlockSpec((1, 1, tn), lambda l, j: (l, 0, j)),
        ],
        out_specs=pl.BlockSpec((1, ADA_ROWS, tn), lambda l, j: (l, 0, j)),
        out_shape=jax.ShapeDtypeStruct((n_layer, ADA_ROWS, n_out), F32),
        compiler_params=_cparams("parallel", "parallel"),
        name="ada_mod",
    )(cond, ada_w, ada_b.reshape(n_layer, 1, n_out))


def _conv_in_kernel(x_ref, mod_ref, nw_ref, win_ref, wdw_ref, z_ref, *, d, cn, tiles_per_b):
    i = pl.program_id(0)
    mod = mod_ref[0]
    h = _norm_mod(x_ref[...], nw_ref[...], mod[:, 0:d], mod[:, d:2 * d]).astype(BF16)
    t = lax.broadcasted_iota(I32, (TILE, 1), 0)
    pos_mask = jnp.where((i % tiles_per_b) == 0, TILE - 1, GRID_W - 1)
    pos = jnp.bitwise_and(t, pos_mask)
    first = pos == 0
    last = pos == pos_mask
    for j in range(d // cn):
        c0 = j * cn
        bg = _dot(h, win_ref[:, c0:c0 + cn])
        cg = _dot(h, win_ref[:, d + c0:d + c0 + cn])
        hi = _dot(h, win_ref[:, 2 * d + c0:2 * d + c0 + cn])
        u = cg * hi
        u_prev = jnp.where(first, 0.0, pltpu.roll(u, 1, 0))
        u_next = jnp.where(last, 0.0, pltpu.roll(u, TILE - 1, 0))
        w = wdw_ref[:, c0:c0 + cn]
        y = u_prev * w[0:1] + u * w[1:2] + u_next * w[2:3]
        z_ref[:, c0:c0 + cn] = (bg * y).astype(BF16)


def _conv_in(xt, mod, nw, w_in, w_dw, *, n_batch, tiles_per_b):
    n_tok, d = xt.shape
    n_tiles = n_tok // TILE
    cn = min(512, d)

    def mod_map(i):
        return (jnp.where(i % tiles_per_b == 0, n_batch, i // tiles_per_b), 0, 0)

    return pl.pallas_call(
        functools.partial(_conv_in_kernel, d=d, cn=cn, tiles_per_b=tiles_per_b),
        grid=(n_tiles,),
        in_specs=[
            pl.BlockSpec((TILE, d), lambda i: (i, 0)),
            pl.BlockSpec((1, 1, ADA_CHUNKS * d), mod_map),
            pl.BlockSpec((1, d), lambda i: (0, 0)),
            _resident((d, 3 * d), lambda i: (0, 0)),
            pl.BlockSpec((3, d), lambda i: (0, 0)),
        ],
        out_specs=pl.BlockSpec((TILE, d), lambda i: (i, 0)),
        out_shape=jax.ShapeDtypeStruct((n_tok, d), BF16),
        compiler_params=_cparams("parallel"),
        name="conv_in",
    )(xt, mod, nw, w_in, w_dw)


def _post_kernel(z_ref, x_ref, mod_ref, nw_ref, wout_ref, rwcat_ref, rwhi_ref, rb_ref,
                 xn_ref, h2_ref, eidx_ref, gate_ref, rank_ref, cnt_ref, carry_ref, *, d, n_exp):
    i = pl.program_id(0)

    @pl.when(i == 0)
    def _():
        carry_ref[...] = jnp.zeros_like(carry_ref)

    mod = mod_ref[0]
    y = _dot(z_ref[...], wout_ref[...])
    xn = x_ref[...] + mod[:, 2 * d:3 * d] * y
    xn_ref[...] = xn
    h2 = _norm_mod(xn, nw_ref[...], mod[:, 3 * d:4 * d], mod[:, 4 * d:5 * d])
    h2_ref[...] = h2

    h2_hi = h2.astype(BF16)
    h2_lo = (h2 - h2_hi.astype(F32)).astype(BF16)
    p_hi = _dot(h2_hi, rwcat_ref[...])
    logits = p_hi[:, :n_exp] + (p_hi[:, n_exp:] + _dot(h2_lo, rwhi_ref[...]))
    scores = _sigmoid(logits)
    lane = lax.broadcasted_iota(I32, (TILE, n_exp), 1)
    lane_f = lane.astype(F32)
    work = scores + rb_ref[...]
    onehots, picks = [], []
    for _ in range(TOP_K):
        mx = jnp.max(work, axis=1, keepdims=True)
        first_max = jnp.min(jnp.where(work == mx, lane_f, float(n_exp)), axis=1, keepdims=True)
        oh = lane_f == first_max
        onehots.append(oh)
        picks.append(first_max)
        work = jnp.where(oh, -jnp.inf, work)
    sel = onehots[0]
    for oh in onehots[1:]:
        sel = jnp.logical_or(sel, oh)
    picked = jnp.where(sel, scores, 0.0)
    gates = picked / jnp.sum(picked, axis=1, keepdims=True) * ROUTED_SCALE
    sel_f = jnp.where(sel, 1.0, 0.0)
    r_i = lax.broadcasted_iota(I32, (TILE, TILE), 0)
    c_i = lax.broadcasted_iota(I32, (TILE, TILE), 1)
    before = jnp.where(c_i < r_i, 1.0, 0.0).astype(BF16)
    cum = _dot(before, sel_f.astype(BF16)) + carry_ref[...]

    eidx_ref[...] = jnp.zeros_like(eidx_ref)
    gate_ref[...] = jnp.zeros_like(gate_ref)
    rank_ref[...] = jnp.zeros_like(rank_ref)
    for k, oh in enumerate(onehots):
        eidx_ref[:, k:k + 1] = picks[k].astype(I32)
        gate_ref[:, k:k + 1] = jnp.sum(jnp.where(oh, gates, 0.0), axis=1, keepdims=True)
        rank_ref[:, k:k + 1] = jnp.sum(jnp.where(oh, cum, 0.0), axis=1, keepdims=True).astype(I32)

    total = carry_ref[...] + jnp.sum(sel_f, axis=0, keepdims=True)
    carry_ref[...] = total
    cnt_ref[...] = total


def _post(z, xt, mod, nw, w_out, router_w, router_b, *, x_tile_map, mod_row_map):
    rw_hi = router_w.astype(BF16)
    rw_lo = (router_w - rw_hi.astype(F32)).astype(BF16)
    rw_cat = jnp.concatenate([rw_hi, rw_lo], axis=1)
    n_tok, d = z.shape
    n_tiles = n_tok // TILE
    n_exp = router_w.shape[1]
    outs = pl.pallas_call(
        functools.partial(_post_kernel, d=d, n_exp=n_exp),
        grid=(n_tiles,),
        in_specs=[
            pl.BlockSpec((TILE, d), lambda i: (i, 0)),
            pl.BlockSpec((TILE, d), lambda i: (x_tile_map(i), 0)),
            pl.BlockSpec((1, 1, ADA_CHUNKS * d), lambda i: (mod_row_map(i), 0, 0)),
            pl.BlockSpec((1, d), lambda i: (0, 0)),
            _resident((d, d), lambda i: (0, 0)),
            pl.BlockSpec((d, 2 * n_exp), lambda i: (0, 0)),
            pl.BlockSpec((d, n_exp), lambda i: (0, 0)),
            pl.BlockSpec((1, n_exp), lambda i: (0, 0)),
        ],
        out_specs=[
            pl.BlockSpec((TILE, d), lambda i: (i, 0)),
            pl.BlockSpec((TILE, d), lambda i: (i, 0)),
            pl.BlockSpec((TILE, IDX_W), lambda i: (i, 0)),
            pl.BlockSpec((TILE, IDX_W), lambda i: (i, 0)),
            pl.BlockSpec((TILE, IDX_W), lambda i: (i, 0)),
            pl.BlockSpec((1, n_exp), lambda i: (0, 0)),
        ],
        out_shape=[
            jax.ShapeDtypeStruct((n_tok, d), F32),
            jax.ShapeDtypeStruct((n_tok, d), F32),
            jax.ShapeDtypeStruct((n_tok, IDX_W), I32),
            jax.ShapeDtypeStruct((n_tok, IDX_W), F32),
            jax.ShapeDtypeStruct((n_tok, IDX_W), I32),
            jax.ShapeDtypeStruct((1, n_exp), F32),
        ],
        scratch_shapes=[pltpu.VMEM((1, n_exp), F32)],
        compiler_params=_cparams("arbitrary"),
        name="post_mixer",
    )(z, xt, mod, nw, w_out, rw_cat, rw_hi, router_b)
    return outs


def _experts_kernel(be_ref, meta_ref, src_cur_ref, src_next_ref, dst_prev_ref, dst_cur_ref, h2_ref, wg_ref, wu_ref, wd_ref,
                    yk_ref, wg_s, wu_s, wd_s, xbuf, ybuf, gsem, ssem, *, dump_base):
    i = pl.program_id(0)
    n_used = meta_ref[0]
    used = i < n_used
    cur = i % 2
    nxt = 1 - cur

    def gather(src_ref, r, buf):
        return pltpu.make_async_copy(h2_ref.at[pl.ds(src_ref[0, 0, r], 1)], xbuf.at[buf, pl.ds(r, 1)], gsem)

    def scatter(r, buf, dst):
        return pltpu.make_async_copy(ybuf.at[buf, pl.ds(r, 1)], yk_ref.at[pl.ds(dst, 1)], ssem)

    def wait_gathers(buf):
        pltpu.make_async_copy(h2_ref.at[pl.ds(0, MOE_BLK)], xbuf.at[buf], gsem).wait()

    def wait_scatters(buf):
        pltpu.make_async_copy(ybuf.at[buf], yk_ref.at[pl.ds(0, MOE_BLK)], ssem).wait()

    @pl.when(jnp.logical_and(used, jnp.logical_or(i == 0, be_ref[i] != be_ref[jnp.maximum(i - 1, 0)])))
    def _():
        wg_s[...] = wg_ref[0, 0].astype(BF16)
        wu_s[...] = wu_ref[0, 0].astype(BF16)
        wd_s[...] = wd_ref[0, 0].astype(BF16)

    @pl.when(i == 0)
    def _():
        ybuf[1] = jnp.zeros(ybuf.shape[1:], ybuf.dtype)

        def first(r, c):
            gather(src_cur_ref, r, 0).start()
            return c
        lax.fori_loop(0, MOE_BLK, first, 0)

    @pl.when(used)
    def _():
        wait_gathers(cur)
        first_step = i == 0

        def prev_dst(r):
            return jnp.where(first_step, dump_base + r, dst_prev_ref[0, 0, r])
        n_group = 8
        per_group = MOE_BLK // n_group

        def issue(g):
            for r in range(g * per_group, (g + 1) * per_group):
                gather(src_next_ref, r, nxt).start()
                scatter(r, nxt, prev_dst(r)).start()

        f = wg_s.shape[1]
        d = wd_s.shape[1]
        xb = xbuf[cur].astype(BF16)
        hg, hu = [], []
        for half in range(2):
            cols = slice(half * (f // 2), (half + 1) * (f // 2))
            issue(2 * half)
            hg.append(_dot(xb, wg_s[:, cols]))
            issue(2 * half + 1)
            hu.append(_dot(xb, wu_s[:, cols]))
        a = (_silu(jnp.concatenate(hg, axis=1)) * jnp.concatenate(hu, axis=1)).astype(BF16)
        for q in range(4):
            cols = slice(q * (d // 4), (q + 1) * (d // 4))
            issue(4 + q)
            ybuf[cur, :, cols] = _dot(a, wd_s[:, cols])
        wait_scatters(nxt)

    @pl.when(i == n_used - 1)
    def _():
        def last(r, c):
            scatter(r, cur, dst_cur_ref[0, 0, r]).start()
            return c
        lax.fori_loop(0, MOE_BLK, last, 0)
        wait_scatters(cur)
        wait_gathers(nxt)


def _experts(blk_expert, n_used, src_tok, dst_row, h2, w_gate, w_up, w_down, layer, n_rows_out):
    n_slots = src_tok.shape[0]
    d = h2.shape[1]
    n_blk = n_slots // MOE_BLK
    f = w_gate.shape[3]
    src3 = src_tok.reshape(n_blk, 1, MOE_BLK)
    dst3 = dst_row.reshape(n_blk, 1, MOE_BLK)

    def last_used(meta):
        return jnp.maximum(meta[0] - 1, 0)

    def smem(index_map):
        return pl.BlockSpec((1, 1, MOE_BLK), index_map, memory_space=pltpu.SMEM)

    grid_spec = pltpu.PrefetchScalarGridSpec(
        num_scalar_prefetch=2,
        grid=(n_blk,),
        in_specs=[
            smem(lambda i, be, meta: (jnp.minimum(i, last_used(meta)), 0, 0)),
            smem(lambda i, be, meta: (jnp.minimum(i + 1, last_used(meta)), 0, 0)),
            smem(lambda i, be, meta: (jnp.maximum(i - 1, 0), 0, 0)),
            smem(lambda i, be, meta: (jnp.minimum(i, last_used(meta)), 0, 0)),
            pl.BlockSpec(memory_space=pl.ANY),
            pl.BlockSpec((1, 1, d, f), lambda i, be, meta: (layer, be[i], 0, 0)),
            pl.BlockSpec((1, 1, d, f), lambda i, be, meta: (layer, be[i], 0, 0)),
            pl.BlockSpec((1, 1, f, d), lambda i, be, meta: (layer, be[i], 0, 0)),
        ],
        out_specs=pl.BlockSpec(memory_space=pl.ANY),
        scratch_shapes=[pltpu.VMEM((d, f), BF16), pltpu.VMEM((d, f), BF16), pltpu.VMEM((f, d), BF16),
                        pltpu.VMEM((2, MOE_BLK, d), F32), pltpu.VMEM((2, MOE_BLK, d), F32),
                        pltpu.SemaphoreType.DMA(()), pltpu.SemaphoreType.DMA(())],
    )
    return pl.pallas_call(
        functools.partial(_experts_kernel, dump_base=n_rows_out - MOE_BLK),
        grid_spec=grid_spec,
        out_shape=jax.ShapeDtypeStruct((n_rows_out, d), F32),
        compiler_params=_cparams("arbitrary"),
        name="moe_experts",
    )(blk_expert, n_used, src3, src3, dst3, dst3, h2, w_gate, w_up, w_down)


def _combine_kernel(gate_ref, h2_ref, xn_ref, mod_ref, sg_ref, su_ref, sd_ref, fw_ref, *rest, d, final):
    yk_refs, o_ref = rest[:TOP_K], rest[TOP_K]
    hb = h2_ref[...].astype(BF16)
    a = _silu(_dot(hb, sg_ref[...])) * _dot(hb, su_ref[...])
    acc = _dot(a.astype(BF16), sd_ref[...])
    gate = gate_ref[...]
    for k in range(TOP_K):
        acc = acc + gate[:, k:k + 1] * yk_refs[k][...]
    x2 = xn_ref[...] + mod_ref[0][:, 5 * d:6 * d] * acc
    if final:
        x2 = x2 * lax.rsqrt(jnp.mean(x2 * x2, axis=-1, keepdims=True) + EPS) * fw_ref[...]
    o_ref[...] = x2


def _combine(gates, h2, xn, mod, sh_gate, sh_up, sh_down, fw, yk, *, mod_row_map, final):
    n_tok, d = h2.shape
    n_tiles = n_tok // TILE
    f = sh_gate.shape[1]
    return pl.pallas_call(
        functools.partial(_combine_kernel, d=d, final=final),
        grid=(n_tiles,),
        in_specs=[
            pl.BlockSpec((TILE, IDX_W), lambda i: (i, 0)),
            pl.BlockSpec((TILE, d), lambda i: (i, 0)),
            pl.BlockSpec((TILE, d), lambda i: (i, 0)),
            pl.BlockSpec((1, 1, ADA_CHUNKS * d), lambda i: (mod_row_map(i), 0, 0)),
            _resident((d, f), lambda i: (0, 0)),
            _resident((d, f), lambda i: (0, 0)),
            _resident((f, d), lambda i: (0, 0)),
            pl.BlockSpec((1, d), lambda i: (0, 0)),
        ] + [pl.BlockSpec((TILE, d), lambda i, k=k: (k * n_tiles + i, 0)) for k in range(TOP_K)],
        out_specs=pl.BlockSpec((TILE, d), lambda i: (i, 0)),
        out_shape=jax.ShapeDtypeStruct((n_tok, d), F32),
        compiler_params=_cparams("parallel"),
        name="moe_combine",
    )(gates, h2, xn, mod, sh_gate, sh_up, sh_down, fw, *([yk] * TOP_K))


def _moe(h2, xn, eidx, gates, rank, counts, mod, w_gate, w_up, w_down, sh_gate, sh_up, sh_down, fw,
         *, layer, mod_row_map, final):
    n_tok = h2.shape[0]
    n_exp = w_gate.shape[1]
    n_assign = n_tok * TOP_K
    n_blk = (n_assign + n_exp * (MOE_BLK - 1) + MOE_BLK - 1) // MOE_BLK
    n_slots = n_blk * MOE_BLK
    cnt = counts[0].astype(I32)
    padded = (cnt + MOE_BLK - 1) // MOE_BLK * MOE_BLK
    pend = jnp.cumsum(padded)
    pstart = pend - padded
    expert_ids = jnp.arange(n_exp, dtype=I32)
    slots = jnp.sum(jnp.where(eidx[:, :TOP_K, None] == expert_ids, pstart, 0), axis=-1) + rank[:, :TOP_K]
    n_used = pend[-1] // MOE_BLK
    blk = jnp.arange(n_blk, dtype=I32)
    be = jnp.sum((pend[None, :] <= blk[:, None] * MOE_BLK).astype(I32), axis=1)
    last_used = jnp.sum(jnp.where(blk == n_used - 1, be, 0))
    be = jnp.minimum(jnp.where(blk < n_used, be, last_used), n_exp - 1)
    assign = jnp.full((n_slots,), -1, I32).at[slots.reshape(-1)].set(
        jnp.arange(n_assign, dtype=I32), unique_indices=True)
    filled = assign >= 0
    src_tok = jnp.where(filled, assign // TOP_K, 0)
    dst_row = jnp.where(filled, (assign % TOP_K) * n_tok + assign // TOP_K,
                        n_assign + jnp.arange(n_slots, dtype=I32) % MOE_BLK)
    yk = _experts(be, n_used.reshape(1).astype(I32), src_tok, dst_row, h2, w_gate, w_up, w_down, layer,
                  n_assign + MOE_BLK)
    return _combine(gates, h2, xn, mod, sh_gate, sh_up, sh_down, fw, yk, mod_row_map=mod_row_map, final=final)


def _log_sigmoid(x):
    return jnp.minimum(x, 0.0) - jnp.log1p(jnp.exp(-jnp.abs(x)))


def _mlstm_in_kernel(x_ref, mod_ref, nw_ref, w_ref, wg_ref, gb_ref, p_ref, g_ref, *, d, cn, qk_w, n_head, k_scale):
    mod = mod_ref[0]
    hf = _norm_mod(x_ref[...], nw_ref[...], mod[:, 0:d], mod[:, d:2 * d])
    h = hf.astype(BF16)
    for j in range(3 * d // cn):
        c0 = j * cn
        p = _dot(h, w_ref[:, c0:c0 + cn])
        if qk_w <= c0 < 2 * qk_w:
            p = p * k_scale
        p_ref[:, c0:c0 + cn] = p.astype(BF16)
    g = _dot_hp(hf, wg_ref[...]) + gb_ref[...]
    g = GATE_CAP * jnp.tanh(g / GATE_CAP)
    col = lax.broadcasted_iota(I32, g.shape, 1)
    is_forget = (col // n_head) % 2 == 1
    g_ref[...] = jnp.where(is_forget, _log_sigmoid(g), g)


def _mlstm_in(xt, mod, nw, w_qkvo, w_g, gate_b, *, n_batch, tiles_per_b, n_head):
    n_tok, d = xt.shape
    n_tiles = n_tok // TILE
    qk_w = d // 2
    cn = min(512, qk_w)
    n_g = w_g.shape[1]
    k_scale = float((qk_w // n_head) ** -0.5)

    def mod_map(i):
        return (jnp.where(i % tiles_per_b == 0, n_batch, i // tiles_per_b), 0, 0)

    return pl.pallas_call(
        functools.partial(_mlstm_in_kernel, d=d, cn=cn, qk_w=qk_w, n_head=n_head, k_scale=k_scale),
        grid=(n_tiles,),
        in_specs=[
            pl.BlockSpec((TILE, d), lambda i: (i, 0)),
            pl.BlockSpec((1, 1, ADA_CHUNKS * d), mod_map),
            pl.BlockSpec((1, d), lambda i: (0, 0)),
            _resident((d, 3 * d), lambda i: (0, 0)),
            pl.BlockSpec((d, n_g), lambda i: (0, 0)),
            pl.BlockSpec((1, n_g), lambda i: (0, 0)),
        ],
        out_specs=[
            pl.BlockSpec((TILE, 3 * d), lambda i: (i, 0)),
            pl.BlockSpec((TILE, n_g), lambda i: (i, 0)),
        ],
        out_shape=[
            jax.ShapeDtypeStruct((n_tok, 3 * d), BF16),
            jax.ShapeDtypeStruct((n_tok, n_g), F32),
        ],
        compiler_params=_cparams("parallel"),
        name="mlstm_in",
    )(xt, mod, nw, w_qkvo, w_g, gate_b)


def _mlstm_chunk(q, k, v, li_r, lf_r, c_st, n_st, m_st, *, backward):
    n_t = q.shape[0]
    tt = lax.broadcasted_iota(I32, (n_t, n_t), 0)
    ss = lax.broadcasted_iota(I32, (n_t, n_t), 1)
    diag = tt == ss
    if backward:
        seen = ss >= tt
        seen_t = tt >= ss
    else:
        seen = ss <= tt
        seen_t = tt <= ss
    lf_b = jnp.broadcast_to(lf_r, (n_t, n_t))
    li_b = jnp.broadcast_to(li_r, (n_t, n_t))
    b_c = jnp.sum(jnp.where(seen, lf_b, 0.0), axis=1, keepdims=True)
    lf_c = jnp.sum(jnp.where(diag, lf_b, 0.0), axis=1, keepdims=True)
    li_c = jnp.sum(jnp.where(diag, li_b, 0.0), axis=1, keepdims=True)
    b_r = jnp.sum(jnp.where(seen_t, lf_c, 0.0), axis=0, keepdims=True)
    b_last = jnp.sum(lf_r, axis=1, keepdims=True)
    dmat = jnp.where(seen, b_c - b_r + li_b, -jnp.inf)
    a = b_c + m_st
    m_row = jnp.maximum(a, jnp.max(dmat, axis=1, keepdims=True))
    w_intra = jnp.exp(dmat - m_row)
    w_inter = jnp.exp(a - m_row)
    s = lax.dot_general(q, k, (((1,), (1,)), ((), ())), preferred_element_type=F32) * w_intra
    inter = _dot(q, c_st.astype(BF16))
    num = _dot(s.astype(BF16), v) + w_inter * inter
    qn = jnp.sum(q.astype(F32) * n_st, axis=1, keepdims=True)
    den = jnp.sum(s, axis=1, keepdims=True) + w_inter * qn
    h = num / jnp.maximum(jnp.abs(den), jnp.exp(-m_row))
    g = b_last - b_c + li_c
    m_new = jnp.maximum(b_last + m_st, jnp.max(g, axis=0, keepdims=True))
    decay = jnp.exp(b_last + m_st - m_new)
    kw = k.astype(F32) * jnp.exp(g - m_new)
    c_new = decay * c_st + lax.dot_general(kw.astype(BF16), v, (((0,), (0,)), ((), ())),
                                           preferred_element_type=F32)
    n_new = decay * n_st + jnp.sum(kw, axis=0, keepdims=True)
    return h, c_new, n_new, m_new


def _mlstm_scan_kernel(q_ref, k_ref, v_ref, o_ref, gr_ref, nw_ref, z_ref, hf_ref, hb_ref, c_ref,
                       *, n_chunk, n_ctx_chunk, dqk, dv, n_hp):
    c_ref[...] = jnp.zeros_like(c_ref)
    n0 = jnp.zeros((1, dqk), F32)
    m0 = jnp.full((1, 1), M_INIT, F32)

    def step(i, carry):
        jf = i
        jb = jnp.where(i < n_ctx_chunk, n_ctx_chunk - 1 - i, n_chunk - 1 - (i - n_ctx_chunk))
        rows = (pl.ds(pl.multiple_of(jf * CHUNK, CHUNK), CHUNK), pl.ds(pl.multiple_of(jb * CHUNK, CHUNK), CHUNK))
        chunk = (jf, jb)
        scans = [(hp, direction) for hp in range(n_hp) for direction in range(2)]
        loaded = []
        for hp, direction in scans:
            r, j = rows[direction], chunk[direction]
            qc = slice(hp * dqk, (hp + 1) * dqk)
            vc = slice(hp * dv, (hp + 1) * dv)
            loaded.append((q_ref[r, qc], k_ref[r, qc], v_ref[r, vc],
                           gr_ref[0, 2 * direction, hp, j], gr_ref[0, 2 * direction + 1, hp, j],
                           c_ref[2 * hp + direction]))
        results = [_mlstm_chunk(*loaded[s], *carry[s], backward=direction == 1)
                   for s, (hp, direction) in enumerate(scans)]
        for s, (hp, direction) in enumerate(scans):
            h, c_new, _, _ = results[s]
            c_ref[s] = c_new
            (hf_ref, hb_ref)[direction][rows[direction], hp * dv:(hp + 1) * dv] = h
        return tuple((n_new, m_new) for _, _, n_new, m_new in results)

    lax.fori_loop(0, n_chunk, step, ((n0, m0),) * (2 * n_hp))
    n_ctx = n_ctx_chunk * CHUNK
    n_lat = (n_chunk - n_ctx_chunk) * CHUNK
    lat = pl.ds(n_ctx, n_lat)
    for hp in range(n_hp):
        vc = slice(hp * dv, (hp + 1) * dv)
        h = hf_ref[lat, vc] + hb_ref[lat, vc]
        hn = h * lax.rsqrt(jnp.mean(h * h, axis=-1, keepdims=True) + EPS)
        y = hn * nw_ref[:, vc] * _sigmoid(o_ref[lat, vc].astype(F32))
        z_ref[:, vc] = y.astype(BF16)


def _mlstm_scan(p, gr, norm_w, *, n_batch, n_head, seq_all, n_ctx, d):
    dqk = d // (2 * n_head)
    dv = d // n_head
    n_hp = 2 if n_head % 2 == 0 else 1
    n_chunk = seq_all // CHUNK
    n_lat = seq_all - n_ctx
    qk_blocks = (d // 2) // (n_hp * dqk)
    v_blocks = d // (n_hp * dv)
    return pl.pallas_call(
        functools.partial(_mlstm_scan_kernel, n_chunk=n_chunk, n_ctx_chunk=n_ctx // CHUNK, dqk=dqk, dv=dv,
                          n_hp=n_hp),
        grid=(n_batch, n_head // n_hp),
        in_specs=[
            pl.BlockSpec((seq_all, n_hp * dqk), lambda b, h: (b, h)),
            pl.BlockSpec((seq_all, n_hp * dqk), lambda b, h: (b, qk_blocks + h)),
            pl.BlockSpec((seq_all, n_hp * dv), lambda b, h: (b, v_blocks + h)),
            pl.BlockSpec((seq_all, n_hp * dv), lambda b, h: (b, 2 * v_blocks + h)),
            pl.BlockSpec((1, 4, n_hp, n_chunk, 1, CHUNK), lambda b, h: (b, 0, h, 0, 0, 0)),
            pl.BlockSpec((1, n_hp * dv), lambda b, h: (0, h)),
        ],
        out_specs=pl.BlockSpec((n_lat, n_hp * dv), lambda b, h: (b, h)),
        out_shape=jax.ShapeDtypeStruct((n_batch * n_lat, d), BF16),
        scratch_shapes=[
            pltpu.VMEM((seq_all, n_hp * dv), F32),
            pltpu.VMEM((seq_all, n_hp * dv), F32),
            pltpu.VMEM((2 * n_hp, dqk, dv), F32),
        ],
        compiler_params=_cparams("parallel", "parallel"),
        name="mlstm_scan",
    )(p, p, p, p, gr, norm_w)


def kernel(x, c, ctx, c_ctx, ada_w, ada_b, norm_mix_w, norm_ffn_w, conv_in_w, conv_dw_w, conv_out_w,
           mlstm_in_w, mlstm_gate_b, mlstm_norm_w, mlstm_out_w, router_w, router_bias,
           exp_gate_w, exp_up_w, exp_down_w, shared_gate_w, shared_up_w, shared_down_w, final_norm_w):
    n_batch, seq, d = x.shape
    n_ctx = ctx.shape[1]
    assert ada_w.shape[0] == 2 and n_ctx == TILE and seq % TILE == 0 and n_batch + 1 <= ADA_ROWS
    seq_all = n_ctx + seq
    tiles_per_b = seq_all // TILE
    lat_tiles_per_b = seq // TILE
    n_head = (mlstm_in_w.shape[2] - 3 * d) // 4

    cond = jnp.zeros((ADA_ROWS, d), F32).at[:n_batch].set(c).at[n_batch].set(c_ctx)
    mod = _ada_mod(cond, ada_w, ada_b)
    mod0 = mod[0].reshape(ADA_ROWS, 1, ADA_CHUNKS * d)
    mod1 = mod[1].reshape(ADA_ROWS, 1, ADA_CHUNKS * d)

    def all_mod_row(i):
        return jnp.where(i % tiles_per_b == 0, n_batch, i // tiles_per_b)

    def lat_mod_row(i):
        return i // lat_tiles_per_b

    def lat_tile(i):
        return (i // lat_tiles_per_b) * tiles_per_b + 1 + i % lat_tiles_per_b

    row = lambda w: w.reshape(1, -1)
    bf = lambda w: w.astype(BF16)
    x0 = jnp.concatenate([ctx, x], axis=1).reshape(n_batch * seq_all, d)

    z0 = _conv_in(x0, mod0, row(norm_mix_w[0]), bf(conv_in_w[0]), conv_dw_w[0],
                  n_batch=n_batch, tiles_per_b=tiles_per_b)
    xn0, h20, eidx0, gate0, rank0, cnt0 = _post(
        z0, x0, mod0, row(norm_ffn_w[0]), bf(conv_out_w[0]), router_w[0], row(router_bias[0]),
        x_tile_map=lambda i: i, mod_row_map=all_mod_row)
    x1 = _moe(h20, xn0, eidx0, gate0, rank0, cnt0, mod0, exp_gate_w, exp_up_w, exp_down_w,
              bf(shared_gate_w[0]), bf(shared_up_w[0]), bf(shared_down_w[0]), row(final_norm_w),
              layer=0, mod_row_map=all_mod_row, final=False)

    w_in = mlstm_in_w[0]
    p, g = _mlstm_in(x1, mod1, row(norm_mix_w[1]), bf(w_in[:, :3 * d]), w_in[:, 3 * d:], row(mlstm_gate_b[0]),
                     n_batch=n_batch, tiles_per_b=tiles_per_b, n_head=n_head)
    n_chunk = seq_all // CHUNK
    gr = g.reshape(n_batch, n_chunk, CHUNK, 4, n_head).transpose(0, 3, 4, 1, 2)
    gr = gr.reshape(n_batch, 4, n_head, n_chunk, 1, CHUNK)
    z1 = _mlstm_scan(p, gr, row(mlstm_norm_w[0]), n_batch=n_batch, n_head=n_head, seq_all=seq_all,
                     n_ctx=n_ctx, d=d)
    xn1, h21, eidx1, gate1, rank1, cnt1 = _post(
        z1, x1, mod1, row(norm_ffn_w[1]), bf(mlstm_out_w[0]), router_w[1], row(router_bias[1]),
        x_tile_map=lat_tile, mod_row_map=lat_mod_row)
    out = _moe(h21, xn1, eidx1, gate1, rank1, cnt1, mod1, exp_gate_w, exp_up_w, exp_down_w,
               bf(shared_gate_w[1]), bf(shared_up_w[1]), bf(shared_down_w[1]), row(final_norm_w),
               layer=1, mod_row_map=lat_mod_row, final=True)
    return out.reshape(n_batch, seq, d)
```

```python
import functools

import jax
import jax.numpy as jnp
from jax import lax
from jax.experimental import pallas as pl
from jax.experimental.pallas import tpu as pltpu

F32 = jnp.float32
BF16 = jnp.bfloat16
I32 = jnp.int32

TILE = 256
GRID_W = 64
CHUNK = 64
TOP_K = 6
MOE_BLK = 256
IDX_W = 8
SUBLANES = 8
ADA_CHUNKS = 6
ADA_ROWS = 16
EPS = 1e-6
GATE_CAP = 15.0
M_INIT = -1e30
ROUTED_SCALE = 2.5
V7X_VMEM_LIMIT = 56 * 1024 * 1024


def _cparams(*sem):
    return pltpu.CompilerParams(dimension_semantics=sem, vmem_limit_bytes=V7X_VMEM_LIMIT)


def _resident(shape, index_map):
    return pl.BlockSpec(shape, index_map, pipeline_mode=pl.Buffered(1))


def _sigmoid(x):
    return 1.0 / (1.0 + jnp.exp(-x))


def _silu(x):
    return x * _sigmoid(x)


def _split3(a):
    hi = a.astype(BF16)
    r1 = a - hi.astype(F32)
    mid = r1.astype(BF16)
    lo = (r1 - mid.astype(F32)).astype(BF16)
    return hi, mid, lo


def _dot(a, b):
    return jnp.dot(a, b, preferred_element_type=F32)


def _dot_hp(a, b):
    a0, a1, a2 = _split3(a)
    b0, b1, b2 = _split3(b)
    return (_dot(a0, b0) + (_dot(a0, b1) + _dot(a1, b0))
            + (_dot(a0, b2) + _dot(a1, b1) + _dot(a2, b0)))


def _norm_mod(x, w, shift, scale):
    y = x * lax.rsqrt(jnp.mean(x * x, axis=-1, keepdims=True) + EPS)
    return (y * w) * (1.0 + scale) + shift


def _ada_kernel(cond_ref, w_ref, b_ref, o_ref):
    a = _silu(cond_ref[...]).astype(BF16)
    o_ref[0] = _dot(a, w_ref[0].astype(BF16)) + b_ref[0]


def _ada_mod(cond, ada_w, ada_b):
    n_layer, d, n_out = ada_w.shape
    tn = 1024 if n_out % 1024 == 0 else n_out
    return pl.pallas_call(
        _ada_kernel,
        grid=(n_layer, n_out // tn),
        in_specs=[
            pl.BlockSpec((ADA_ROWS, d), lambda l, j: (0, 0)),
            pl.BlockSpec((1, d, tn), lambda l, j: (l, 0, j)),
            pl.BlockSpec((1, 1, tn), lambda l, j: (l, 0, j)),
        ],
        out_specs=pl.BlockSpec((1, ADA_ROWS, tn), lambda l, j: (l, 0, j)),
        out_shape=jax.ShapeDtypeStruct((n_layer, ADA_ROWS, n_out), F32),
        compiler_params=_cparams("parallel", "parallel"),
        name="ada_mod",
    )(cond, ada_w, ada_b.reshape(n_layer, 1, n_out))


def _conv_in_kernel(ctx_ref, x_ref, mod_ref, nw_ref, win_ref, wdw_ref, z_ref, *, d, cn, tiles_per_b):
    is_ctx = (pl.program_id(0) % tiles_per_b) == 0
    mod = mod_ref[0]
    xt = jnp.where(is_ctx, ctx_ref[...], x_ref[...])
    h = _norm_mod(xt, nw_ref[...], mod[:, 0:d], mod[:, d:2 * d]).astype(BF16)
    t = lax.broadcasted_iota(I32, (TILE, 1), 0)
    pos_mask = jnp.where(is_ctx, TILE - 1, GRID_W - 1)
    pos = jnp.bitwise_and(t, pos_mask)
    first = pos == 0
    last = pos == pos_mask
    for j in range(d // cn):
        c0 = j * cn
        bg = _dot(h, win_ref[:, c0:c0 + cn])
        cg = _dot(h, win_ref[:, d + c0:d + c0 + cn])
        hi = _dot(h, win_ref[:, 2 * d + c0:2 * d + c0 + cn])
        u = cg * hi
        u_prev = jnp.where(first, 0.0, pltpu.roll(u, 1, 0))
        u_next = jnp.where(last, 0.0, pltpu.roll(u, TILE - 1, 0))
        w = wdw_ref[:, c0:c0 + cn]
        y = u_prev * w[0:1] + u * w[1:2] + u_next * w[2:3]
        z_ref[:, c0:c0 + cn] = (bg * y).astype(BF16)


def _conv_in(ctx2, x2, mod, nw, w_in, w_dw, *, n_batch, tiles_per_b, ctx_map, lat_map):
    d = x2.shape[1]
    n_tiles = n_batch * tiles_per_b
    n_tok = n_tiles * TILE
    cn = min(512, d)

    def mod_map(i):
        return (jnp.where(i % tiles_per_b == 0, n_batch, i // tiles_per_b), 0, 0)

    return pl.pallas_call(
        functools.partial(_conv_in_kernel, d=d, cn=cn, tiles_per_b=tiles_per_b),
        grid=(n_tiles,),
        in_specs=[
            pl.BlockSpec((TILE, d), lambda i: (ctx_map(i), 0)),
            pl.BlockSpec((TILE, d), lambda i: (lat_map(i), 0)),
            pl.BlockSpec((1, 1, ADA_CHUNKS * d), mod_map),
            pl.BlockSpec((1, d), lambda i: (0, 0)),
            _resident((d, 3 * d), lambda i: (0, 0)),
            pl.BlockSpec((3, d), lambda i: (0, 0)),
        ],
        out_specs=pl.BlockSpec((TILE, d), lambda i: (i, 0)),
        out_shape=jax.ShapeDtypeStruct((n_tok, d), BF16),
        compiler_params=_cparams("parallel"),
        name="conv_in",
    )(ctx2, x2, mod, nw, w_in, w_dw)


def _post_kernel(z_ref, xa_ref, xb_ref, mod_ref, nw_ref, wout_ref, rwcat_ref, rwhi_ref, rb_ref,
                 xn_ref, h2_ref, eidx_ref, gate_ref, rank_ref, cnt_ref, carry_ref, *, d, n_exp, pick_a):
    i = pl.program_id(0)

    @pl.when(i == 0)
    def _():
        carry_ref[...] = jnp.zeros_like(carry_ref)

    mod = mod_ref[0]
    y = _dot(z_ref[...], wout_ref[...])
    xn = jnp.where(pick_a(i), xa_ref[...], xb_ref[...]) + mod[:, 2 * d:3 * d] * y
    xn_ref[...] = xn
    h2 = _norm_mod(xn, nw_ref[...], mod[:, 3 * d:4 * d], mod[:, 4 * d:5 * d])
    h2_ref[...] = h2

    h2_hi = h2.astype(BF16)
    h2_lo = (h2 - h2_hi.astype(F32)).astype(BF16)
    p_hi = _dot(h2_hi, rwcat_ref[...])
    logits = p_hi[:, :n_exp] + (p_hi[:, n_exp:] + _dot(h2_lo, rwhi_ref[...]))
    scores = _sigmoid(logits)
    lane = lax.broadcasted_iota(I32, (TILE, n_exp), 1)
    lane_f = lane.astype(F32)
    work = scores + rb_ref[...]
    onehots, picks = [], []
    for _ in range(TOP_K):
        mx = jnp.max(work, axis=1, keepdims=True)
        first_max = jnp.min(jnp.where(work == mx, lane_f, float(n_exp)), axis=1, keepdims=True)
        oh = lane_f == first_max
        onehots.append(oh)
        picks.append(first_max)
        work = jnp.where(oh, -jnp.inf, work)
    sel = onehots[0]
    for oh in onehots[1:]:
        sel = jnp.logical_or(sel, oh)
    picked = jnp.where(sel, scores, 0.0)
    gates = picked / jnp.sum(picked, axis=1, keepdims=True) * ROUTED_SCALE
    sel_f = jnp.where(sel, 1.0, 0.0)
    r_i = lax.broadcasted_iota(I32, (TILE, TILE), 0)
    c_i = lax.broadcasted_iota(I32, (TILE, TILE), 1)
    before = jnp.where(c_i < r_i, 1.0, 0.0).astype(BF16)
    cum = _dot(before, sel_f.astype(BF16)) + carry_ref[...]

    eidx_ref[...] = jnp.zeros_like(eidx_ref)
    gate_ref[...] = jnp.zeros_like(gate_ref)
    rank_ref[...] = jnp.zeros_like(rank_ref)
    for k, oh in enumerate(onehots):
        eidx_ref[:, k:k + 1] = picks[k].astype(I32)
        gate_ref[:, k:k + 1] = jnp.sum(jnp.where(oh, gates, 0.0), axis=1, keepdims=True)
        rank_ref[:, k:k + 1] = jnp.sum(jnp.where(oh, cum, 0.0), axis=1, keepdims=True).astype(I32)

    total = carry_ref[...] + jnp.sum(sel_f, axis=0, keepdims=True)
    carry_ref[...] = total
    cnt_ref[...] = total


def _post(z, xa, xb, mod, nw, w_out, router_w, router_b, *, a_map, b_map, pick_a, mod_row_map):
    rw_hi = router_w.astype(BF16)
    rw_lo = (router_w - rw_hi.astype(F32)).astype(BF16)
    rw_cat = jnp.concatenate([rw_hi, rw_lo], axis=1)
    n_tok, d = z.shape
    n_tiles = n_tok // TILE
    n_exp = router_w.shape[1]
    outs = pl.pallas_call(
        functools.partial(_post_kernel, d=d, n_exp=n_exp, pick_a=pick_a),
        grid=(n_tiles,),
        in_specs=[
            pl.BlockSpec((TILE, d), lambda i: (i, 0)),
            pl.BlockSpec((TILE, d), lambda i: (a_map(i), 0)),
            pl.BlockSpec((TILE, d), lambda i: (b_map(i), 0)),
            pl.BlockSpec((1, 1, ADA_CHUNKS * d), lambda i: (mod_row_map(i), 0, 0)),
            pl.BlockSpec((1, d), lambda i: (0, 0)),
            _resident((d, d), lambda i: (0, 0)),
            pl.BlockSpec((d, 2 * n_exp), lambda i: (0, 0)),
            pl.BlockSpec((d, n_exp), lambda i: (0, 0)),
            pl.BlockSpec((1, n_exp), lambda i: (0, 0)),
        ],
        out_specs=[
            pl.BlockSpec((TILE, d), lambda i: (i, 0)),
            pl.BlockSpec((TILE, d), lambda i: (i, 0)),
            pl.BlockSpec((TILE, IDX_W), lambda i: (i, 0)),
            pl.BlockSpec((TILE, IDX_W), lambda i: (i, 0)),
            pl.BlockSpec((TILE, IDX_W), lambda i: (i, 0)),
            pl.BlockSpec((1, n_exp), lambda i: (0, 0)),
        ],
        out_shape=[
            jax.ShapeDtypeStruct((n_tok, d), F32),
            jax.ShapeDtypeStruct((n_tok, d), F32),
            jax.ShapeDtypeStruct((n_tok, IDX_W), I32),
            jax.ShapeDtypeStruct((n_tok, IDX_W), F32),
            jax.ShapeDtypeStruct((n_tok, IDX_W), I32),
            jax.ShapeDtypeStruct((1, n_exp), F32),
        ],
        scratch_shapes=[pltpu.VMEM((1, n_exp), F32)],
        compiler_params=_cparams("arbitrary"),
        name="post_mixer",
    )(z, xa, xb, mod, nw, w_out, rw_cat, rw_hi, router_b)
    return outs


def _dispatch_kernel(pstart_ref, cnt_ref, slot_ref, h2_ref, xs_ref, zbuf, sem, zsem, *, n_exp):
    def pad_fill(e, wait):
        rem = cnt_ref[e] % MOE_BLK
        pad = jnp.where(rem == 0, 0, MOE_BLK - rem)
        base = pstart_ref[e] + cnt_ref[e]
        head = jnp.minimum(pad, jnp.bitwise_and(-base, SUBLANES - 1))

        def fill(off, size, cond):
            copy = pltpu.make_async_copy(zbuf.at[pl.ds(0, size)], xs_ref.at[pl.ds(off, size)], zsem)

            @pl.when(cond)
            def _():
                copy.wait() if wait else copy.start()

        for r in range(SUBLANES - 1):
            fill(base + r, 1, r < head)
        off = base + head
        rest = pad - head
        for bit in reversed(range(SUBLANES.bit_length() - 1, MOE_BLK.bit_length() - 1)):
            size = 1 << bit
            take = (rest >> bit) & 1
            fill(pl.multiple_of(off, SUBLANES), size, take == 1)
            off = off + take * size

    @pl.when(pl.program_id(0) == 0)
    def _():
        zbuf[...] = jnp.zeros_like(zbuf)

        def fill(e, c):
            pad_fill(e, False)
            return c

        def fill_wait(e, c):
            pad_fill(e, True)
            return c

        lax.fori_loop(0, n_exp, fill, 0)
        lax.fori_loop(0, n_exp, fill_wait, 0)

    def row_copy(t8, r, k):
        s = slot_ref[0, 0, (t8 * SUBLANES + r) * TOP_K + k]
        return pltpu.make_async_copy(h2_ref.at[t8, pl.ds(r, 1)], xs_ref.at[pl.ds(s, 1)], sem)

    def issue(t8, c):
        for r in range(SUBLANES):
            for k in range(TOP_K):
                row_copy(t8, r, k).start(priority=k % 2)
        return c

    lax.fori_loop(0, TILE // SUBLANES, issue, 0)
    for k in range(TOP_K):
        pltpu.make_async_copy(xs_ref.at[pl.ds(0, TILE)], xs_ref.at[pl.ds(0, TILE)], sem).wait()


def _dispatch(pstart, cnt, slots, h2, n_slots):
    n_tok, d = h2.shape
    n_tiles = n_tok // TILE
    grid_spec = pltpu.PrefetchScalarGridSpec(
        num_scalar_prefetch=2,
        grid=(n_tiles,),
        in_specs=[
            pl.BlockSpec((1, 1, TILE * TOP_K), lambda i, ps, ct: (i, 0, 0), memory_space=pltpu.SMEM),
            pl.BlockSpec((TILE // SUBLANES, SUBLANES, d), lambda i, ps, ct: (i, 0, 0)),
        ],
        out_specs=pl.BlockSpec(memory_space=pl.ANY),
        scratch_shapes=[pltpu.VMEM((MOE_BLK // 2, d), F32), pltpu.SemaphoreType.DMA(()),
                        pltpu.SemaphoreType.DMA(())],
    )
    return pl.pallas_call(
        functools.partial(_dispatch_kernel, n_exp=pstart.shape[0]),
        grid_spec=grid_spec,
        out_shape=jax.ShapeDtypeStruct((n_slots, d), F32),
        compiler_params=_cparams("arbitrary"),
        name="moe_dispatch",
    )(pstart, cnt, slots.reshape(n_tiles, 1, TILE * TOP_K), h2.reshape(n_tok // SUBLANES, SUBLANES, d))


def _grouped_kernel(be_ref, meta_ref, xs_ref, wg_ref, wu_ref, wd_ref, ys_ref, wg_s, wu_s, wd_s):
    i = pl.program_id(0)
    used = i < meta_ref[0]
    new_expert = jnp.logical_or(i == 0, be_ref[i] != be_ref[jnp.maximum(i - 1, 0)])

    @pl.when(jnp.logical_and(used, new_expert))
    def _():
        wg_s[...] = wg_ref[0, 0].astype(BF16)
        wu_s[...] = wu_ref[0, 0].astype(BF16)
        wd_s[...] = wd_ref[0, 0].astype(BF16)

    @pl.when(used)
    def _():
        xb = xs_ref[...].astype(BF16)
        a = _silu(_dot(xb, wg_s[...])) * _dot(xb, wu_s[...])
        ys_ref[...] = _dot(a.astype(BF16), wd_s[...])

    @pl.when(jnp.logical_not(used))
    def _():
        ys_ref[...] = jnp.zeros_like(ys_ref)


def _grouped(blk_expert, n_used, xs, w_gate, w_up, w_down, layer):
    n_slots, d = xs.shape
    n_blk = n_slots // MOE_BLK
    f = w_gate.shape[3]

    def row_map(i, be, meta):
        return (jnp.minimum(i, jnp.maximum(meta[0] - 1, 0)), 0)

    grid_spec = pltpu.PrefetchScalarGridSpec(
        num_scalar_prefetch=2,
        grid=(n_blk,),
        in_specs=[
            pl.BlockSpec((MOE_BLK, d), row_map),
            pl.BlockSpec((1, 1, d, f), lambda i, be, meta: (layer, be[i], 0, 0)),
            pl.BlockSpec((1, 1, d, f), lambda i, be, meta: (layer, be[i], 0, 0)),
            pl.BlockSpec((1, 1, f, d), lambda i, be, meta: (layer, be[i], 0, 0)),
        ],
        out_specs=pl.BlockSpec((MOE_BLK, d), lambda i, be, meta: (i, 0)),
        scratch_shapes=[pltpu.VMEM((d, f), BF16), pltpu.VMEM((d, f), BF16), pltpu.VMEM((f, d), BF16)],
    )
    return pl.pallas_call(
        _grouped_kernel,
        grid_spec=grid_spec,
        out_shape=jax.ShapeDtypeStruct((n_slots, d), F32),
        compiler_params=_cparams("arbitrary"),
        name="moe_experts",
    )(blk_expert, n_used, xs, w_gate, w_up, w_down)


def _combine_kernel(slot_ref, gate_ref, h2_ref, xn_ref, mod_ref, sg_ref, su_ref, sd_ref, fw_ref, ys_ref,
                    o_ref, gbuf, sem, *, d, final):
    def row_copy(t8, r, k):
        s = slot_ref[0, 0, (t8 * SUBLANES + r) * TOP_K + k]
        return pltpu.make_async_copy(ys_ref.at[pl.ds(s, 1)], gbuf.at[k, t8, pl.ds(r, 1)], sem)

    def issue(t8, c):
        for r in range(SUBLANES):
            for k in range(TOP_K):
                row_copy(t8, r, k).start(priority=k % 2)
        return c

    lax.fori_loop(0, TILE // SUBLANES, issue, 0)
    hb = h2_ref[...].astype(BF16)
    a = _silu(_dot(hb, sg_ref[...])) * _dot(hb, su_ref[...])
    acc = _dot(a.astype(BF16), sd_ref[...])
    for k in range(TOP_K):
        pltpu.make_async_copy(ys_ref.at[pl.ds(0, TILE)], ys_ref.at[pl.ds(0, TILE)], sem).wait()
    gate = gate_ref[...]
    for k in range(TOP_K):
        acc = acc + gate[:, k:k + 1] * gbuf[k].reshape(TILE, d)
    x2 = xn_ref[...] + mod_ref[0][:, 5 * d:6 * d] * acc
    if final:
        x2 = x2 * lax.rsqrt(jnp.mean(x2 * x2, axis=-1, keepdims=True) + EPS) * fw_ref[...]
    o_ref[...] = x2


def _combine(slots, gates, h2, xn, mod, sh_gate, sh_up, sh_down, fw, ys, *, mod_row_map, final):
    n_tok, d = h2.shape
    n_tiles = n_tok // TILE
    f = sh_gate.shape[1]
    return pl.pallas_call(
        functools.partial(_combine_kernel, d=d, final=final),
        grid=(n_tiles,),
        in_specs=[
            pl.BlockSpec((1, 1, TILE * TOP_K), lambda i: (i, 0, 0), memory_space=pltpu.SMEM),
            pl.BlockSpec((TILE, IDX_W), lambda i: (i, 0)),
            pl.BlockSpec((TILE, d), lambda i: (i, 0)),
            pl.BlockSpec((TILE, d), lambda i: (i, 0)),
            pl.BlockSpec((1, 1, ADA_CHUNKS * d), lambda i: (mod_row_map(i), 0, 0)),
            pl.BlockSpec((d, f), lambda i: (0, 0)),
            pl.BlockSpec((d, f), lambda i: (0, 0)),
            pl.BlockSpec((f, d), lambda i: (0, 0)),
            pl.BlockSpec((1, d), lambda i: (0, 0)),
            pl.BlockSpec(memory_space=pl.ANY),
        ],
        out_specs=pl.BlockSpec((TILE, d), lambda i: (i, 0)),
        out_shape=jax.ShapeDtypeStruct((n_tok, d), F32),
        scratch_shapes=[pltpu.VMEM((TOP_K, TILE // SUBLANES, SUBLANES, d), F32), pltpu.SemaphoreType.DMA(())],
        compiler_params=_cparams("arbitrary"),
        name="moe_combine",
    )(slots.reshape(n_tiles, 1, TILE * TOP_K), gates, h2, xn, mod, sh_gate, sh_up, sh_down, fw, ys)


def _moe(h2, xn, eidx, gates, rank, counts, mod, w_gate, w_up, w_down, sh_gate, sh_up, sh_down, fw,
         *, layer, mod_row_map, final):
    n_tok = h2.shape[0]
    n_exp = w_gate.shape[1]
    n_blk = (n_tok * TOP_K + n_exp * (MOE_BLK - 1) + MOE_BLK - 1) // MOE_BLK
    cnt = counts[0].astype(I32)
    padded = (cnt + MOE_BLK - 1) // MOE_BLK * MOE_BLK
    pend = jnp.cumsum(padded)
    pstart = pend - padded
    expert_ids = jnp.arange(n_exp, dtype=I32)
    slots = jnp.sum(jnp.where(eidx[:, :TOP_K, None] == expert_ids, pstart, 0), axis=-1) + rank[:, :TOP_K]
    n_used = pend[-1] // MOE_BLK
    blk = jnp.arange(n_blk, dtype=I32)
    be = jnp.sum((pend[None, :] <= blk[:, None] * MOE_BLK).astype(I32), axis=1)
    last_used = jnp.sum(jnp.where(blk == n_used - 1, be, 0))
    be = jnp.minimum(jnp.where(blk < n_used, be, last_used), n_exp - 1)
    xs = _dispatch(pstart, cnt, slots, h2, n_blk * MOE_BLK)
    ys = _grouped(be, n_used.reshape(1).astype(I32), xs, w_gate, w_up, w_down, layer)
    return _combine(slots, gates, h2, xn, mod, sh_gate, sh_up, sh_down, fw, ys,
                    mod_row_map=mod_row_map, final=final)


def _log_sigmoid(x):
    return jnp.minimum(x, 0.0) - jnp.log1p(jnp.exp(-jnp.abs(x)))


def _mlstm_in_kernel(x_ref, mod_ref, nw_ref, w_ref, wg_ref, gb_ref, p_ref, g_ref, *, d, cn, qk_w, n_head, k_scale):
    mod = mod_ref[0]
    hf = _norm_mod(x_ref[...], nw_ref[...], mod[:, 0:d], mod[:, d:2 * d])
    h = hf.astype(BF16)
    for j in range(3 * d // cn):
        c0 = j * cn
        p = _dot(h, w_ref[:, c0:c0 + cn])
        if qk_w <= c0 < 2 * qk_w:
            p = p * k_scale
        p_ref[:, c0:c0 + cn] = p.astype(BF16)
    g = _dot_hp(hf, wg_ref[...]) + gb_ref[...]
    g = GATE_CAP * jnp.tanh(g / GATE_CAP)
    col = lax.broadcasted_iota(I32, g.shape, 1)
    is_forget = (col // n_head) % 2 == 1
    g_ref[...] = jnp.where(is_forget, _log_sigmoid(g), g)


def _mlstm_in(xt, mod, nw, w_qkvo, w_g, gate_b, *, n_batch, tiles_per_b, n_head):
    n_tok, d = xt.shape
    n_tiles = n_tok // TILE
    qk_w = d // 2
    cn = min(512, qk_w)
    n_g = w_g.shape[1]
    k_scale = float((qk_w // n_head) ** -0.5)

    def mod_map(i):
        return (jnp.where(i % tiles_per_b == 0, n_batch, i // tiles_per_b), 0, 0)

    return pl.pallas_call(
        functools.partial(_mlstm_in_kernel, d=d, cn=cn, qk_w=qk_w, n_head=n_head, k_scale=k_scale),
        grid=(n_tiles,),
        in_specs=[
            pl.BlockSpec((TILE, d), lambda i: (i, 0)),
            pl.BlockSpec((1, 1, ADA_CHUNKS * d), mod_map),
            pl.BlockSpec((1, d), lambda i: (0, 0)),
            _resident((d, 3 * d), lambda i: (0, 0)),
            pl.BlockSpec((d, n_g), lambda i: (0, 0)),
            pl.BlockSpec((1, n_g), lambda i: (0, 0)),
        ],
        out_specs=[
            pl.BlockSpec((TILE, 3 * d), lambda i: (i, 0)),
            pl.BlockSpec((TILE, n_g), lambda i: (i, 0)),
        ],
        out_shape=[
            jax.ShapeDtypeStruct((n_tok, 3 * d), BF16),
            jax.ShapeDtypeStruct((n_tok, n_g), F32),
        ],
        compiler_params=_cparams("parallel"),
        name="mlstm_in",
    )(xt, mod, nw, w_qkvo, w_g, gate_b)


def _mlstm_chunk(q, k, v, li_r, lf_r, c_st, n_st, m_st, *, backward):
    n_t = q.shape[0]
    tt = lax.broadcasted_iota(I32, (n_t, n_t), 0)
    ss = lax.broadcasted_iota(I32, (n_t, n_t), 1)
    diag = tt == ss
    if backward:
        seen = ss >= tt
        seen_t = tt >= ss
    else:
        seen = ss <= tt
        seen_t = tt <= ss
    lf_b = jnp.broadcast_to(lf_r, (n_t, n_t))
    li_b = jnp.broadcast_to(li_r, (n_t, n_t))
    b_c = jnp.sum(jnp.where(seen, lf_b, 0.0), axis=1, keepdims=True)
    lf_c = jnp.sum(jnp.where(diag, lf_b, 0.0), axis=1, keepdims=True)
    li_c = jnp.sum(jnp.where(diag, li_b, 0.0), axis=1, keepdims=True)
    b_r = jnp.sum(jnp.where(seen_t, lf_c, 0.0), axis=0, keepdims=True)
    b_last = jnp.sum(lf_r, axis=1, keepdims=True)
    dmat = jnp.where(seen, b_c - b_r + li_b, -jnp.inf)
    a = b_c + m_st
    m_row = jnp.maximum(a, jnp.max(dmat, axis=1, keepdims=True))
    w_intra = jnp.exp(dmat - m_row)
    w_inter = jnp.exp(a - m_row)
    s = lax.dot_general(q, k, (((1,), (1,)), ((), ())), preferred_element_type=F32) * w_intra
    inter = _dot(q, c_st.astype(BF16))
    num = _dot(s.astype(BF16), v) + w_inter * inter
    qn = jnp.sum(q.astype(F32) * n_st, axis=1, keepdims=True)
    den = jnp.sum(s, axis=1, keepdims=True) + w_inter * qn
    h = num / jnp.maximum(jnp.abs(den), jnp.exp(-m_row))
    g = b_last - b_c + li_c
    m_new = jnp.maximum(b_last + m_st, jnp.max(g, axis=0, keepdims=True))
    decay = jnp.exp(b_last + m_st - m_new)
    kw = k.astype(F32) * jnp.exp(g - m_new)
    c_new = decay * c_st + lax.dot_general(kw.astype(BF16), v, (((0,), (0,)), ((), ())),
                                           preferred_element_type=F32)
    n_new = decay * n_st + jnp.sum(kw, axis=0, keepdims=True)
    return h, c_new, n_new, m_new


def _mlstm_scan_kernel(q_ref, k_ref, v_ref, o_ref, gr_ref, nw_ref, z_ref, hf_ref, hb_ref, c_ref,
                       *, n_chunk, n_ctx_chunk, dqk, dv, n_hp):
    c_ref[...] = jnp.zeros_like(c_ref)
    n0 = jnp.zeros((1, dqk), F32)
    m0 = jnp.full((1, 1), M_INIT, F32)

    def step(i, carry):
        jf = i
        jb = jnp.where(i < n_ctx_chunk, n_ctx_chunk - 1 - i, n_chunk - 1 - (i - n_ctx_chunk))
        rows = (pl.ds(pl.multiple_of(jf * CHUNK, CHUNK), CHUNK), pl.ds(pl.multiple_of(jb * CHUNK, CHUNK), CHUNK))
        chunk = (jf, jb)
        scans = [(hp, direction) for hp in range(n_hp) for direction in range(2)]
        loaded = []
        for hp, direction in scans:
            r, j = rows[direction], chunk[direction]
            qc = slice(hp * dqk, (hp + 1) * dqk)
            vc = slice(hp * dv, (hp + 1) * dv)
            loaded.append((q_ref[r, qc], k_ref[r, qc], v_ref[r, vc],
                           gr_ref[0, 2 * direction, hp, j], gr_ref[0, 2 * direction + 1, hp, j],
                           c_ref[2 * hp + direction]))
        results = [_mlstm_chunk(*loaded[s], *carry[s], backward=direction == 1)
                   for s, (hp, direction) in enumerate(scans)]
        for s, (hp, direction) in enumerate(scans):
            h, c_new, _, _ = results[s]
            c_ref[s] = c_new
            (hf_ref, hb_ref)[direction][rows[direction], hp * dv:(hp + 1) * dv] = h
        return tuple((n_new, m_new) for _, _, n_new, m_new in results)

    lax.fori_loop(0, n_chunk, step, ((n0, m0),) * (2 * n_hp))
    n_ctx = n_ctx_chunk * CHUNK
    n_lat = (n_chunk - n_ctx_chunk) * CHUNK
    lat = pl.ds(n_ctx, n_lat)
    for hp in range(n_hp):
        vc = slice(hp * dv, (hp + 1) * dv)
        h = hf_ref[lat, vc] + hb_ref[lat, vc]
        hn = h * lax.rsqrt(jnp.mean(h * h, axis=-1, keepdims=True) + EPS)
        y = hn * nw_ref[:, vc] * _sigmoid(o_ref[lat, vc].astype(F32))
        z_ref[:, vc] = y.astype(BF16)


def _mlstm_scan(p, gr, norm_w, *, n_batch, n_head, seq_all, n_ctx, d):
    dqk = d // (2 * n_head)
    dv = d // n_head
    n_hp = 2 if n_head % 2 == 0 else 1
    n_chunk = seq_all // CHUNK
    n_lat = seq_all - n_ctx
    qk_blocks = (d // 2) // (n_hp * dqk)
    v_blocks = d // (n_hp * dv)
    return pl.pallas_call(
        functools.partial(_mlstm_scan_kernel, n_chunk=n_chunk, n_ctx_chunk=n_ctx // CHUNK, dqk=dqk, dv=dv,
                          n_hp=n_hp),
        grid=(n_batch, n_head // n_hp),
        in_specs=[
            pl.BlockSpec((seq_all, n_hp * dqk), lambda b, h: (b, h)),
            pl.BlockSpec((seq_all, n_hp * dqk), lambda b, h: (b, qk_blocks + h)),
            pl.BlockSpec((seq_all, n_hp * dv), lambda b, h: (b, v_blocks + h)),
            pl.BlockSpec((seq_all, n_hp * dv), lambda b, h: (b, 2 * v_blocks + h)),
            pl.BlockSpec((1, 4, n_hp, n_chunk, 1, CHUNK), lambda b, h: (b, 0, h, 0, 0, 0)),
            pl.BlockSpec((1, n_hp * dv), lambda b, h: (0, h)),
        ],
        out_specs=pl.BlockSpec((n_lat, n_hp * dv), lambda b, h: (b, h)),
        out_shape=jax.ShapeDtypeStruct((n_batch * n_lat, d), BF16),
        scratch_shapes=[
            pltpu.VMEM((seq_all, n_hp * dv), F32),
            pltpu.VMEM((seq_all, n_hp * dv), F32),
            pltpu.VMEM((2 * n_hp, dqk, dv), F32),
        ],
        compiler_params=_cparams("parallel", "parallel"),
        name="mlstm_scan",
    )(p, p, p, p, gr, norm_w)


def kernel(x, c, ctx, c_ctx, ada_w, ada_b, norm_mix_w, norm_ffn_w, conv_in_w, conv_dw_w, conv_out_w,
           mlstm_in_w, mlstm_gate_b, mlstm_norm_w, mlstm_out_w, router_w, router_bias,
           exp_gate_w, exp_up_w, exp_down_w, shared_gate_w, shared_up_w, shared_down_w, final_norm_w):
    n_batch, seq, d = x.shape
    n_ctx = ctx.shape[1]
    assert ada_w.shape[0] == 2 and n_ctx == TILE and seq % TILE == 0 and n_batch + 1 <= ADA_ROWS
    seq_all = n_ctx + seq
    tiles_per_b = seq_all // TILE
    lat_tiles_per_b = seq // TILE
    n_head = (mlstm_in_w.shape[2] - 3 * d) // 4

    cond = jnp.zeros((ADA_ROWS, d), F32).at[:n_batch].set(c).at[n_batch].set(c_ctx)
    mod = _ada_mod(cond, ada_w, ada_b)
    mod0 = mod[0].reshape(ADA_ROWS, 1, ADA_CHUNKS * d)
    mod1 = mod[1].reshape(ADA_ROWS, 1, ADA_CHUNKS * d)

    def all_mod_row(i):
        return jnp.where(i % tiles_per_b == 0, n_batch, i // tiles_per_b)

    def lat_mod_row(i):
        return i // lat_tiles_per_b

    def lat_tile(i):
        return (i // lat_tiles_per_b) * tiles_per_b + 1 + i % lat_tiles_per_b

    row = lambda w: w.reshape(1, -1)
    bf = lambda w: w.astype(BF16)
    ctx2 = ctx.reshape(n_batch * n_ctx, d)
    x2 = x.reshape(n_batch * seq, d)

    def is_ctx_tile(i):
        return i % tiles_per_b == 0

    def ctx_tile(i):
        return i // tiles_per_b

    def x_tile(i):
        return (i // tiles_per_b) * lat_tiles_per_b + jnp.maximum(i % tiles_per_b - 1, 0)

    z0 = _conv_in(ctx2, x2, mod0, row(norm_mix_w[0]), bf(conv_in_w[0]), conv_dw_w[0],
                  n_batch=n_batch, tiles_per_b=tiles_per_b, ctx_map=ctx_tile, lat_map=x_tile)
    xn0, h20, eidx0, gate0, rank0, cnt0 = _post(
        z0, ctx2, x2, mod0, row(norm_ffn_w[0]), bf(conv_out_w[0]), router_w[0], row(router_bias[0]),
        a_map=ctx_tile, b_map=x_tile, pick_a=is_ctx_tile, mod_row_map=all_mod_row)
    x1 = _moe(h20, xn0, eidx0, gate0, rank0, cnt0, mod0, exp_gate_w, exp_up_w, exp_down_w,
              bf(shared_gate_w[0]), bf(shared_up_w[0]), bf(shared_down_w[0]), row(final_norm_w),
              layer=0, mod_row_map=all_mod_row, final=False)

    w_in = mlstm_in_w[0]
    p, g = _mlstm_in(x1, mod1, row(norm_mix_w[1]), bf(w_in[:, :3 * d]), w_in[:, 3 * d:], row(mlstm_gate_b[0]),
                     n_batch=n_batch, tiles_per_b=tiles_per_b, n_head=n_head)
    n_chunk = seq_all // CHUNK
    gr = g.reshape(n_batch, n_chunk, CHUNK, 4, n_head).transpose(0, 3, 4, 1, 2)
    gr = gr.reshape(n_batch, 4, n_head, n_chunk, 1, CHUNK)
    z1 = _mlstm_scan(p, gr, row(mlstm_norm_w[0]), n_batch=n_batch, n_head=n_head, seq_all=seq_all,
                     n_ctx=n_ctx, d=d)
    xn1, h21, eidx1, gate1, rank1, cnt1 = _post(
        z1, x1, x1, mod1, row(norm_ffn_w[1]), bf(mlstm_out_w[0]), router_w[1], row(router_bias[1]),
        a_map=lat_tile, b_map=lambda i: 0, pick_a=lambda i: i >= 0, mod_row_map=lat_mod_row)
    out = _moe(h21, xn1, eidx1, gate1, rank1, cnt1, mod1, exp_gate_w, exp_up_w, exp_down_w,
               bf(shared_gate_w[1]), bf(shared_up_w[1]), bf(shared_down_w[1]), row(final_norm_w),
               layer=1, mod_row_map=lat_mod_row, final=True)
    return out.reshape(n_batch, seq, d)
```

```python
import functools

import jax
import jax.numpy as jnp
import numpy as np
from jax import lax
from jax.experimental import pallas as pl
from jax.experimental.pallas import tpu as pltpu

F32 = jnp.float32
BF16 = jnp.bfloat16
I32 = jnp.int32
U32 = jnp.uint32
HI_HALF = np.uint32(0xFFFF0000)
HALF_BITS = np.uint32(16)

TILE = 256
GRID_W = 64
CHUNK = 64
TOP_K = 6
MOE_BLK = 256
IDX_W = 8
STAT_W = 8
LANES = 128
SUBLANES = 8
ADA_CHUNKS = 6
ADA_ROWS = 16
EPS = 1e-6
GATE_CAP = 15.0
M_INIT = -1e30
ROUTED_SCALE = 2.5
V7X_VMEM_LIMIT = 56 * 1024 * 1024


def _cparams(*sem):
    return pltpu.CompilerParams(dimension_semantics=sem, vmem_limit_bytes=V7X_VMEM_LIMIT)


def _resident(shape, index_map):
    return pl.BlockSpec(shape, index_map, pipeline_mode=pl.Buffered(1))


def _sigmoid(x):
    return 1.0 / (1.0 + jnp.exp(-x))


def _silu(x):
    return x * _sigmoid(x)


def _split3(a):
    hi = a.astype(BF16)
    r1 = a - hi.astype(F32)
    mid = r1.astype(BF16)
    lo = (r1 - mid.astype(F32)).astype(BF16)
    return hi, mid, lo


def _dot(a, b):
    return jnp.dot(a, b, preferred_element_type=F32)


def _pack_bf16_pairs(x):
    half = x.shape[1] // 2
    lo = pltpu.bitcast(x[:, :half].astype(BF16).astype(F32), U32)
    hi = pltpu.bitcast(x[:, half:].astype(BF16).astype(F32), U32)
    return jnp.bitwise_or(jnp.right_shift(lo, HALF_BITS), jnp.bitwise_and(hi, HI_HALF))


def _unpack_bf16_pairs(u):
    lo = pltpu.bitcast(jnp.left_shift(u, HALF_BITS), F32).astype(BF16)
    hi = pltpu.bitcast(jnp.bitwise_and(u, HI_HALF), F32).astype(BF16)
    return jnp.concatenate([lo, hi], axis=1)


def _norm_mod(x, w, shift, scale):
    y = x * lax.rsqrt(jnp.mean(x * x, axis=-1, keepdims=True) + EPS)
    return (y * w) * (1.0 + scale) + shift


def _ada_kernel(cond_ref, w_ref, b_ref, o_ref):
    a = _silu(cond_ref[...]).astype(BF16)
    o_ref[0] = _dot(a, w_ref[0].astype(BF16)) + b_ref[0]


def _ada_mod(cond, ada_w, ada_b):
    n_layer, d, n_out = ada_w.shape
    tn = 1024 if n_out % 1024 == 0 else n_out
    return pl.pallas_call(
        _ada_kernel,
        grid=(n_layer, n_out // tn),
        in_specs=[
            pl.BlockSpec((ADA_ROWS, d), lambda l, j: (0, 0)),
            pl.BlockSpec((1, d, tn), lambda l, j: (l, 0, j)),
            pl.BlockSpec((1, 1, tn), lambda l, j: (l, 0, j)),
        ],
        out_specs=pl.BlockSpec((1, ADA_ROWS, tn), lambda l, j: (l, 0, j)),
        out_shape=jax.ShapeDtypeStruct((n_layer, ADA_ROWS, n_out), F32),
        compiler_params=_cparams("parallel", "parallel"),
        name="ada_mod",
    )(cond, ada_w, ada_b.reshape(n_layer, 1, n_out))


def _conv_in_kernel(ctx_ref, x_ref, mod_ref, nw_ref, win_ref, wdw_ref, z_ref, *, d, cn, tiles_per_b):
    is_ctx = (pl.program_id(0) % tiles_per_b) == 0
    mod = mod_ref[0]
    xt = jnp.where(is_ctx, ctx_ref[...], x_ref[...])
    h = _norm_mod(xt, nw_ref[...], mod[:, 0:d], mod[:, d:2 * d]).astype(BF16)
    t = lax.broadcasted_iota(I32, (TILE, 1), 0)
    pos_mask = jnp.where(is_ctx, TILE - 1, GRID_W - 1)
    pos = jnp.bitwise_and(t, pos_mask)
    first = pos == 0
    last = pos == pos_mask
    for j in range(d // cn):
        c0 = j * cn
        bg = _dot(h, win_ref[:, c0:c0 + cn])
        cg = _dot(h, win_ref[:, d + c0:d + c0 + cn])
        hi = _dot(h, win_ref[:, 2 * d + c0:2 * d + c0 + cn])
        u = cg * hi
        u_prev = jnp.where(first, 0.0, pltpu.roll(u, 1, 0))
        u_next = jnp.where(last, 0.0, pltpu.roll(u, TILE - 1, 0))
        w = wdw_ref[:, c0:c0 + cn]
        y = u_prev * w[0:1] + u * w[1:2] + u_next * w[2:3]
        z_ref[:, c0:c0 + cn] = (bg * y).astype(BF16)


def _conv_in(ctx2, x2, mod, nw, w_in, w_dw, *, n_batch, tiles_per_b, ctx_map, lat_map):
    d = x2.shape[1]
    n_tiles = n_batch * tiles_per_b
    n_tok = n_tiles * TILE
    cn = min(512, d)

    def mod_map(i):
        return (jnp.where(i % tiles_per_b == 0, n_batch, i // tiles_per_b), 0, 0)

    return pl.pallas_call(
        functools.partial(_conv_in_kernel, d=d, cn=cn, tiles_per_b=tiles_per_b),
        grid=(n_tiles,),
        in_specs=[
            pl.BlockSpec((TILE, d), lambda i: (ctx_map(i), 0)),
            pl.BlockSpec((TILE, d), lambda i: (lat_map(i), 0)),
            pl.BlockSpec((1, 1, ADA_CHUNKS * d), mod_map),
            pl.BlockSpec((1, d), lambda i: (0, 0)),
            _resident((d, 3 * d), lambda i: (0, 0)),
            pl.BlockSpec((3, d), lambda i: (0, 0)),
        ],
        out_specs=pl.BlockSpec((TILE, d), lambda i: (i, 0)),
        out_shape=jax.ShapeDtypeStruct((n_tok, d), BF16),
        compiler_params=_cparams("parallel"),
        name="conv_in",
    )(ctx2, x2, mod, nw, w_in, w_dw)


def _post_kernel(z_ref, xa_ref, xb_ref, mod_ref, nw_ref, wout_ref, rwcat_ref, rwhi_ref, rb_ref,
                 xn_ref, h2_ref, eidx_ref, gate_ref, rank_ref, cnt_ref, carry_ref, *, d, n_exp, pick_a):
    i = pl.program_id(0)

    @pl.when(i == 0)
    def _():
        carry_ref[...] = jnp.zeros_like(carry_ref)

    mod = mod_ref[0]
    y = _dot(z_ref[...], wout_ref[...])
    xn = jnp.where(pick_a(i), xa_ref[...], xb_ref[...]) + mod[:, 2 * d:3 * d] * y
    xn_ref[...] = xn
    h2 = _norm_mod(xn, nw_ref[...], mod[:, 3 * d:4 * d], mod[:, 4 * d:5 * d])
    h2_ref[...] = _pack_bf16_pairs(h2)

    h2_hi = h2.astype(BF16)
    h2_lo = (h2 - h2_hi.astype(F32)).astype(BF16)
    p_hi = _dot(h2_hi, rwcat_ref[...])
    logits = p_hi[:, :n_exp] + (p_hi[:, n_exp:] + _dot(h2_lo, rwhi_ref[...]))
    scores = _sigmoid(logits)
    lane = lax.broadcasted_iota(I32, (TILE, n_exp), 1)
    lane_f = lane.astype(F32)
    work = scores + rb_ref[...]
    onehots, picks = [], []
    for _ in range(TOP_K):
        mx = jnp.max(work, axis=1, keepdims=True)
        first_max = jnp.min(jnp.where(work == mx, lane_f, float(n_exp)), axis=1, keepdims=True)
        oh = lane_f == first_max
        onehots.append(oh)
        picks.append(first_max)
        work = jnp.where(oh, -jnp.inf, work)
    sel = onehots[0]
    for oh in onehots[1:]:
        sel = jnp.logical_or(sel, oh)
    picked = jnp.where(sel, scores, 0.0)
    gates = picked / jnp.sum(picked, axis=1, keepdims=True) * ROUTED_SCALE
    sel_f = jnp.where(sel, 1.0, 0.0)
    r_i = lax.broadcasted_iota(I32, (TILE, TILE), 0)
    c_i = lax.broadcasted_iota(I32, (TILE, TILE), 1)
    before = jnp.where(c_i < r_i, 1.0, 0.0).astype(BF16)
    cum = _dot(before, sel_f.astype(BF16)) + carry_ref[...]

    eidx_ref[...] = jnp.zeros_like(eidx_ref)
    gate_ref[...] = jnp.zeros_like(gate_ref)
    rank_ref[...] = jnp.zeros_like(rank_ref)
    for k, oh in enumerate(onehots):
        eidx_ref[:, k:k + 1] = picks[k].astype(I32)
        gate_ref[:, k:k + 1] = jnp.sum(jnp.where(oh, gates, 0.0), axis=1, keepdims=True)
        rank_ref[:, k:k + 1] = jnp.sum(jnp.where(oh, cum, 0.0), axis=1, keepdims=True).astype(I32)

    total = carry_ref[...] + jnp.sum(sel_f, axis=0, keepdims=True)
    carry_ref[...] = total
    cnt_ref[...] = total


def _post(z, xa, xb, mod, nw, w_out, router_w, router_b, *, a_map, b_map, pick_a, mod_row_map):
    rw_hi = router_w.astype(BF16)
    rw_lo = (router_w - rw_hi.astype(F32)).astype(BF16)
    rw_cat = jnp.concatenate([rw_hi, rw_lo], axis=1)
    n_tok, d = z.shape
    n_tiles = n_tok // TILE
    n_exp = router_w.shape[1]
    outs = pl.pallas_call(
        functools.partial(_post_kernel, d=d, n_exp=n_exp, pick_a=pick_a),
        grid=(n_tiles,),
        in_specs=[
            pl.BlockSpec((TILE, d), lambda i: (i, 0)),
            pl.BlockSpec((TILE, d), lambda i: (a_map(i), 0)),
            pl.BlockSpec((TILE, d), lambda i: (b_map(i), 0)),
            pl.BlockSpec((1, 1, ADA_CHUNKS * d), lambda i: (mod_row_map(i), 0, 0)),
            pl.BlockSpec((1, d), lambda i: (0, 0)),
            _resident((d, d), lambda i: (0, 0)),
            pl.BlockSpec((d, 2 * n_exp), lambda i: (0, 0)),
            pl.BlockSpec((d, n_exp), lambda i: (0, 0)),
            pl.BlockSpec((1, n_exp), lambda i: (0, 0)),
        ],
        out_specs=[
            pl.BlockSpec((TILE, d), lambda i: (i, 0)),
            pl.BlockSpec((TILE, d // 2), lambda i: (i, 0)),
            pl.BlockSpec((TILE, IDX_W), lambda i: (i, 0)),
            pl.BlockSpec((TILE, IDX_W), lambda i: (i, 0)),
            pl.BlockSpec((TILE, IDX_W), lambda i: (i, 0)),
            pl.BlockSpec((1, n_exp), lambda i: (0, 0)),
        ],
        out_shape=[
            jax.ShapeDtypeStruct((n_tok, d), F32),
            jax.ShapeDtypeStruct((n_tok, d // 2), U32),
            jax.ShapeDtypeStruct((n_tok, IDX_W), I32),
            jax.ShapeDtypeStruct((n_tok, IDX_W), F32),
            jax.ShapeDtypeStruct((n_tok, IDX_W), I32),
            jax.ShapeDtypeStruct((1, n_exp), F32),
        ],
        scratch_shapes=[pltpu.VMEM((1, n_exp), F32)],
        compiler_params=_cparams("arbitrary"),
        name="post_mixer",
    )(z, xa, xb, mod, nw, w_out, rw_cat, rw_hi, router_b)
    return outs


def _dispatch_kernel(pstart_ref, cnt_ref, slot_ref, h2_ref, xs_ref, zbuf, sem, zsem, *, n_exp):
    def pad_fill(e, wait):
        rem = cnt_ref[e] % MOE_BLK
        pad = jnp.where(rem == 0, 0, MOE_BLK - rem)
        base = pstart_ref[e] + cnt_ref[e]
        head = jnp.minimum(pad, jnp.bitwise_and(-base, SUBLANES - 1))

        def fill(off, size, cond):
            copy = pltpu.make_async_copy(zbuf.at[pl.ds(0, size)], xs_ref.at[pl.ds(off, size)], zsem)

            @pl.when(cond)
            def _():
                copy.wait() if wait else copy.start()

        for r in range(SUBLANES - 1):
            fill(base + r, 1, r < head)
        off = base + head
        rest = pad - head
        for bit in reversed(range(SUBLANES.bit_length() - 1, MOE_BLK.bit_length() - 1)):
            size = 1 << bit
            take = (rest >> bit) & 1
            fill(pl.multiple_of(off, SUBLANES), size, take == 1)
            off = off + take * size

    @pl.when(pl.program_id(0) == 0)
    def _():
        zbuf[...] = jnp.zeros_like(zbuf)

        def fill(e, c):
            pad_fill(e, False)
            return c

        def fill_wait(e, c):
            pad_fill(e, True)
            return c

        lax.fori_loop(0, n_exp, fill, 0)
        lax.fori_loop(0, n_exp, fill_wait, 0)

    def row_copy(t8, r, k):
        s = slot_ref[0, 0, (t8 * SUBLANES + r) * TOP_K + k]
        return pltpu.make_async_copy(h2_ref.at[t8, pl.ds(r, 1)], xs_ref.at[pl.ds(s, 1)], sem)

    def issue(t8, c):
        for r in range(SUBLANES):
            for k in range(TOP_K):
                row_copy(t8, r, k).start(priority=k % 2)
        return c

    lax.fori_loop(0, TILE // SUBLANES, issue, 0)
    for k in range(TOP_K):
        pltpu.make_async_copy(xs_ref.at[pl.ds(0, TILE)], xs_ref.at[pl.ds(0, TILE)], sem).wait()


def _dispatch(pstart, cnt, slots, h2p, n_slots):
    n_tok, d = h2p.shape
    n_tiles = n_tok // TILE
    grid_spec = pltpu.PrefetchScalarGridSpec(
        num_scalar_prefetch=2,
        grid=(n_tiles,),
        in_specs=[
            pl.BlockSpec((1, 1, TILE * TOP_K), lambda i, ps, ct: (i, 0, 0), memory_space=pltpu.SMEM),
            pl.BlockSpec((TILE // SUBLANES, SUBLANES, d), lambda i, ps, ct: (i, 0, 0)),
        ],
        out_specs=pl.BlockSpec(memory_space=pl.ANY),
        scratch_shapes=[pltpu.VMEM((MOE_BLK // 2, d), U32), pltpu.SemaphoreType.DMA(()),
                        pltpu.SemaphoreType.DMA(())],
    )
    return pl.pallas_call(
        functools.partial(_dispatch_kernel, n_exp=pstart.shape[0]),
        grid_spec=grid_spec,
        out_shape=jax.ShapeDtypeStruct((n_slots, d), U32),
        compiler_params=_cparams("arbitrary"),
        name="moe_dispatch",
    )(pstart, cnt, slots.reshape(n_tiles, 1, TILE * TOP_K), h2p.reshape(n_tok // SUBLANES, SUBLANES, d))


def _grouped_kernel(be_ref, meta_ref, xs_ref, wg_ref, wu_ref, wd_ref, ys_ref, wg_s, wu_s, wd_s):
    i = pl.program_id(0)
    used = i < meta_ref[0]
    new_expert = jnp.logical_or(i == 0, be_ref[i] != be_ref[jnp.maximum(i - 1, 0)])

    @pl.when(jnp.logical_and(used, new_expert))
    def _():
        wg_s[...] = wg_ref[0, 0].astype(BF16)
        wu_s[...] = wu_ref[0, 0].astype(BF16)
        wd_s[...] = wd_ref[0, 0].astype(BF16)

    @pl.when(used)
    def _():
        xb = _unpack_bf16_pairs(xs_ref[...])
        a = _silu(_dot(xb, wg_s[...])) * _dot(xb, wu_s[...])
        ys_ref[...] = _dot(a.astype(BF16), wd_s[...])

    @pl.when(jnp.logical_not(used))
    def _():
        ys_ref[...] = jnp.zeros_like(ys_ref)


def _grouped(blk_expert, n_used, xs, w_gate, w_up, w_down, layer):
    n_slots = xs.shape[0]
    d = w_gate.shape[2]
    n_blk = n_slots // MOE_BLK
    f = w_gate.shape[3]

    def row_map(i, be, meta):
        return (jnp.minimum(i, jnp.maximum(meta[0] - 1, 0)), 0)

    grid_spec = pltpu.PrefetchScalarGridSpec(
        num_scalar_prefetch=2,
        grid=(n_blk,),
        in_specs=[
            pl.BlockSpec((MOE_BLK, d // 2), row_map),
            pl.BlockSpec((1, 1, d, f), lambda i, be, meta: (layer, be[i], 0, 0)),
            pl.BlockSpec((1, 1, d, f), lambda i, be, meta: (layer, be[i], 0, 0)),
            pl.BlockSpec((1, 1, f, d), lambda i, be, meta: (layer, be[i], 0, 0)),
        ],
        out_specs=pl.BlockSpec((MOE_BLK, d), lambda i, be, meta: (i, 0)),
        scratch_shapes=[pltpu.VMEM((d, f), BF16), pltpu.VMEM((d, f), BF16), pltpu.VMEM((f, d), BF16)],
    )
    return pl.pallas_call(
        _grouped_kernel,
        grid_spec=grid_spec,
        out_shape=jax.ShapeDtypeStruct((n_slots, d), F32),
        compiler_params=_cparams("arbitrary"),
        name="moe_experts",
    )(blk_expert, n_used, xs, w_gate, w_up, w_down)


def _combine_kernel(slot_ref, gate_ref, h2_ref, xn_ref, mod_ref, sg_ref, su_ref, sd_ref, fw_ref, ys_ref,
                    o_ref, gbuf, sem, *, d, final):
    def row_copy(t8, r, k):
        s = slot_ref[0, 0, (t8 * SUBLANES + r) * TOP_K + k]
        return pltpu.make_async_copy(ys_ref.at[pl.ds(s, 1)], gbuf.at[k, t8, pl.ds(r, 1)], sem)

    def issue(t8, c):
        for r in range(SUBLANES):
            for k in range(TOP_K):
                row_copy(t8, r, k).start(priority=k % 2)
        return c

    lax.fori_loop(0, TILE // SUBLANES, issue, 0)
    hb = _unpack_bf16_pairs(h2_ref[...])
    a = _silu(_dot(hb, sg_ref[...])) * _dot(hb, su_ref[...])
    acc = _dot(a.astype(BF16), sd_ref[...])
    for k in range(TOP_K):
        pltpu.make_async_copy(ys_ref.at[pl.ds(0, TILE)], ys_ref.at[pl.ds(0, TILE)], sem).wait()
    gate = gate_ref[...]
    for k in range(TOP_K):
        acc = acc + gate[:, k:k + 1] * gbuf[k].reshape(TILE, d)
    x2 = xn_ref[...] + mod_ref[0][:, 5 * d:6 * d] * acc
    if final:
        x2 = x2 * lax.rsqrt(jnp.mean(x2 * x2, axis=-1, keepdims=True) + EPS) * fw_ref[...]
    o_ref[...] = x2


def _combine(slots, gates, h2p, xn, mod, sh_gate, sh_up, sh_down, fw, ys, *, mod_row_map, final):
    n_tok, d = xn.shape
    n_tiles = n_tok // TILE
    f = sh_gate.shape[1]
    return pl.pallas_call(
        functools.partial(_combine_kernel, d=d, final=final),
        grid=(n_tiles,),
        in_specs=[
            pl.BlockSpec((1, 1, TILE * TOP_K), lambda i: (i, 0, 0), memory_space=pltpu.SMEM),
            pl.BlockSpec((TILE, IDX_W), lambda i: (i, 0)),
            pl.BlockSpec((TILE, d // 2), lambda i: (i, 0)),
            pl.BlockSpec((TILE, d), lambda i: (i, 0)),
            pl.BlockSpec((1, 1, ADA_CHUNKS * d), lambda i: (mod_row_map(i), 0, 0)),
            pl.BlockSpec((d, f), lambda i: (0, 0)),
            pl.BlockSpec((d, f), lambda i: (0, 0)),
            pl.BlockSpec((f, d), lambda i: (0, 0)),
            pl.BlockSpec((1, d), lambda i: (0, 0)),
            pl.BlockSpec(memory_space=pl.ANY),
        ],
        out_specs=pl.BlockSpec((TILE, d), lambda i: (i, 0)),
        out_shape=jax.ShapeDtypeStruct((n_tok, d), F32),
        scratch_shapes=[pltpu.VMEM((TOP_K, TILE // SUBLANES, SUBLANES, d), F32), pltpu.SemaphoreType.DMA(())],
        compiler_params=_cparams("arbitrary"),
        name="moe_combine",
    )(slots.reshape(n_tiles, 1, TILE * TOP_K), gates, h2p, xn, mod, sh_gate, sh_up, sh_down, fw, ys)


def _moe(h2, xn, eidx, gates, rank, counts, mod, w_gate, w_up, w_down, sh_gate, sh_up, sh_down, fw,
         *, layer, mod_row_map, final):
    n_tok = h2.shape[0]
    n_exp = w_gate.shape[1]
    n_blk = (n_tok * TOP_K + n_exp * (MOE_BLK - 1) + MOE_BLK - 1) // MOE_BLK
    cnt = counts[0].astype(I32)
    padded = (cnt + MOE_BLK - 1) // MOE_BLK * MOE_BLK
    pend = jnp.cumsum(padded)
    pstart = pend - padded
    expert_ids = jnp.arange(n_exp, dtype=I32)
    slots = jnp.sum(jnp.where(eidx[:, :TOP_K, None] == expert_ids, pstart, 0), axis=-1) + rank[:, :TOP_K]
    n_used = pend[-1] // MOE_BLK
    blk = jnp.arange(n_blk, dtype=I32)
    be = jnp.sum((pend[None, :] <= blk[:, None] * MOE_BLK).astype(I32), axis=1)
    last_used = jnp.sum(jnp.where(blk == n_used - 1, be, 0))
    be = jnp.minimum(jnp.where(blk < n_used, be, last_used), n_exp - 1)
    xs = _dispatch(pstart, cnt, slots, h2, n_blk * MOE_BLK)
    ys = _grouped(be, n_used.reshape(1).astype(I32), xs, w_gate, w_up, w_down, layer)
    return _combine(slots, gates, h2, xn, mod, sh_gate, sh_up, sh_down, fw, ys,
                    mod_row_map=mod_row_map, final=final)


def _log_sigmoid(x):
    return jnp.minimum(x, 0.0) - jnp.log1p(jnp.exp(-jnp.abs(x)))


def _mlstm_in_kernel(x_ref, mod_ref, nw_ref, w_ref, wgcat_ref, wghi_ref, gb_ref, p_ref, st_ref, r_ref,
                     *, d, cn, qk_w, n_head, k_scale):
    n_s = 2 * n_head
    mod = mod_ref[0]
    hf = _norm_mod(x_ref[...], nw_ref[...], mod[:, 0:d], mod[:, d:2 * d])
    h = hf.astype(BF16)
    for j in range(3 * d // cn):
        c0 = j * cn
        p = _dot(h, w_ref[:, c0:c0 + cn])
        if qk_w <= c0 < 2 * qk_w:
            p = p * k_scale
        p_ref[:, c0:c0 + cn] = p.astype(BF16)
    h_lo = (hf - h.astype(F32)).astype(BF16)
    p_hi = _dot(h, wgcat_ref[...])
    g = p_hi[:, :2 * n_s] + (p_hi[:, 2 * n_s:] + _dot(h_lo, wghi_ref[...])) + gb_ref[...]
    g = GATE_CAP * jnp.tanh(g / GATE_CAP)
    li = g[:, :n_s]
    lf = _log_sigmoid(g[:, n_s:])
    t_i = lax.broadcasted_iota(I32, (TILE, TILE), 0)
    u_i = lax.broadcasted_iota(I32, (TILE, TILE), 1)
    same = (t_i // CHUNK) == (u_i // CHUNK)
    one = lambda m: jnp.where(m, 1.0, 0.0).astype(BF16)
    m_all, m_pre, m_suf = one(same), one(same & (u_i <= t_i)), one(same & (u_i >= t_i))
    parts = _split3(lf)
    msum = lambda m: _dot(m, parts[0]) + (_dot(m, parts[1]) + _dot(m, parts[2]))
    is_fwd = lax.broadcasted_iota(I32, (TILE, n_s), 1) < n_head
    b = jnp.where(is_fwd, msum(m_pre), msum(m_suf))
    r = li - b
    pos = lax.broadcasted_iota(I32, (TILE, n_s), 0) % CHUNK
    run_pre, run_suf = r, r
    step = 1
    while step < CHUNK:
        run_pre = jnp.where(pos >= step, jnp.maximum(run_pre, pltpu.roll(run_pre, step, 0)), run_pre)
        run_suf = jnp.where(pos < CHUNK - step, jnp.maximum(run_suf, pltpu.roll(run_suf, TILE - step, 0)), run_suf)
        step *= 2
    st_ref[:, 0:n_s] = b
    st_ref[:, n_s:2 * n_s] = b + jnp.where(is_fwd, run_pre, run_suf)
    st_ref[:, 2 * n_s:3 * n_s] = msum(m_all) - b + li
    r_ref[...] = r


def _mlstm_in(xt, mod, nw, w_qkvo, w_g, gate_b, *, n_batch, tiles_per_b, n_head):
    n_tok, d = xt.shape
    n_tiles = n_tok // TILE
    qk_w = d // 2
    cn = min(512, qk_w)
    n_s = 2 * n_head
    k_scale = float((qk_w // n_head) ** -0.5)
    order = jnp.arange(4 * n_head).reshape(2, 2, n_head).transpose(1, 0, 2).reshape(-1)
    w_g = w_g[:, order]
    gate_b = gate_b[:, order]
    wg_hi = w_g.astype(BF16)
    wg_cat = jnp.concatenate([wg_hi, (w_g - wg_hi.astype(F32)).astype(BF16)], axis=1)

    def mod_map(i):
        return (jnp.where(i % tiles_per_b == 0, n_batch, i // tiles_per_b), 0, 0)

    return pl.pallas_call(
        functools.partial(_mlstm_in_kernel, d=d, cn=cn, qk_w=qk_w, n_head=n_head, k_scale=k_scale),
        grid=(n_tiles,),
        in_specs=[
            pl.BlockSpec((TILE, d), lambda i: (i, 0)),
            pl.BlockSpec((1, 1, ADA_CHUNKS * d), mod_map),
            pl.BlockSpec((1, d), lambda i: (0, 0)),
            _resident((d, 3 * d), lambda i: (0, 0)),
            pl.BlockSpec((d, 4 * n_s), lambda i: (0, 0)),
            pl.BlockSpec((d, 2 * n_s), lambda i: (0, 0)),
            pl.BlockSpec((1, 2 * n_s), lambda i: (0, 0)),
        ],
        out_specs=[
            pl.BlockSpec((TILE, 3 * d), lambda i: (i, 0)),
            pl.BlockSpec((TILE, 3 * n_s), lambda i: (i, 0)),
            pl.BlockSpec((TILE, n_s), lambda i: (i, 0)),
        ],
        out_shape=[
            jax.ShapeDtypeStruct((n_tok, 3 * d), BF16),
            jax.ShapeDtypeStruct((n_tok, 3 * n_s), F32),
            jax.ShapeDtypeStruct((n_tok, n_s), F32),
        ],
        compiler_params=_cparams("parallel"),
        name="mlstm_in",
    )(xt, mod, nw, w_qkvo, wg_cat, wg_hi, gate_b)


def _lanes(x, width):
    if width <= LANES:
        return x[:, :width]
    return jnp.concatenate([x] * (width // LANES), axis=1)


def _mlstm_chunk_open(q, k, v, b_t, top_t, g_t, r_r, c_st, n_st, m_st, *, backward):
    n_t, dqk = q.shape
    dv = v.shape[1]
    tt = lax.broadcasted_iota(I32, (n_t, n_t), 0)
    ss = lax.broadcasted_iota(I32, (n_t, n_t), 1)
    seen = (ss >= tt) if backward else (ss <= tt)
    a = b_t + m_st
    m_row = jnp.maximum(a, top_t)
    w_intra = jnp.exp(jnp.where(seen, _lanes(b_t - m_row, n_t) + r_r, -jnp.inf))
    w_inter = jnp.exp(a - m_row)
    qk = lax.dot_general(q, k, (((1,), (1,)), ((), ())), preferred_element_type=F32)
    inter = _dot(q, c_st.astype(BF16))
    qn = jnp.sum(q.astype(F32) * n_st, axis=1, keepdims=True)
    b_last = b_t[0:1] if backward else b_t[n_t - 1:n_t]
    m_new = jnp.maximum(b_last + m_st, jnp.max(g_t, axis=0, keepdims=True))
    decay = jnp.exp(b_last + m_st - m_new)
    kw = k.astype(F32) * _lanes(jnp.exp(g_t - m_new), dqk)
    c_new = _lanes(decay, dv) * c_st + lax.dot_general(kw.astype(BF16), v, (((0,), (0,)), ((), ())),
                                                     preferred_element_type=F32)
    n_new = _lanes(decay, dqk) * n_st + jnp.sum(kw, axis=0, keepdims=True)
    return (qk, w_intra, w_inter, inter, qn, m_row, v), (c_new, n_new, m_new)


def _mlstm_chunk_close(qk, w_intra, w_inter, inter, qn, m_row, v):
    s = qk * w_intra
    num = _dot(s.astype(BF16), v) + _lanes(w_inter, v.shape[1]) * inter
    den = jnp.sum(s, axis=1, keepdims=True) + w_inter[:, :1] * qn
    return num / jnp.maximum(jnp.abs(den), jnp.exp(-m_row[:, :1]))


def _mlstm_scan_kernel(q_ref, k_ref, v_ref, o_ref, st_ref, r_ref, nw_ref, z_ref, hf_ref, hb_ref, c_ref, rep_ref,
                       *, n_chunk, n_ctx_chunk, dqk, dv, n_hp):
    c_ref[...] = jnp.zeros_like(c_ref)
    n_stat = 2 * 3
    for hp in range(n_hp):
        for col in range(n_stat):
            rep_ref[hp * n_stat + col] = jnp.broadcast_to(st_ref[0, hp, :, col:col + 1], rep_ref.shape[1:])
    n0 = jnp.zeros((1, dqk), F32)
    m0 = jnp.full((1, LANES), M_INIT, F32)

    def step(i, carry):
        jf = i
        jb = jnp.where(i < n_ctx_chunk, n_ctx_chunk - 1 - i, n_chunk - 1 - (i - n_ctx_chunk))
        rows = (pl.ds(pl.multiple_of(jf * CHUNK, CHUNK), CHUNK), pl.ds(pl.multiple_of(jb * CHUNK, CHUNK), CHUNK))
        chunk = (jf, jb)
        scans = [(hp, direction) for hp in range(n_hp) for direction in range(2)]
        loaded = []
        for hp, direction in scans:
            r, j = rows[direction], chunk[direction]
            qc = slice(hp * dqk, (hp + 1) * dqk)
            vc = slice(hp * dv, (hp + 1) * dv)
            sc = hp * n_stat + 3 * direction
            loaded.append((q_ref[r, qc], k_ref[r, qc], v_ref[r, vc],
                           rep_ref[sc, r, :], rep_ref[sc + 1, r, :], rep_ref[sc + 2, r, :],
                           r_ref[0, hp, direction, j], c_ref[2 * hp + direction]))
        opened = [_mlstm_chunk_open(*loaded[s], *carry[s], backward=direction == 1)
                  for s, (hp, direction) in enumerate(scans)]
        for s, (hp, direction) in enumerate(scans):
            c_ref[s] = opened[s][1][0]
            (hf_ref, hb_ref)[direction][rows[direction], hp * dv:(hp + 1) * dv] = _mlstm_chunk_close(*opened[s][0])
        return tuple((n_new, m_new) for _, (_, n_new, m_new) in opened)

    lax.fori_loop(0, n_chunk, step, ((n0, m0),) * (2 * n_hp))
    n_ctx = n_ctx_chunk * CHUNK
    n_lat = (n_chunk - n_ctx_chunk) * CHUNK
    lat = pl.ds(n_ctx, n_lat)
    for hp in range(n_hp):
        vc = slice(hp * dv, (hp + 1) * dv)
        h = hf_ref[lat, vc] + hb_ref[lat, vc]
        hn = h * lax.rsqrt(jnp.mean(h * h, axis=-1, keepdims=True) + EPS)
        y = hn * nw_ref[:, vc] * _sigmoid(o_ref[lat, vc].astype(F32))
        z_ref[:, vc] = y.astype(BF16)


def _mlstm_scan(p, st, rr, norm_w, *, n_batch, n_head, seq_all, n_ctx, d):
    dqk = d // (2 * n_head)
    dv = d // n_head
    n_hp = 2 if n_head % 2 == 0 else 1
    n_chunk = seq_all // CHUNK
    n_lat = seq_all - n_ctx
    qk_blocks = (d // 2) // (n_hp * dqk)
    v_blocks = d // (n_hp * dv)
    return pl.pallas_call(
        functools.partial(_mlstm_scan_kernel, n_chunk=n_chunk, n_ctx_chunk=n_ctx // CHUNK, dqk=dqk, dv=dv,
                          n_hp=n_hp),
        grid=(n_batch, n_head // n_hp),
        in_specs=[
            pl.BlockSpec((seq_all, n_hp * dqk), lambda b, h: (b, h)),
            pl.BlockSpec((seq_all, n_hp * dqk), lambda b, h: (b, qk_blocks + h)),
            pl.BlockSpec((seq_all, n_hp * dv), lambda b, h: (b, v_blocks + h)),
            pl.BlockSpec((seq_all, n_hp * dv), lambda b, h: (b, 2 * v_blocks + h)),
            pl.BlockSpec((1, n_hp, seq_all, STAT_W), lambda b, h: (b, h, 0, 0)),
            pl.BlockSpec((1, n_hp, 2, n_chunk, 1, CHUNK), lambda b, h: (b, h, 0, 0, 0, 0)),
            pl.BlockSpec((1, n_hp * dv), lambda b, h: (0, h)),
        ],
        out_specs=pl.BlockSpec((n_lat, n_hp * dv), lambda b, h: (b, h)),
        out_shape=jax.ShapeDtypeStruct((n_batch * n_lat, d), BF16),
        scratch_shapes=[
            pltpu.VMEM((seq_all, n_hp * dv), F32),
            pltpu.VMEM((seq_all, n_hp * dv), F32),
            pltpu.VMEM((2 * n_hp, dqk, dv), F32),
            pltpu.VMEM((n_hp * 6, seq_all, LANES), F32),
        ],
        compiler_params=_cparams("parallel", "parallel"),
        name="mlstm_scan",
    )(p, p, p, p, st, rr, norm_w)


def kernel(x, c, ctx, c_ctx, ada_w, ada_b, norm_mix_w, norm_ffn_w, conv_in_w, conv_dw_w, conv_out_w,
           mlstm_in_w, mlstm_gate_b, mlstm_norm_w, mlstm_out_w, router_w, router_bias,
           exp_gate_w, exp_up_w, exp_down_w, shared_gate_w, shared_up_w, shared_down_w, final_norm_w):
    n_batch, seq, d = x.shape
    n_ctx = ctx.shape[1]
    assert ada_w.shape[0] == 2 and n_ctx == TILE and seq % TILE == 0 and n_batch + 1 <= ADA_ROWS
    seq_all = n_ctx + seq
    tiles_per_b = seq_all // TILE
    lat_tiles_per_b = seq // TILE
    n_head = (mlstm_in_w.shape[2] - 3 * d) // 4

    cond = jnp.zeros((ADA_ROWS, d), F32).at[:n_batch].set(c).at[n_batch].set(c_ctx)
    mod = _ada_mod(cond, ada_w, ada_b)
    mod0 = mod[0].reshape(ADA_ROWS, 1, ADA_CHUNKS * d)
    mod1 = mod[1].reshape(ADA_ROWS, 1, ADA_CHUNKS * d)

    def all_mod_row(i):
        return jnp.where(i % tiles_per_b == 0, n_batch, i // tiles_per_b)

    def lat_mod_row(i):
        return i // lat_tiles_per_b

    def lat_tile(i):
        return (i // lat_tiles_per_b) * tiles_per_b + 1 + i % lat_tiles_per_b

    row = lambda w: w.reshape(1, -1)
    bf = lambda w: w.astype(BF16)
    ctx2 = ctx.reshape(n_batch * n_ctx, d)
    x2 = x.reshape(n_batch * seq, d)

    def is_ctx_tile(i):
        return i % tiles_per_b == 0

    def ctx_tile(i):
        return i // tiles_per_b

    def x_tile(i):
        return (i // tiles_per_b) * lat_tiles_per_b + jnp.maximum(i % tiles_per_b - 1, 0)

    z0 = _conv_in(ctx2, x2, mod0, row(norm_mix_w[0]), bf(conv_in_w[0]), conv_dw_w[0],
                  n_batch=n_batch, tiles_per_b=tiles_per_b, ctx_map=ctx_tile, lat_map=x_tile)
    xn0, h20, eidx0, gate0, rank0, cnt0 = _post(
        z0, ctx2, x2, mod0, row(norm_ffn_w[0]), bf(conv_out_w[0]), router_w[0], row(router_bias[0]),
        a_map=ctx_tile, b_map=x_tile, pick_a=is_ctx_tile, mod_row_map=all_mod_row)
    x1 = _moe(h20, xn0, eidx0, gate0, rank0, cnt0, mod0, exp_gate_w, exp_up_w, exp_down_w,
              bf(shared_gate_w[0]), bf(shared_up_w[0]), bf(shared_down_w[0]), row(final_norm_w),
              layer=0, mod_row_map=all_mod_row, final=False)

    w_in = mlstm_in_w[0]
    p, st, r = _mlstm_in(x1, mod1, row(norm_mix_w[1]), bf(w_in[:, :3 * d]), w_in[:, 3 * d:], row(mlstm_gate_b[0]),
                     n_batch=n_batch, tiles_per_b=tiles_per_b, n_head=n_head)
    n_chunk = seq_all // CHUNK
    st = st.reshape(n_batch, seq_all, 3, 2, n_head).transpose(0, 4, 1, 3, 2).reshape(n_batch, n_head, seq_all, 6)
    st = jnp.pad(st, ((0, 0), (0, 0), (0, 0), (0, STAT_W - 6)))
    rr = r.reshape(n_batch, n_chunk, CHUNK, 2, n_head).transpose(0, 4, 3, 1, 2)
    rr = rr.reshape(n_batch, n_head, 2, n_chunk, 1, CHUNK)
    z1 = _mlstm_scan(p, st, rr, row(mlstm_norm_w[0]), n_batch=n_batch, n_head=n_head, seq_all=seq_all,
                     n_ctx=n_ctx, d=d)
    xn1, h21, eidx1, gate1, rank1, cnt1 = _post(
        z1, x1, x1, mod1, row(norm_ffn_w[1]), bf(mlstm_out_w[0]), router_w[1], row(router_bias[1]),
        a_map=lat_tile, b_map=lambda i: 0, pick_a=lambda i: i >= 0, mod_row_map=lat_mod_row)
    out = _moe(h21, xn1, eidx1, gate1, rank1, cnt1, mod1, exp_gate_w, exp_up_w, exp_down_w,
               bf(shared_gate_w[1]), bf(shared_up_w[1]), bf(shared_down_w[1]), row(final_norm_w),
               layer=1, mod_row_map=lat_mod_row, final=True)
    return out.reshape(n_batch, seq, d)
```

```python
import functools

import jax
import jax.numpy as jnp
from jax import lax
from jax.experimental import pallas as pl
from jax.experimental.pallas import tpu as pltpu

F32 = jnp.float32
BF16 = jnp.bfloat16
I32 = jnp.int32

TILE = 256
GRID_W = 64
CHUNK = 64
TOP_K = 6
MOE_BLK = 256
IDX_W = 8
STAT_W = 8
LANES = 128
SUBLANES = 8
ADA_CHUNKS = 6
ADA_ROWS = 16
EPS = 1e-6
GATE_CAP = 15.0
M_INIT = -1e30
ROUTED_SCALE = 2.5
V7X_VMEM_LIMIT = 56 * 1024 * 1024


def _cparams(*sem):
    return pltpu.CompilerParams(dimension_semantics=sem, vmem_limit_bytes=V7X_VMEM_LIMIT)


def _resident(shape, index_map):
    return pl.BlockSpec(shape, index_map, pipeline_mode=pl.Buffered(1))


def _sigmoid(x):
    return 1.0 / (1.0 + jnp.exp(-x))


def _silu(x):
    return x * _sigmoid(x)


def _split3(a):
    hi = a.astype(BF16)
    r1 = a - hi.astype(F32)
    mid = r1.astype(BF16)
    lo = (r1 - mid.astype(F32)).astype(BF16)
    return hi, mid, lo


def _dot(a, b):
    return jnp.dot(a, b, preferred_element_type=F32)


def _norm_mod(x, w, shift, scale):
    y = x * lax.rsqrt(jnp.mean(x * x, axis=-1, keepdims=True) + EPS)
    return (y * w) * (1.0 + scale) + shift


def _ada_kernel(cond_ref, w_ref, b_ref, o_ref):
    a = _silu(cond_ref[...]).astype(BF16)
    o_ref[0] = _dot(a, w_ref[0].astype(BF16)) + b_ref[0]


def _ada_mod(cond, ada_w, ada_b):
    n_layer, d, n_out = ada_w.shape
    tn = 1024 if n_out % 1024 == 0 else n_out
    return pl.pallas_call(
        _ada_kernel,
        grid=(n_layer, n_out // tn),
        in_specs=[
            pl.BlockSpec((ADA_ROWS, d), lambda l, j: (0, 0)),
            pl.BlockSpec((1, d, tn), lambda l, j: (l, 0, j)),
            pl.BlockSpec((1, 1, tn), lambda l, j: (l, 0, j)),
        ],
        out_specs=pl.BlockSpec((1, ADA_ROWS, tn), lambda l, j: (l, 0, j)),
        out_shape=jax.ShapeDtypeStruct((n_layer, ADA_ROWS, n_out), F32),
        compiler_params=_cparams("parallel", "parallel"),
        name="ada_mod",
    )(cond, ada_w, ada_b.reshape(n_layer, 1, n_out))


def _conv_in_kernel(ctx_ref, x_ref, mod_ref, nw_ref, win_ref, wdw_ref, z_ref, *, d, cn, tiles_per_b):
    is_ctx = (pl.program_id(0) % tiles_per_b) == 0
    mod = mod_ref[0]
    xt = jnp.where(is_ctx, ctx_ref[...], x_ref[...])
    h = _norm_mod(xt, nw_ref[...], mod[:, 0:d], mod[:, d:2 * d]).astype(BF16)
    t = lax.broadcasted_iota(I32, (TILE, 1), 0)
    pos_mask = jnp.where(is_ctx, TILE - 1, GRID_W - 1)
    pos = jnp.bitwise_and(t, pos_mask)
    first = pos == 0
    last = pos == pos_mask
    for j in range(d // cn):
        c0 = j * cn
        bg = _dot(h, win_ref[:, c0:c0 + cn])
        cg = _dot(h, win_ref[:, d + c0:d + c0 + cn])
        hi = _dot(h, win_ref[:, 2 * d + c0:2 * d + c0 + cn])
        u = cg * hi
        u_prev = jnp.where(first, 0.0, pltpu.roll(u, 1, 0))
        u_next = jnp.where(last, 0.0, pltpu.roll(u, TILE - 1, 0))
        w = wdw_ref[:, c0:c0 + cn]
        y = u_prev * w[0:1] + u * w[1:2] + u_next * w[2:3]
        z_ref[:, c0:c0 + cn] = (bg * y).astype(BF16)


def _conv_in(ctx2, x2, mod, nw, w_in, w_dw, *, n_batch, tiles_per_b, ctx_map, lat_map):
    d = x2.shape[1]
    n_tiles = n_batch * tiles_per_b
    n_tok = n_tiles * TILE
    cn = min(512, d)

    def mod_map(i):
        return (jnp.where(i % tiles_per_b == 0, n_batch, i // tiles_per_b), 0, 0)

    return pl.pallas_call(
        functools.partial(_conv_in_kernel, d=d, cn=cn, tiles_per_b=tiles_per_b),
        grid=(n_tiles,),
        in_specs=[
            pl.BlockSpec((TILE, d), lambda i: (ctx_map(i), 0)),
            pl.BlockSpec((TILE, d), lambda i: (lat_map(i), 0)),
            pl.BlockSpec((1, 1, ADA_CHUNKS * d), mod_map),
            pl.BlockSpec((1, d), lambda i: (0, 0)),
            _resident((d, 3 * d), lambda i: (0, 0)),
            pl.BlockSpec((3, d), lambda i: (0, 0)),
        ],
        out_specs=pl.BlockSpec((TILE, d), lambda i: (i, 0)),
        out_shape=jax.ShapeDtypeStruct((n_tok, d), BF16),
        compiler_params=_cparams("parallel"),
        name="conv_in",
    )(ctx2, x2, mod, nw, w_in, w_dw)


def _post_kernel(z_ref, xa_ref, xb_ref, mod_ref, nw_ref, wout_ref, rwcat_ref, rwhi_ref, rb_ref,
                 xn_ref, h2_ref, eidx_ref, gate_ref, rank_ref, cnt_ref, carry_ref, *, d, n_exp, pick_a):
    i = pl.program_id(0)

    @pl.when(i == 0)
    def _():
        carry_ref[...] = jnp.zeros_like(carry_ref)

    mod = mod_ref[0]
    y = _dot(z_ref[...], wout_ref[...])
    xn = jnp.where(pick_a(i), xa_ref[...], xb_ref[...]) + mod[:, 2 * d:3 * d] * y
    xn_ref[...] = xn
    h2 = _norm_mod(xn, nw_ref[...], mod[:, 3 * d:4 * d], mod[:, 4 * d:5 * d])
    h2_ref[...] = h2

    h2_hi = h2.astype(BF16)
    h2_lo = (h2 - h2_hi.astype(F32)).astype(BF16)
    p_hi = _dot(h2_hi, rwcat_ref[...])
    logits = p_hi[:, :n_exp] + (p_hi[:, n_exp:] + _dot(h2_lo, rwhi_ref[...]))
    scores = _sigmoid(logits)
    lane = lax.broadcasted_iota(I32, (TILE, n_exp), 1)
    lane_f = lane.astype(F32)
    work = scores + rb_ref[...]
    onehots, picks = [], []
    for _ in range(TOP_K):
        mx = jnp.max(work, axis=1, keepdims=True)
        first_max = jnp.min(jnp.where(work == mx, lane_f, float(n_exp)), axis=1, keepdims=True)
        oh = lane_f == first_max
        onehots.append(oh)
        picks.append(first_max)
        work = jnp.where(oh, -jnp.inf, work)
    sel = onehots[0]
    for oh in onehots[1:]:
        sel = jnp.logical_or(sel, oh)
    picked = jnp.where(sel, scores, 0.0)
    gates = picked / jnp.sum(picked, axis=1, keepdims=True) * ROUTED_SCALE
    sel_f = jnp.where(sel, 1.0, 0.0)
    r_i = lax.broadcasted_iota(I32, (TILE, TILE), 0)
    c_i = lax.broadcasted_iota(I32, (TILE, TILE), 1)
    before = jnp.where(c_i < r_i, 1.0, 0.0).astype(BF16)
    cum = _dot(before, sel_f.astype(BF16)) + carry_ref[...]

    eidx_ref[...] = jnp.zeros_like(eidx_ref)
    gate_ref[...] = jnp.zeros_like(gate_ref)
    rank_ref[...] = jnp.zeros_like(rank_ref)
    for k, oh in enumerate(onehots):
        eidx_ref[:, k:k + 1] = picks[k].astype(I32)
        gate_ref[:, k:k + 1] = jnp.sum(jnp.where(oh, gates, 0.0), axis=1, keepdims=True)
        rank_ref[:, k:k + 1] = jnp.sum(jnp.where(oh, cum, 0.0), axis=1, keepdims=True).astype(I32)

    total = carry_ref[...] + jnp.sum(sel_f, axis=0, keepdims=True)
    carry_ref[...] = total
    cnt_ref[...] = total


def _post(z, xa, xb, mod, nw, w_out, router_w, router_b, *, a_map, b_map, pick_a, mod_row_map):
    rw_hi = router_w.astype(BF16)
    rw_lo = (router_w - rw_hi.astype(F32)).astype(BF16)
    rw_cat = jnp.concatenate([rw_hi, rw_lo], axis=1)
    n_tok, d = z.shape
    n_tiles = n_tok // TILE
    n_exp = router_w.shape[1]
    outs = pl.pallas_call(
        functools.partial(_post_kernel, d=d, n_exp=n_exp, pick_a=pick_a),
        grid=(n_tiles,),
        in_specs=[
            pl.BlockSpec((TILE, d), lambda i: (i, 0)),
            pl.BlockSpec((TILE, d), lambda i: (a_map(i), 0)),
            pl.BlockSpec((TILE, d), lambda i: (b_map(i), 0)),
            pl.BlockSpec((1, 1, ADA_CHUNKS * d), lambda i: (mod_row_map(i), 0, 0)),
            pl.BlockSpec((1, d), lambda i: (0, 0)),
            _resident((d, d), lambda i: (0, 0)),
            pl.BlockSpec((d, 2 * n_exp), lambda i: (0, 0)),
            pl.BlockSpec((d, n_exp), lambda i: (0, 0)),
            pl.BlockSpec((1, n_exp), lambda i: (0, 0)),
        ],
        out_specs=[
            pl.BlockSpec((TILE, d), lambda i: (i, 0)),
            pl.BlockSpec((TILE, d), lambda i: (i, 0)),
            pl.BlockSpec((TILE, IDX_W), lambda i: (i, 0)),
            pl.BlockSpec((TILE, IDX_W), lambda i: (i, 0)),
            pl.BlockSpec((TILE, IDX_W), lambda i: (i, 0)),
            pl.BlockSpec((1, n_exp), lambda i: (0, 0)),
        ],
        out_shape=[
            jax.ShapeDtypeStruct((n_tok, d), F32),
            jax.ShapeDtypeStruct((n_tok, d), F32),
            jax.ShapeDtypeStruct((n_tok, IDX_W), I32),
            jax.ShapeDtypeStruct((n_tok, IDX_W), F32),
            jax.ShapeDtypeStruct((n_tok, IDX_W), I32),
            jax.ShapeDtypeStruct((1, n_exp), F32),
        ],
        scratch_shapes=[pltpu.VMEM((1, n_exp), F32)],
        compiler_params=_cparams("arbitrary"),
        name="post_mixer",
    )(z, xa, xb, mod, nw, w_out, rw_cat, rw_hi, router_b)
    return outs


def _dispatch_kernel(pstart_ref, cnt_ref, slot_ref, h2_ref, xs_ref, zbuf, sem, zsem, *, n_exp):
    def pad_fill(e, wait):
        rem = cnt_ref[e] % MOE_BLK
        pad = jnp.where(rem == 0, 0, MOE_BLK - rem)
        base = pstart_ref[e] + cnt_ref[e]
        head = jnp.minimum(pad, jnp.bitwise_and(-base, SUBLANES - 1))

        def fill(off, size, cond):
            copy = pltpu.make_async_copy(zbuf.at[pl.ds(0, size)], xs_ref.at[pl.ds(off, size)], zsem)

            @pl.when(cond)
            def _():
                copy.wait() if wait else copy.start()

        for r in range(SUBLANES - 1):
            fill(base + r, 1, r < head)
        off = base + head
        rest = pad - head
        for bit in reversed(range(SUBLANES.bit_length() - 1, MOE_BLK.bit_length() - 1)):
            size = 1 << bit
            take = (rest >> bit) & 1
            fill(pl.multiple_of(off, SUBLANES), size, take == 1)
            off = off + take * size

    @pl.when(pl.program_id(0) == 0)
    def _():
        zbuf[...] = jnp.zeros_like(zbuf)

        def fill(e, c):
            pad_fill(e, False)
            return c

        def fill_wait(e, c):
            pad_fill(e, True)
            return c

        lax.fori_loop(0, n_exp, fill, 0)
        lax.fori_loop(0, n_exp, fill_wait, 0)

    def row_copy(t8, r, k):
        s = slot_ref[0, 0, (t8 * SUBLANES + r) * TOP_K + k]
        return pltpu.make_async_copy(h2_ref.at[t8, pl.ds(r, 1)], xs_ref.at[pl.ds(s, 1)], sem)

    def issue(t8, c):
        for r in range(SUBLANES):
            for k in range(TOP_K):
                row_copy(t8, r, k).start(priority=k % 2)
        return c

    lax.fori_loop(0, TILE // SUBLANES, issue, 0)
    for k in range(TOP_K):
        pltpu.make_async_copy(xs_ref.at[pl.ds(0, TILE)], xs_ref.at[pl.ds(0, TILE)], sem).wait()


def _dispatch(pstart, cnt, slots, h2, n_slots):
    n_tok, d = h2.shape
    n_tiles = n_tok // TILE
    grid_spec = pltpu.PrefetchScalarGridSpec(
        num_scalar_prefetch=2,
        grid=(n_tiles,),
        in_specs=[
            pl.BlockSpec((1, 1, TILE * TOP_K), lambda i, ps, ct: (i, 0, 0), memory_space=pltpu.SMEM),
            pl.BlockSpec((TILE // SUBLANES, SUBLANES, d), lambda i, ps, ct: (i, 0, 0)),
        ],
        out_specs=pl.BlockSpec(memory_space=pl.ANY),
        scratch_shapes=[pltpu.VMEM((MOE_BLK // 2, d), F32), pltpu.SemaphoreType.DMA(()),
                        pltpu.SemaphoreType.DMA(())],
    )
    return pl.pallas_call(
        functools.partial(_dispatch_kernel, n_exp=pstart.shape[0]),
        grid_spec=grid_spec,
        out_shape=jax.ShapeDtypeStruct((n_slots, d), F32),
        compiler_params=_cparams("arbitrary"),
        name="moe_dispatch",
    )(pstart, cnt, slots.reshape(n_tiles, 1, TILE * TOP_K), h2.reshape(n_tok // SUBLANES, SUBLANES, d))


def _grouped_kernel(be_ref, meta_ref, xs_ref, wg_ref, wu_ref, wd_ref, ys_ref, wg_s, wu_s, wd_s):
    i = pl.program_id(0)
    used = i < meta_ref[0]
    new_expert = jnp.logical_or(i == 0, be_ref[i] != be_ref[jnp.maximum(i - 1, 0)])

    @pl.when(jnp.logical_and(used, new_expert))
    def _():
        wg_s[...] = wg_ref[0, 0].astype(BF16)
        wu_s[...] = wu_ref[0, 0].astype(BF16)
        wd_s[...] = wd_ref[0, 0].astype(BF16)

    @pl.when(used)
    def _():
        xb = xs_ref[...].astype(BF16)
        a = _silu(_dot(xb, wg_s[...])) * _dot(xb, wu_s[...])
        ys_ref[...] = _dot(a.astype(BF16), wd_s[...])

    @pl.when(jnp.logical_not(used))
    def _():
        ys_ref[...] = jnp.zeros_like(ys_ref)


def _grouped(blk_expert, n_used, xs, w_gate, w_up, w_down, layer):
    n_slots, d = xs.shape
    n_blk = n_slots // MOE_BLK
    f = w_gate.shape[3]

    def row_map(i, be, meta):
        return (jnp.minimum(i, jnp.maximum(meta[0] - 1, 0)), 0)

    grid_spec = pltpu.PrefetchScalarGridSpec(
        num_scalar_prefetch=2,
        grid=(n_blk,),
        in_specs=[
            pl.BlockSpec((MOE_BLK, d), row_map),
            pl.BlockSpec((1, 1, d, f), lambda i, be, meta: (layer, be[i], 0, 0)),
            pl.BlockSpec((1, 1, d, f), lambda i, be, meta: (layer, be[i], 0, 0)),
            pl.BlockSpec((1, 1, f, d), lambda i, be, meta: (layer, be[i], 0, 0)),
        ],
        out_specs=pl.BlockSpec((MOE_BLK, d), lambda i, be, meta: (i, 0)),
        scratch_shapes=[pltpu.VMEM((d, f), BF16), pltpu.VMEM((d, f), BF16), pltpu.VMEM((f, d), BF16)],
    )
    return pl.pallas_call(
        _grouped_kernel,
        grid_spec=grid_spec,
        out_shape=jax.ShapeDtypeStruct((n_slots, d), F32),
        compiler_params=_cparams("arbitrary"),
        name="moe_experts",
    )(blk_expert, n_used, xs, w_gate, w_up, w_down)


def _combine_kernel(slot_ref, gate_ref, h2_ref, xn_ref, mod_ref, sg_ref, su_ref, sd_ref, fw_ref, ys_ref,
                    o_ref, gbuf, sem, *, d, final):
    def row_copy(t8, r, k):
        s = slot_ref[0, 0, (t8 * SUBLANES + r) * TOP_K + k]
        return pltpu.make_async_copy(ys_ref.at[pl.ds(s, 1)], gbuf.at[k, t8, pl.ds(r, 1)], sem)

    def issue(t8, c):
        for r in range(SUBLANES):
            for k in range(TOP_K):
                row_copy(t8, r, k).start(priority=k % 2)
        return c

    lax.fori_loop(0, TILE // SUBLANES, issue, 0)
    hb = h2_ref[...].astype(BF16)
    a = _silu(_dot(hb, sg_ref[...])) * _dot(hb, su_ref[...])
    acc = _dot(a.astype(BF16), sd_ref[...])
    for k in range(TOP_K):
        pltpu.make_async_copy(ys_ref.at[pl.ds(0, TILE)], ys_ref.at[pl.ds(0, TILE)], sem).wait()
    gate = gate_ref[...]
    for k in range(TOP_K):
        acc = acc + gate[:, k:k + 1] * gbuf[k].reshape(TILE, d)
    x2 = xn_ref[...] + mod_ref[0][:, 5 * d:6 * d] * acc
    if final:
        x2 = x2 * lax.rsqrt(jnp.mean(x2 * x2, axis=-1, keepdims=True) + EPS) * fw_ref[...]
    o_ref[...] = x2


def _combine(slots, gates, h2, xn, mod, sh_gate, sh_up, sh_down, fw, ys, *, mod_row_map, final):
    n_tok, d = h2.shape
    n_tiles = n_tok // TILE
    f = sh_gate.shape[1]
    return pl.pallas_call(
        functools.partial(_combine_kernel, d=d, final=final),
        grid=(n_tiles,),
        in_specs=[
            pl.BlockSpec((1, 1, TILE * TOP_K), lambda i: (i, 0, 0), memory_space=pltpu.SMEM),
            pl.BlockSpec((TILE, IDX_W), lambda i: (i, 0)),
            pl.BlockSpec((TILE, d), lambda i: (i, 0)),
            pl.BlockSpec((TILE, d), lambda i: (i, 0)),
            pl.BlockSpec((1, 1, ADA_CHUNKS * d), lambda i: (mod_row_map(i), 0, 0)),
            pl.BlockSpec((d, f), lambda i: (0, 0)),
            pl.BlockSpec((d, f), lambda i: (0, 0)),
            pl.BlockSpec((f, d), lambda i: (0, 0)),
            pl.BlockSpec((1, d), lambda i: (0, 0)),
            pl.BlockSpec(memory_space=pl.ANY),
        ],
        out_specs=pl.BlockSpec((TILE, d), lambda i: (i, 0)),
        out_shape=jax.ShapeDtypeStruct((n_tok, d), F32),
        scratch_shapes=[pltpu.VMEM((TOP_K, TILE // SUBLANES, SUBLANES, d), F32), pltpu.SemaphoreType.DMA(())],
        compiler_params=_cparams("arbitrary"),
        name="moe_combine",
    )(slots.reshape(n_tiles, 1, TILE * TOP_K), gates, h2, xn, mod, sh_gate, sh_up, sh_down, fw, ys)


def _moe(h2, xn, eidx, gates, rank, counts, mod, w_gate, w_up, w_down, sh_gate, sh_up, sh_down, fw,
         *, layer, mod_row_map, final):
    n_tok = h2.shape[0]
    n_exp = w_gate.shape[1]
    n_blk = (n_tok * TOP_K + n_exp * (MOE_BLK - 1) + MOE_BLK - 1) // MOE_BLK
    cnt = counts[0].astype(I32)
    padded = (cnt + MOE_BLK - 1) // MOE_BLK * MOE_BLK
    pend = jnp.cumsum(padded)
    pstart = pend - padded
    expert_ids = jnp.arange(n_exp, dtype=I32)
    slots = jnp.sum(jnp.where(eidx[:, :TOP_K, None] == expert_ids, pstart, 0), axis=-1) + rank[:, :TOP_K]
    n_used = pend[-1] // MOE_BLK
    blk = jnp.arange(n_blk, dtype=I32)
    be = jnp.sum((pend[None, :] <= blk[:, None] * MOE_BLK).astype(I32), axis=1)
    last_used = jnp.sum(jnp.where(blk == n_used - 1, be, 0))
    be = jnp.minimum(jnp.where(blk < n_used, be, last_used), n_exp - 1)
    xs = _dispatch(pstart, cnt, slots, h2, n_blk * MOE_BLK)
    ys = _grouped(be, n_used.reshape(1).astype(I32), xs, w_gate, w_up, w_down, layer)
    return _combine(slots, gates, h2, xn, mod, sh_gate, sh_up, sh_down, fw, ys,
                    mod_row_map=mod_row_map, final=final)


def _log_sigmoid(x):
    return jnp.minimum(x, 0.0) - jnp.log1p(jnp.exp(-jnp.abs(x)))


def _mlstm_in_kernel(x_ref, mod_ref, nw_ref, w_ref, wgcat_ref, wghi_ref, gb_ref, p_ref, st_ref, r_ref,
                     *, d, cn, qk_w, n_head, k_scale):
    n_s = 2 * n_head
    mod = mod_ref[0]
    hf = _norm_mod(x_ref[...], nw_ref[...], mod[:, 0:d], mod[:, d:2 * d])
    h = hf.astype(BF16)
    for j in range(3 * d // cn):
        c0 = j * cn
        p = _dot(h, w_ref[:, c0:c0 + cn])
        if qk_w <= c0 < 2 * qk_w:
            p = p * k_scale
        p_ref[:, c0:c0 + cn] = p.astype(BF16)
    h_lo = (hf - h.astype(F32)).astype(BF16)
    p_hi = _dot(h, wgcat_ref[...])
    g = p_hi[:, :2 * n_s] + (p_hi[:, 2 * n_s:] + _dot(h_lo, wghi_ref[...])) + gb_ref[...]
    g = GATE_CAP * jnp.tanh(g / GATE_CAP)
    li = g[:, :n_s]
    lf = _log_sigmoid(g[:, n_s:])
    t_i = lax.broadcasted_iota(I32, (TILE, TILE), 0)
    u_i = lax.broadcasted_iota(I32, (TILE, TILE), 1)
    same = (t_i // CHUNK) == (u_i // CHUNK)
    one = lambda m: jnp.where(m, 1.0, 0.0).astype(BF16)
    m_all, m_pre, m_suf = one(same), one(same & (u_i <= t_i)), one(same & (u_i >= t_i))
    parts = _split3(lf)
    msum = lambda m: _dot(m, parts[0]) + (_dot(m, parts[1]) + _dot(m, parts[2]))
    is_fwd = lax.broadcasted_iota(I32, (TILE, n_s), 1) < n_head
    b = jnp.where(is_fwd, msum(m_pre), msum(m_suf))
    r = li - b
    pos = lax.broadcasted_iota(I32, (TILE, n_s), 0) % CHUNK
    run_pre, run_suf = r, r
    step = 1
    while step < CHUNK:
        run_pre = jnp.where(pos >= step, jnp.maximum(run_pre, pltpu.roll(run_pre, step, 0)), run_pre)
        run_suf = jnp.where(pos < CHUNK - step, jnp.maximum(run_suf, pltpu.roll(run_suf, TILE - step, 0)), run_suf)
        step *= 2
    st_ref[:, 0:n_s] = b
    st_ref[:, n_s:2 * n_s] = b + jnp.where(is_fwd, run_pre, run_suf)
    st_ref[:, 2 * n_s:3 * n_s] = msum(m_all) - b + li
    r_ref[...] = r


def _mlstm_in(xt, mod, nw, w_qkvo, w_g, gate_b, *, n_batch, tiles_per_b, n_head):
    n_tok, d = xt.shape
    n_tiles = n_tok // TILE
    qk_w = d // 2
    cn = min(512, qk_w)
    n_s = 2 * n_head
    k_scale = float((qk_w // n_head) ** -0.5)
    order = jnp.arange(4 * n_head).reshape(2, 2, n_head).transpose(1, 0, 2).reshape(-1)
    w_g = w_g[:, order]
    gate_b = gate_b[:, order]
    wg_hi = w_g.astype(BF16)
    wg_cat = jnp.concatenate([wg_hi, (w_g - wg_hi.astype(F32)).astype(BF16)], axis=1)

    def mod_map(i):
        return (jnp.where(i % tiles_per_b == 0, n_batch, i // tiles_per_b), 0, 0)

    return pl.pallas_call(
        functools.partial(_mlstm_in_kernel, d=d, cn=cn, qk_w=qk_w, n_head=n_head, k_scale=k_scale),
        grid=(n_tiles,),
        in_specs=[
            pl.BlockSpec((TILE, d), lambda i: (i, 0)),
            pl.BlockSpec((1, 1, ADA_CHUNKS * d), mod_map),
            pl.BlockSpec((1, d), lambda i: (0, 0)),
            _resident((d, 3 * d), lambda i: (0, 0)),
            pl.BlockSpec((d, 4 * n_s), lambda i: (0, 0)),
            pl.BlockSpec((d, 2 * n_s), lambda i: (0, 0)),
            pl.BlockSpec((1, 2 * n_s), lambda i: (0, 0)),
        ],
        out_specs=[
            pl.BlockSpec((TILE, 3 * d), lambda i: (i, 0)),
            pl.BlockSpec((TILE, 3 * n_s), lambda i: (i, 0)),
            pl.BlockSpec((TILE, n_s), lambda i: (i, 0)),
        ],
        out_shape=[
            jax.ShapeDtypeStruct((n_tok, 3 * d), BF16),
            jax.ShapeDtypeStruct((n_tok, 3 * n_s), F32),
            jax.ShapeDtypeStruct((n_tok, n_s), F32),
        ],
        compiler_params=_cparams("parallel"),
        name="mlstm_in",
    )(xt, mod, nw, w_qkvo, wg_cat, wg_hi, gate_b)


def _lanes(x, width):
    if width <= LANES:
        return x[:, :width]
    return jnp.concatenate([x] * (width // LANES), axis=1)


def _mlstm_chunk_open(q, k, v, b_t, top_t, g_t, r_r, c_st, n_st, m_st, *, backward):
    n_t, dqk = q.shape
    dv = v.shape[1]
    tt = lax.broadcasted_iota(I32, (n_t, n_t), 0)
    ss = lax.broadcasted_iota(I32, (n_t, n_t), 1)
    seen = (ss >= tt) if backward else (ss <= tt)
    a = b_t + m_st
    m_row = jnp.maximum(a, top_t)
    w_intra = jnp.exp(jnp.where(seen, _lanes(b_t - m_row, n_t) + r_r, -jnp.inf))
    w_inter = jnp.exp(a - m_row)
    qk = lax.dot_general(q, k, (((1,), (1,)), ((), ())), preferred_element_type=F32)
    inter = _dot(q, c_st.astype(BF16))
    qn = jnp.sum(q.astype(F32) * n_st, axis=1, keepdims=True)
    b_last = b_t[0:1] if backward else b_t[n_t - 1:n_t]
    m_new = jnp.maximum(b_last + m_st, jnp.max(g_t, axis=0, keepdims=True))
    decay = jnp.exp(b_last + m_st - m_new)
    kw = k.astype(F32) * _lanes(jnp.exp(g_t - m_new), dqk)
    c_new = _lanes(decay, dv) * c_st + lax.dot_general(kw.astype(BF16), v, (((0,), (0,)), ((), ())),
                                                     preferred_element_type=F32)
    n_new = _lanes(decay, dqk) * n_st + jnp.sum(kw, axis=0, keepdims=True)
    return (qk, w_intra, w_inter, inter, qn, m_row, v), (c_new, n_new, m_new)


def _mlstm_chunk_close(qk, w_intra, w_inter, inter, qn, m_row, v):
    s = qk * w_intra
    num = _dot(s.astype(BF16), v) + _lanes(w_inter, v.shape[1]) * inter
    den = jnp.sum(s, axis=1, keepdims=True) + w_inter[:, :1] * qn
    return num / jnp.maximum(jnp.abs(den), jnp.exp(-m_row[:, :1]))


def _mlstm_scan_kernel(q_ref, k_ref, v_ref, o_ref, st_ref, r_ref, nw_ref, z_ref, hf_ref, hb_ref, c_ref, rep_ref,
                       *, n_chunk, n_ctx_chunk, dqk, dv, n_hp):
    c_ref[...] = jnp.zeros_like(c_ref)
    n_stat = 2 * 3
    for hp in range(n_hp):
        for col in range(n_stat):
            rep_ref[hp * n_stat + col] = jnp.broadcast_to(st_ref[0, hp, :, col:col + 1], rep_ref.shape[1:])
    n0 = jnp.zeros((1, dqk), F32)
    m0 = jnp.full((1, LANES), M_INIT, F32)

    def step(i, carry):
        jf = i
        jb = jnp.where(i < n_ctx_chunk, n_ctx_chunk - 1 - i, n_chunk - 1 - (i - n_ctx_chunk))
        rows = (pl.ds(pl.multiple_of(jf * CHUNK, CHUNK), CHUNK), pl.ds(pl.multiple_of(jb * CHUNK, CHUNK), CHUNK))
        chunk = (jf, jb)
        scans = [(hp, direction) for hp in range(n_hp) for direction in range(2)]
        loaded = []
        for hp, direction in scans:
            r, j = rows[direction], chunk[direction]
            qc = slice(hp * dqk, (hp + 1) * dqk)
            vc = slice(hp * dv, (hp + 1) * dv)
            sc = hp * n_stat + 3 * direction
            loaded.append((q_ref[r, qc], k_ref[r, qc], v_ref[r, vc],
                           rep_ref[sc, r, :], rep_ref[sc + 1, r, :], rep_ref[sc + 2, r, :],
                           r_ref[0, hp, direction, j], c_ref[2 * hp + direction]))
        opened = [_mlstm_chunk_open(*loaded[s], *carry[s], backward=direction == 1)
                  for s, (hp, direction) in enumerate(scans)]
        for s, (hp, direction) in enumerate(scans):
            c_ref[s] = opened[s][1][0]
            (hf_ref, hb_ref)[direction][rows[direction], hp * dv:(hp + 1) * dv] = _mlstm_chunk_close(*opened[s][0])
        return tuple((n_new, m_new) for _, (_, n_new, m_new) in opened)

    lax.fori_loop(0, n_chunk, step, ((n0, m0),) * (2 * n_hp))
    n_ctx = n_ctx_chunk * CHUNK
    n_lat = (n_chunk - n_ctx_chunk) * CHUNK
    lat = pl.ds(n_ctx, n_lat)
    for hp in range(n_hp):
        vc = slice(hp * dv, (hp + 1) * dv)
        h = hf_ref[lat, vc] + hb_ref[lat, vc]
        hn = h * lax.rsqrt(jnp.mean(h * h, axis=-1, keepdims=True) + EPS)
        y = hn * nw_ref[:, vc] * _sigmoid(o_ref[lat, vc].astype(F32))
        z_ref[:, vc] = y.astype(BF16)


def _mlstm_scan(p, st, rr, norm_w, *, n_batch, n_head, seq_all, n_ctx, d):
    dqk = d // (2 * n_head)
    dv = d // n_head
    n_hp = 2 if n_head % 2 == 0 else 1
    n_chunk = seq_all // CHUNK
    n_lat = seq_all - n_ctx
    qk_blocks = (d // 2) // (n_hp * dqk)
    v_blocks = d // (n_hp * dv)
    return pl.pallas_call(
        functools.partial(_mlstm_scan_kernel, n_chunk=n_chunk, n_ctx_chunk=n_ctx // CHUNK, dqk=dqk, dv=dv,
                          n_hp=n_hp),
        grid=(n_batch, n_head // n_hp),
        in_specs=[
            pl.BlockSpec((seq_all, n_hp * dqk), lambda b, h: (b, h)),
            pl.BlockSpec((seq_all, n_hp * dqk), lambda b, h: (b, qk_blocks + h)),
            pl.BlockSpec((seq_all, n_hp * dv), lambda b, h: (b, v_blocks + h)),
            pl.BlockSpec((seq_all, n_hp * dv), lambda b, h: (b, 2 * v_blocks + h)),
            pl.BlockSpec((1, n_hp, seq_all, STAT_W), lambda b, h: (b, h, 0, 0)),
            pl.BlockSpec((1, n_hp, 2, n_chunk, 1, CHUNK), lambda b, h: (b, h, 0, 0, 0, 0)),
            pl.BlockSpec((1, n_hp * dv), lambda b, h: (0, h)),
        ],
        out_specs=pl.BlockSpec((n_lat, n_hp * dv), lambda b, h: (b, h)),
        out_shape=jax.ShapeDtypeStruct((n_batch * n_lat, d), BF16),
        scratch_shapes=[
            pltpu.VMEM((seq_all, n_hp * dv), F32),
            pltpu.VMEM((seq_all, n_hp * dv), F32),
            pltpu.VMEM((2 * n_hp, dqk, dv), F32),
            pltpu.VMEM((n_hp * 6, seq_all, LANES), F32),
        ],
        compiler_params=_cparams("parallel", "parallel"),
        name="mlstm_scan",
    )(p, p, p, p, st, rr, norm_w)


def kernel(x, c, ctx, c_ctx, ada_w, ada_b, norm_mix_w, norm_ffn_w, conv_in_w, conv_dw_w, conv_out_w,
           mlstm_in_w, mlstm_gate_b, mlstm_norm_w, mlstm_out_w, router_w, router_bias,
           exp_gate_w, exp_up_w, exp_down_w, shared_gate_w, shared_up_w, shared_down_w, final_norm_w):
    n_batch, seq, d = x.shape
    n_ctx = ctx.shape[1]
    assert ada_w.shape[0] == 2 and n_ctx == TILE and seq % TILE == 0 and n_batch + 1 <= ADA_ROWS
    seq_all = n_ctx + seq
    tiles_per_b = seq_all // TILE
    lat_tiles_per_b = seq // TILE
    n_head = (mlstm_in_w.shape[2] - 3 * d) // 4

    cond = jnp.zeros((ADA_ROWS, d), F32).at[:n_batch].set(c).at[n_batch].set(c_ctx)
    mod = _ada_mod(cond, ada_w, ada_b)
    mod0 = mod[0].reshape(ADA_ROWS, 1, ADA_CHUNKS * d)
    mod1 = mod[1].reshape(ADA_ROWS, 1, ADA_CHUNKS * d)

    def all_mod_row(i):
        return jnp.where(i % tiles_per_b == 0, n_batch, i // tiles_per_b)

    def lat_mod_row(i):
        return i // lat_tiles_per_b

    def lat_tile(i):
        return (i // lat_tiles_per_b) * tiles_per_b + 1 + i % lat_tiles_per_b

    row = lambda w: w.reshape(1, -1)
    bf = lambda w: w.astype(BF16)
    ctx2 = ctx.reshape(n_batch * n_ctx, d)
    x2 = x.reshape(n_batch * seq, d)

    def is_ctx_tile(i):
        return i % tiles_per_b == 0

    def ctx_tile(i):
        return i // tiles_per_b

    def x_tile(i):
        return (i // tiles_per_b) * lat_tiles_per_b + jnp.maximum(i % tiles_per_b - 1, 0)

    z0 = _conv_in(ctx2, x2, mod0, row(norm_mix_w[0]), bf(conv_in_w[0]), conv_dw_w[0],
                  n_batch=n_batch, tiles_per_b=tiles_per_b, ctx_map=ctx_tile, lat_map=x_tile)
    xn0, h20, eidx0, gate0, rank0, cnt0 = _post(
        z0, ctx2, x2, mod0, row(norm_ffn_w[0]), bf(conv_out_w[0]), router_w[0], row(router_bias[0]),
        a_map=ctx_tile, b_map=x_tile, pick_a=is_ctx_tile, mod_row_map=all_mod_row)
    x1 = _moe(h20, xn0, eidx0, gate0, rank0, cnt0, mod0, exp_gate_w, exp_up_w, exp_down_w,
              bf(shared_gate_w[0]), bf(shared_up_w[0]), bf(shared_down_w[0]), row(final_norm_w),
              layer=0, mod_row_map=all_mod_row, final=False)

    w_in = mlstm_in_w[0]
    p, st, r = _mlstm_in(x1, mod1, row(norm_mix_w[1]), bf(w_in[:, :3 * d]), w_in[:, 3 * d:], row(mlstm_gate_b[0]),
                     n_batch=n_batch, tiles_per_b=tiles_per_b, n_head=n_head)
    n_chunk = seq_all // CHUNK
    st = st.reshape(n_batch, seq_all, 3, 2, n_head).transpose(0, 4, 1, 3, 2).reshape(n_batch, n_head, seq_all, 6)
    st = jnp.pad(st, ((0, 0), (0, 0), (0, 0), (0, STAT_W - 6)))
    rr = r.reshape(n_batch, n_chunk, CHUNK, 2, n_head).transpose(0, 4, 3, 1, 2)
    rr = rr.reshape(n_batch, n_head, 2, n_chunk, 1, CHUNK)
    z1 = _mlstm_scan(p, st, rr, row(mlstm_norm_w[0]), n_batch=n_batch, n_head=n_head, seq_all=seq_all,
                     n_ctx=n_ctx, d=d)
    xn1, h21, eidx1, gate1, rank1, cnt1 = _post(
        z1, x1, x1, mod1, row(norm_ffn_w[1]), bf(mlstm_out_w[0]), router_w[1], row(router_bias[1]),
        a_map=lat_tile, b_map=lambda i: 0, pick_a=lambda i: i >= 0, mod_row_map=lat_mod_row)
    out = _moe(h21, xn1, eidx1, gate1, rank1, cnt1, mod1, exp_gate_w, exp_up_w, exp_down_w,
               bf(shared_gate_w[1]), bf(shared_up_w[1]), bf(shared_down_w[1]), row(final_norm_w),
               layer=1, mod_row_map=lat_mod_row, final=True)
    return out.reshape(n_batch, seq, d)
```

```python
import functools

import jax
import jax.numpy as jnp
from jax import lax
from jax.experimental import pallas as pl
from jax.experimental.pallas import tpu as pltpu

F32 = jnp.float32
BF16 = jnp.bfloat16
I32 = jnp.int32

TILE = 256
GRID_W = 64
CHUNK = 64
TOP_K = 6
MOE_BLK = 256
IDX_W = 8
LANES = 128
SUBLANES = 8
ADA_CHUNKS = 6
ADA_ROWS = 16
EPS = 1e-6
GATE_CAP = 15.0
M_INIT = -1e30
ROUTED_SCALE = 2.5
V7X_VMEM_LIMIT = 56 * 1024 * 1024


def _cparams(*sem):
    return pltpu.CompilerParams(dimension_semantics=sem, vmem_limit_bytes=V7X_VMEM_LIMIT)


def _resident(shape, index_map):
    return pl.BlockSpec(shape, index_map, pipeline_mode=pl.Buffered(1))


def _sigmoid(x):
    return 1.0 / (1.0 + jnp.exp(-x))


def _silu(x):
    return x * _sigmoid(x)


def _split3(a):
    hi = a.astype(BF16)
    r1 = a - hi.astype(F32)
    mid = r1.astype(BF16)
    lo = (r1 - mid.astype(F32)).astype(BF16)
    return hi, mid, lo


def _dot(a, b):
    return jnp.dot(a, b, preferred_element_type=F32)


def _norm_mod(x, w, shift, scale):
    y = x * lax.rsqrt(jnp.mean(x * x, axis=-1, keepdims=True) + EPS)
    return (y * w) * (1.0 + scale) + shift


def _ada_kernel(cond_ref, w_ref, b_ref, o_ref):
    a = _silu(cond_ref[...]).astype(BF16)
    o_ref[0] = _dot(a, w_ref[0].astype(BF16)) + b_ref[0]


def _ada_mod(cond, ada_w, ada_b):
    n_layer, d, n_out = ada_w.shape
    tn = 1024 if n_out % 1024 == 0 else n_out
    return pl.pallas_call(
        _ada_kernel,
        grid=(n_layer, n_out // tn),
        in_specs=[
            pl.BlockSpec((ADA_ROWS, d), lambda l, j: (0, 0)),
            pl.BlockSpec((1, d, tn), lambda l, j: (l, 0, j)),
            pl.BlockSpec((1, 1, tn), lambda l, j: (l, 0, j)),
        ],
        out_specs=pl.BlockSpec((1, ADA_ROWS, tn), lambda l, j: (l, 0, j)),
        out_shape=jax.ShapeDtypeStruct((n_layer, ADA_ROWS, n_out), F32),
        compiler_params=_cparams("parallel", "parallel"),
        name="ada_mod",
    )(cond, ada_w, ada_b.reshape(n_layer, 1, n_out))


def _conv_in_kernel(ctx_ref, x_ref, mod_ref, nw_ref, win_ref, wdw_ref, z_ref, *, d, cn, tiles_per_b):
    is_ctx = (pl.program_id(0) % tiles_per_b) == 0
    mod = mod_ref[0]
    xt = jnp.where(is_ctx, ctx_ref[...], x_ref[...])
    h = _norm_mod(xt, nw_ref[...], mod[:, 0:d], mod[:, d:2 * d]).astype(BF16)
    t = lax.broadcasted_iota(I32, (TILE, 1), 0)
    pos_mask = jnp.where(is_ctx, TILE - 1, GRID_W - 1)
    pos = jnp.bitwise_and(t, pos_mask)
    first = pos == 0
    last = pos == pos_mask
    for j in range(d // cn):
        c0 = j * cn
        bg = _dot(h, win_ref[:, c0:c0 + cn])
        cg = _dot(h, win_ref[:, d + c0:d + c0 + cn])
        hi = _dot(h, win_ref[:, 2 * d + c0:2 * d + c0 + cn])
        u = cg * hi
        u_prev = jnp.where(first, 0.0, pltpu.roll(u, 1, 0))
        u_next = jnp.where(last, 0.0, pltpu.roll(u, TILE - 1, 0))
        w = wdw_ref[:, c0:c0 + cn]
        y = u_prev * w[0:1] + u * w[1:2] + u_next * w[2:3]
        z_ref[:, c0:c0 + cn] = (bg * y).astype(BF16)


def _conv_in(ctx2, x2, mod, nw, w_in, w_dw, *, n_batch, tiles_per_b, ctx_map, lat_map):
    d = x2.shape[1]
    n_tiles = n_batch * tiles_per_b
    n_tok = n_tiles * TILE
    cn = min(512, d)

    def mod_map(i):
        return (jnp.where(i % tiles_per_b == 0, n_batch, i // tiles_per_b), 0, 0)

    return pl.pallas_call(
        functools.partial(_conv_in_kernel, d=d, cn=cn, tiles_per_b=tiles_per_b),
        grid=(n_tiles,),
        in_specs=[
            pl.BlockSpec((TILE, d), lambda i: (ctx_map(i), 0)),
            pl.BlockSpec((TILE, d), lambda i: (lat_map(i), 0)),
            pl.BlockSpec((1, 1, ADA_CHUNKS * d), mod_map),
            pl.BlockSpec((1, d), lambda i: (0, 0)),
            _resident((d, 3 * d), lambda i: (0, 0)),
            pl.BlockSpec((3, d), lambda i: (0, 0)),
        ],
        out_specs=pl.BlockSpec((TILE, d), lambda i: (i, 0)),
        out_shape=jax.ShapeDtypeStruct((n_tok, d), BF16),
        compiler_params=_cparams("parallel"),
        name="conv_in",
    )(ctx2, x2, mod, nw, w_in, w_dw)


def _post_kernel(z_ref, xa_ref, xb_ref, mod_ref, nw_ref, wout_ref, rwcat_ref, rwhi_ref, rb_ref,
                 xn_ref, h2_ref, eidx_ref, gate_ref, rank_ref, cnt_ref, carry_ref, *, d, n_exp, pick_a):
    i = pl.program_id(0)

    @pl.when(i == 0)
    def _():
        carry_ref[...] = jnp.zeros_like(carry_ref)

    mod = mod_ref[0]
    y = _dot(z_ref[...], wout_ref[...])
    xn = jnp.where(pick_a(i), xa_ref[...], xb_ref[...]) + mod[:, 2 * d:3 * d] * y
    xn_ref[...] = xn
    h2 = _norm_mod(xn, nw_ref[...], mod[:, 3 * d:4 * d], mod[:, 4 * d:5 * d])
    h2_ref[...] = h2

    nt = (((1,), (1,)), ((), ()))
    h2_hi = h2.astype(BF16)
    h2_lo = (h2 - h2_hi.astype(F32)).astype(BF16)
    p_hi = lax.dot_general(rwcat_ref[...], h2_hi, nt, preferred_element_type=F32)
    logits = p_hi[:n_exp] + (p_hi[n_exp:] + lax.dot_general(rwhi_ref[...], h2_lo, nt, preferred_element_type=F32))
    scores = _sigmoid(logits)
    row_f = lax.broadcasted_iota(I32, (n_exp, TILE), 0).astype(F32)
    work = scores + rb_ref[...]
    onehots, picks = [], []
    for _ in range(TOP_K):
        mx = jnp.max(work, axis=0, keepdims=True)
        first_max = jnp.min(jnp.where(work == mx, row_f, float(n_exp)), axis=0, keepdims=True)
        oh = row_f == first_max
        onehots.append(oh)
        picks.append(first_max)
        work = jnp.where(oh, -jnp.inf, work)
    sel = onehots[0]
    for oh in onehots[1:]:
        sel = jnp.logical_or(sel, oh)
    picked = jnp.where(sel, scores, 0.0)
    gates = picked / jnp.sum(picked, axis=0, keepdims=True) * ROUTED_SCALE
    sel_f = jnp.where(sel, 1.0, 0.0)
    r_i = lax.broadcasted_iota(I32, (TILE, TILE), 0)
    c_i = lax.broadcasted_iota(I32, (TILE, TILE), 1)
    before = jnp.where(r_i < c_i, 1.0, 0.0).astype(BF16)
    cum = _dot(sel_f.astype(BF16), before) + carry_ref[...]

    eidx_ref[...] = jnp.zeros_like(eidx_ref)
    gate_ref[...] = jnp.zeros_like(gate_ref)
    rank_ref[...] = jnp.zeros_like(rank_ref)
    for k, oh in enumerate(onehots):
        eidx_ref[k:k + 1, :] = picks[k].astype(I32)
        gate_ref[k:k + 1, :] = jnp.sum(jnp.where(oh, gates, 0.0), axis=0, keepdims=True)
        rank_ref[k:k + 1, :] = jnp.sum(jnp.where(oh, cum, 0.0), axis=0, keepdims=True).astype(I32)

    total = carry_ref[...] + jnp.sum(sel_f, axis=1, keepdims=True)
    carry_ref[...] = total
    cnt_ref[...] = total


def _post(z, xa, xb, mod, nw, w_out, router_w, router_b, *, a_map, b_map, pick_a, mod_row_map):
    rw_t = router_w.T
    rw_hi = rw_t.astype(BF16)
    rw_lo = (rw_t - rw_hi.astype(F32)).astype(BF16)
    rw_cat = jnp.concatenate([rw_hi, rw_lo], axis=0)
    n_tok, d = z.shape
    n_tiles = n_tok // TILE
    n_exp = router_w.shape[1]
    outs = pl.pallas_call(
        functools.partial(_post_kernel, d=d, n_exp=n_exp, pick_a=pick_a),
        grid=(n_tiles,),
        in_specs=[
            pl.BlockSpec((TILE, d), lambda i: (i, 0)),
            pl.BlockSpec((TILE, d), lambda i: (a_map(i), 0)),
            pl.BlockSpec((TILE, d), lambda i: (b_map(i), 0)),
            pl.BlockSpec((1, 1, ADA_CHUNKS * d), lambda i: (mod_row_map(i), 0, 0)),
            pl.BlockSpec((1, d), lambda i: (0, 0)),
            _resident((d, d), lambda i: (0, 0)),
            pl.BlockSpec((2 * n_exp, d), lambda i: (0, 0)),
            pl.BlockSpec((n_exp, d), lambda i: (0, 0)),
            pl.BlockSpec((n_exp, 1), lambda i: (0, 0)),
        ],
        out_specs=[
            pl.BlockSpec((TILE, d), lambda i: (i, 0)),
            pl.BlockSpec((TILE, d), lambda i: (i, 0)),
            pl.BlockSpec((IDX_W, TILE), lambda i: (0, i)),
            pl.BlockSpec((IDX_W, TILE), lambda i: (0, i)),
            pl.BlockSpec((IDX_W, TILE), lambda i: (0, i)),
            pl.BlockSpec((n_exp, 1), lambda i: (0, 0)),
        ],
        out_shape=[
            jax.ShapeDtypeStruct((n_tok, d), F32),
            jax.ShapeDtypeStruct((n_tok, d), F32),
            jax.ShapeDtypeStruct((IDX_W, n_tok), I32),
            jax.ShapeDtypeStruct((IDX_W, n_tok), F32),
            jax.ShapeDtypeStruct((IDX_W, n_tok), I32),
            jax.ShapeDtypeStruct((n_exp, 1), F32),
        ],
        scratch_shapes=[pltpu.VMEM((n_exp, 1), F32)],
        compiler_params=_cparams("arbitrary"),
        name="post_mixer",
    )(z, xa, xb, mod, nw, w_out, rw_cat, rw_hi, router_b.reshape(n_exp, 1))
    return outs


def _dispatch_kernel(pstart_ref, cnt_ref, slot_ref, h2_ref, xs_ref, zbuf, sem, zsem, *, n_exp):
    def pad_fill(e, wait):
        rem = cnt_ref[e] % MOE_BLK
        pad = jnp.where(rem == 0, 0, MOE_BLK - rem)
        base = pstart_ref[e] + cnt_ref[e]
        head = jnp.minimum(pad, jnp.bitwise_and(-base, SUBLANES - 1))

        def fill(off, size, cond):
            copy = pltpu.make_async_copy(zbuf.at[pl.ds(0, size)], xs_ref.at[pl.ds(off, size)], zsem)

            @pl.when(cond)
            def _():
                copy.wait() if wait else copy.start()

        for r in range(SUBLANES - 1):
            fill(base + r, 1, r < head)
        off = base + head
        rest = pad - head
        for bit in reversed(range(SUBLANES.bit_length() - 1, MOE_BLK.bit_length() - 1)):
            size = 1 << bit
            take = (rest >> bit) & 1
            fill(pl.multiple_of(off, SUBLANES), size, take == 1)
            off = off + take * size

    @pl.when(pl.program_id(0) == 0)
    def _():
        zbuf[...] = jnp.zeros_like(zbuf)

        def fill(e, c):
            pad_fill(e, False)
            return c

        def fill_wait(e, c):
            pad_fill(e, True)
            return c

        lax.fori_loop(0, n_exp, fill, 0)
        lax.fori_loop(0, n_exp, fill_wait, 0)

    def row_copy(t8, r, k):
        s = slot_ref[0, 0, (t8 * SUBLANES + r) * TOP_K + k]
        return pltpu.make_async_copy(h2_ref.at[t8, pl.ds(r, 1)], xs_ref.at[pl.ds(s, 1)], sem)

    def issue(t8, c):
        for r in range(SUBLANES):
            for k in range(TOP_K):
                row_copy(t8, r, k).start(priority=k % 2)
        return c

    lax.fori_loop(0, TILE // SUBLANES, issue, 0)
    for k in range(TOP_K):
        pltpu.make_async_copy(xs_ref.at[pl.ds(0, TILE)], xs_ref.at[pl.ds(0, TILE)], sem).wait()


def _dispatch(pstart, cnt, slots, h2, n_slots):
    n_tok, d = h2.shape
    n_tiles = n_tok // TILE
    grid_spec = pltpu.PrefetchScalarGridSpec(
        num_scalar_prefetch=2,
        grid=(n_tiles,),
        in_specs=[
            pl.BlockSpec((1, 1, TILE * TOP_K), lambda i, ps, ct: (i, 0, 0), memory_space=pltpu.SMEM),
            pl.BlockSpec((TILE // SUBLANES, SUBLANES, d), lambda i, ps, ct: (i, 0, 0)),
        ],
        out_specs=pl.BlockSpec(memory_space=pl.ANY),
        scratch_shapes=[pltpu.VMEM((MOE_BLK // 2, d), F32), pltpu.SemaphoreType.DMA(()),
                        pltpu.SemaphoreType.DMA(())],
    )
    return pl.pallas_call(
        functools.partial(_dispatch_kernel, n_exp=pstart.shape[0]),
        grid_spec=grid_spec,
        out_shape=jax.ShapeDtypeStruct((n_slots, d), F32),
        compiler_params=_cparams("arbitrary"),
        name="moe_dispatch",
    )(pstart, cnt, slots.reshape(n_tiles, 1, TILE * TOP_K), h2.reshape(n_tok // SUBLANES, SUBLANES, d))


def _grouped_kernel(be_ref, meta_ref, xs_ref, wg_ref, wu_ref, wd_ref, ys_ref, wg_s, wu_s, wd_s):
    i = pl.program_id(0)
    used = i < meta_ref[0]
    new_expert = jnp.logical_or(i == 0, be_ref[i] != be_ref[jnp.maximum(i - 1, 0)])

    @pl.when(jnp.logical_and(used, new_expert))
    def _():
        wg_s[...] = wg_ref[0, 0].astype(BF16)
        wu_s[...] = wu_ref[0, 0].astype(BF16)
        wd_s[...] = wd_ref[0, 0].astype(BF16)

    @pl.when(used)
    def _():
        xb = xs_ref[...].astype(BF16)
        a = _silu(_dot(xb, wg_s[...])) * _dot(xb, wu_s[...])
        ys_ref[...] = _dot(a.astype(BF16), wd_s[...])

    @pl.when(jnp.logical_not(used))
    def _():
        ys_ref[...] = jnp.zeros_like(ys_ref)


def _grouped(blk_expert, n_used, xs, w_gate, w_up, w_down, layer):
    n_slots, d = xs.shape
    n_blk = n_slots // MOE_BLK
    f = w_gate.shape[3]

    def row_map(i, be, meta):
        return (jnp.minimum(i, jnp.maximum(meta[0] - 1, 0)), 0)

    grid_spec = pltpu.PrefetchScalarGridSpec(
        num_scalar_prefetch=2,
        grid=(n_blk,),
        in_specs=[
            pl.BlockSpec((MOE_BLK, d), row_map),
            pl.BlockSpec((1, 1, d, f), lambda i, be, meta: (layer, be[i], 0, 0)),
            pl.BlockSpec((1, 1, d, f), lambda i, be, meta: (layer, be[i], 0, 0)),
            pl.BlockSpec((1, 1, f, d), lambda i, be, meta: (layer, be[i], 0, 0)),
        ],
        out_specs=pl.BlockSpec((MOE_BLK, d), lambda i, be, meta: (i, 0)),
        scratch_shapes=[pltpu.VMEM((d, f), BF16), pltpu.VMEM((d, f), BF16), pltpu.VMEM((f, d), BF16)],
    )
    return pl.pallas_call(
        _grouped_kernel,
        grid_spec=grid_spec,
        out_shape=jax.ShapeDtypeStruct((n_slots, d), F32),
        compiler_params=_cparams("arbitrary"),
        name="moe_experts",
    )(blk_expert, n_used, xs, w_gate, w_up, w_down)


def _combine_kernel(slot_ref, gate_ref, h2_ref, xn_ref, mod_ref, sg_ref, su_ref, sd_ref, fw_ref, ys_ref,
                    o_ref, gbuf, sem, *, d, final):
    def row_copy(t8, r, k):
        s = slot_ref[0, 0, (t8 * SUBLANES + r) * TOP_K + k]
        return pltpu.make_async_copy(ys_ref.at[pl.ds(s, 1)], gbuf.at[k, t8, pl.ds(r, 1)], sem)

    def issue(t8, c):
        for r in range(SUBLANES):
            for k in range(TOP_K):
                row_copy(t8, r, k).start(priority=k % 2)
        return c

    lax.fori_loop(0, TILE // SUBLANES, issue, 0)
    hb = h2_ref[...].astype(BF16)
    a = _silu(_dot(hb, sg_ref[...])) * _dot(hb, su_ref[...])
    acc = _dot(a.astype(BF16), sd_ref[...])
    for k in range(TOP_K):
        pltpu.make_async_copy(ys_ref.at[pl.ds(0, TILE)], ys_ref.at[pl.ds(0, TILE)], sem).wait()
    gate = gate_ref[...]
    for k in range(TOP_K):
        acc = acc + gate[:, k:k + 1] * gbuf[k].reshape(TILE, d)
    x2 = xn_ref[...] + mod_ref[0][:, 5 * d:6 * d] * acc
    if final:
        x2 = x2 * lax.rsqrt(jnp.mean(x2 * x2, axis=-1, keepdims=True) + EPS) * fw_ref[...]
    o_ref[...] = x2


def _combine(slots, gates, h2, xn, mod, sh_gate, sh_up, sh_down, fw, ys, *, mod_row_map, final):
    n_tok, d = h2.shape
    n_tiles = n_tok // TILE
    f = sh_gate.shape[1]
    return pl.pallas_call(
        functools.partial(_combine_kernel, d=d, final=final),
        grid=(n_tiles,),
        in_specs=[
            pl.BlockSpec((1, 1, TILE * TOP_K), lambda i: (i, 0, 0), memory_space=pltpu.SMEM),
            pl.BlockSpec((TILE, IDX_W), lambda i: (i, 0)),
            pl.BlockSpec((TILE, d), lambda i: (i, 0)),
            pl.BlockSpec((TILE, d), lambda i: (i, 0)),
            pl.BlockSpec((1, 1, ADA_CHUNKS * d), lambda i: (mod_row_map(i), 0, 0)),
            pl.BlockSpec((d, f), lambda i: (0, 0)),
            pl.BlockSpec((d, f), lambda i: (0, 0)),
            pl.BlockSpec((f, d), lambda i: (0, 0)),
            pl.BlockSpec((1, d), lambda i: (0, 0)),
            pl.BlockSpec(memory_space=pl.ANY),
        ],
        out_specs=pl.BlockSpec((TILE, d), lambda i: (i, 0)),
        out_shape=jax.ShapeDtypeStruct((n_tok, d), F32),
        scratch_shapes=[pltpu.VMEM((TOP_K, TILE // SUBLANES, SUBLANES, d), F32), pltpu.SemaphoreType.DMA(())],
        compiler_params=_cparams("arbitrary"),
        name="moe_combine",
    )(slots.reshape(n_tiles, 1, TILE * TOP_K), gates, h2, xn, mod, sh_gate, sh_up, sh_down, fw, ys)


def _moe(h2, xn, eidx, gates, rank, counts, mod, w_gate, w_up, w_down, sh_gate, sh_up, sh_down, fw,
         *, layer, mod_row_map, final):
    n_tok = h2.shape[0]
    n_exp = w_gate.shape[1]
    n_blk = (n_tok * TOP_K + n_exp * (MOE_BLK - 1) + MOE_BLK - 1) // MOE_BLK
    cnt = counts[:, 0].astype(I32)
    padded = (cnt + MOE_BLK - 1) // MOE_BLK * MOE_BLK
    pend = jnp.cumsum(padded)
    pstart = pend - padded
    expert_ids = jnp.arange(n_exp, dtype=I32)
    eidx, rank, gates = eidx[:TOP_K].T, rank[:TOP_K].T, gates.T
    slots = jnp.sum(jnp.where(eidx[:, :, None] == expert_ids, pstart, 0), axis=-1) + rank
    n_used = pend[-1] // MOE_BLK
    blk = jnp.arange(n_blk, dtype=I32)
    be = jnp.sum((pend[None, :] <= blk[:, None] * MOE_BLK).astype(I32), axis=1)
    last_used = jnp.sum(jnp.where(blk == n_used - 1, be, 0))
    be = jnp.minimum(jnp.where(blk < n_used, be, last_used), n_exp - 1)
    xs = _dispatch(pstart, cnt, slots, h2, n_blk * MOE_BLK)
    ys = _grouped(be, n_used.reshape(1).astype(I32), xs, w_gate, w_up, w_down, layer)
    return _combine(slots, gates, h2, xn, mod, sh_gate, sh_up, sh_down, fw, ys,
                    mod_row_map=mod_row_map, final=final)


def _log_sigmoid(x):
    return jnp.minimum(x, 0.0) - jnp.log1p(jnp.exp(-jnp.abs(x)))


def _mlstm_in_kernel(x_ref, mod_ref, nw_ref, w_ref, wgcat_ref, wghi_ref, gb_ref, p_ref, st_ref, r_ref,
                     *, d, cn, qk_w, n_head, k_scale):
    n_s = 2 * n_head
    mod = mod_ref[0]
    hf = _norm_mod(x_ref[...], nw_ref[...], mod[:, 0:d], mod[:, d:2 * d])
    h = hf.astype(BF16)
    for j in range(3 * d // cn):
        c0 = j * cn
        p = _dot(h, w_ref[:, c0:c0 + cn])
        if qk_w <= c0 < 2 * qk_w:
            p = p * k_scale
        p_ref[:, c0:c0 + cn] = p.astype(BF16)
    h_lo = (hf - h.astype(F32)).astype(BF16)
    p_hi = _dot(h, wgcat_ref[...])
    g = p_hi[:, :2 * n_s] + (p_hi[:, 2 * n_s:] + _dot(h_lo, wghi_ref[...])) + gb_ref[...]
    g = GATE_CAP * jnp.tanh(g / GATE_CAP)
    li = g[:, :n_s]
    lf = _log_sigmoid(g[:, n_s:])
    t_i = lax.broadcasted_iota(I32, (TILE, TILE), 0)
    u_i = lax.broadcasted_iota(I32, (TILE, TILE), 1)
    same = (t_i // CHUNK) == (u_i // CHUNK)
    one = lambda m: jnp.where(m, 1.0, 0.0).astype(BF16)
    m_all, m_pre, m_suf = one(same), one(same & (u_i <= t_i)), one(same & (u_i >= t_i))
    parts = _split3(lf)
    msum = lambda m: _dot(m, parts[0]) + (_dot(m, parts[1]) + _dot(m, parts[2]))
    is_fwd = lax.broadcasted_iota(I32, (TILE, n_s), 1) < n_head
    b = jnp.where(is_fwd, msum(m_pre), msum(m_suf))
    r = li - b
    pos = lax.broadcasted_iota(I32, (TILE, n_s), 0) % CHUNK
    run_pre, run_suf = r, r
    step = 1
    while step < CHUNK:
        run_pre = jnp.where(pos >= step, jnp.maximum(run_pre, pltpu.roll(run_pre, step, 0)), run_pre)
        run_suf = jnp.where(pos < CHUNK - step, jnp.maximum(run_suf, pltpu.roll(run_suf, TILE - step, 0)), run_suf)
        step *= 2
    st_ref[:, 0:n_s] = b
    st_ref[:, n_s:2 * n_s] = b + jnp.where(is_fwd, run_pre, run_suf)
    st_ref[:, 2 * n_s:3 * n_s] = msum(m_all) - b + li
    r_ref[...] = r


def _mlstm_in(xt, mod, nw, w_qkvo, w_g, gate_b, *, n_batch, tiles_per_b, n_head):
    n_tok, d = xt.shape
    n_tiles = n_tok // TILE
    qk_w = d // 2
    cn = min(512, qk_w)
    n_s = 2 * n_head
    k_scale = float((qk_w // n_head) ** -0.5)
    order = jnp.arange(4 * n_head).reshape(2, 2, n_head).transpose(1, 0, 2).reshape(-1)
    w_g = w_g[:, order]
    gate_b = gate_b[:, order]
    wg_hi = w_g.astype(BF16)
    wg_cat = jnp.concatenate([wg_hi, (w_g - wg_hi.astype(F32)).astype(BF16)], axis=1)

    def mod_map(i):
        return (jnp.where(i % tiles_per_b == 0, n_batch, i // tiles_per_b), 0, 0)

    return pl.pallas_call(
        functools.partial(_mlstm_in_kernel, d=d, cn=cn, qk_w=qk_w, n_head=n_head, k_scale=k_scale),
        grid=(n_tiles,),
        in_specs=[
            pl.BlockSpec((TILE, d), lambda i: (i, 0)),
            pl.BlockSpec((1, 1, ADA_CHUNKS * d), mod_map),
            pl.BlockSpec((1, d), lambda i: (0, 0)),
            _resident((d, 3 * d), lambda i: (0, 0)),
            pl.BlockSpec((d, 4 * n_s), lambda i: (0, 0)),
            pl.BlockSpec((d, 2 * n_s), lambda i: (0, 0)),
            pl.BlockSpec((1, 2 * n_s), lambda i: (0, 0)),
        ],
        out_specs=[
            pl.BlockSpec((TILE, 3 * d), lambda i: (i, 0)),
            pl.BlockSpec((TILE, 3 * n_s), lambda i: (i, 0)),
            pl.BlockSpec((TILE, n_s), lambda i: (i, 0)),
        ],
        out_shape=[
            jax.ShapeDtypeStruct((n_tok, 3 * d), BF16),
            jax.ShapeDtypeStruct((n_tok, 3 * n_s), F32),
            jax.ShapeDtypeStruct((n_tok, n_s), F32),
        ],
        compiler_params=_cparams("parallel"),
        name="mlstm_in",
    )(xt, mod, nw, w_qkvo, wg_cat, wg_hi, gate_b)


def _lanes(x, width):
    if width <= LANES:
        return x[:, :width]
    return jnp.concatenate([x] * (width // LANES), axis=1)


def _mlstm_chunk_open(q, k, v, b_t, top_t, g_t, r_r, c_st, n_st, m_st, *, backward):
    n_t, dqk = q.shape
    dv = v.shape[1]
    tt = lax.broadcasted_iota(I32, (n_t, n_t), 0)
    ss = lax.broadcasted_iota(I32, (n_t, n_t), 1)
    seen = (ss >= tt) if backward else (ss <= tt)
    a = b_t + m_st
    m_row = jnp.maximum(a, top_t)
    w_intra = jnp.exp(jnp.where(seen, _lanes(b_t - m_row, n_t) + r_r, -jnp.inf))
    w_inter = jnp.exp(a - m_row)
    qk = lax.dot_general(q, k, (((1,), (1,)), ((), ())), preferred_element_type=F32)
    inter = _dot(q, c_st.astype(BF16))
    qn = jnp.sum(q.astype(F32) * n_st, axis=1, keepdims=True)
    b_last = b_t[0:1] if backward else b_t[n_t - 1:n_t]
    m_new = jnp.maximum(b_last + m_st, jnp.max(g_t, axis=0, keepdims=True))
    decay = jnp.exp(b_last + m_st - m_new)
    kw = k.astype(F32) * _lanes(jnp.exp(g_t - m_new), dqk)
    c_new = _lanes(decay, dv) * c_st + lax.dot_general(kw.astype(BF16), v, (((0,), (0,)), ((), ())),
                                                     preferred_element_type=F32)
    n_new = _lanes(decay, dqk) * n_st + jnp.sum(kw, axis=0, keepdims=True)
    return (qk, w_intra, w_inter, inter, qn, m_row, v), (c_new, n_new, m_new)


def _mlstm_chunk_close(qk, w_intra, w_inter, inter, qn, m_row, v):
    s = qk * w_intra
    num = _dot(s.astype(BF16), v) + _lanes(w_inter, v.shape[1]) * inter
    den = jnp.sum(s, axis=1, keepdims=True) + w_inter[:, :1] * qn
    return num / jnp.maximum(jnp.abs(den), jnp.exp(-m_row[:, :1]))


def _mlstm_scan_kernel(q_ref, k_ref, v_ref, o_ref, st_ref, r_ref, nw_ref, z_ref, hf_ref, hb_ref, c_ref, rep_ref,
                       *, n_chunk, n_ctx_chunk, dqk, dv, n_hp):
    c_ref[...] = jnp.zeros_like(c_ref)
    n_stat = 2 * 3
    st = st_ref[...]
    lane = lax.broadcasted_iota(I32, st.shape, 1)
    n_head = st.shape[1] // n_stat
    for hp in range(n_hp):
        head = pl.program_id(1) * n_hp + hp
        for direction in range(2):
            for stat in range(3):
                col = jnp.sum(jnp.where(lane == (stat * 2 + direction) * n_head + head, st, 0.0), axis=1, keepdims=True)
                rep_ref[hp * n_stat + 3 * direction + stat] = jnp.broadcast_to(col, rep_ref.shape[1:])
    n0 = jnp.zeros((1, dqk), F32)
    m0 = jnp.full((1, LANES), M_INIT, F32)

    def step(i, carry):
        jf = i
        jb = jnp.where(i < n_ctx_chunk, n_ctx_chunk - 1 - i, n_chunk - 1 - (i - n_ctx_chunk))
        rows = (pl.ds(pl.multiple_of(jf * CHUNK, CHUNK), CHUNK), pl.ds(pl.multiple_of(jb * CHUNK, CHUNK), CHUNK))
        chunk = (jf, jb)
        scans = [(hp, direction) for hp in range(n_hp) for direction in range(2)]
        loaded = []
        for hp, direction in scans:
            r, j = rows[direction], chunk[direction]
            qc = slice(hp * dqk, (hp + 1) * dqk)
            vc = slice(hp * dv, (hp + 1) * dv)
            sc = hp * n_stat + 3 * direction
            loaded.append((q_ref[r, qc], k_ref[r, qc], v_ref[r, vc],
                           rep_ref[sc, r, :], rep_ref[sc + 1, r, :], rep_ref[sc + 2, r, :],
                           r_ref[0, hp, direction, j], c_ref[2 * hp + direction]))
        opened = [_mlstm_chunk_open(*loaded[s], *carry[s], backward=direction == 1)
                  for s, (hp, direction) in enumerate(scans)]
        for s, (hp, direction) in enumerate(scans):
            c_ref[s] = opened[s][1][0]
            (hf_ref, hb_ref)[direction][rows[direction], hp * dv:(hp + 1) * dv] = _mlstm_chunk_close(*opened[s][0])
        return tuple((n_new, m_new) for _, (_, n_new, m_new) in opened)

    lax.fori_loop(0, n_chunk, step, ((n0, m0),) * (2 * n_hp))
    n_ctx = n_ctx_chunk * CHUNK
    n_lat = (n_chunk - n_ctx_chunk) * CHUNK
    lat = pl.ds(n_ctx, n_lat)
    for hp in range(n_hp):
        vc = slice(hp * dv, (hp + 1) * dv)
        h = hf_ref[lat, vc] + hb_ref[lat, vc]
        hn = h * lax.rsqrt(jnp.mean(h * h, axis=-1, keepdims=True) + EPS)
        y = hn * nw_ref[:, vc] * _sigmoid(o_ref[lat, vc].astype(F32))
        z_ref[:, vc] = y.astype(BF16)


def _mlstm_scan(p, st, rr, norm_w, *, n_batch, n_head, seq_all, n_ctx, d):
    dqk = d // (2 * n_head)
    dv = d // n_head
    n_hp = 2 if n_head % 2 == 0 else 1
    n_chunk = seq_all // CHUNK
    n_lat = seq_all - n_ctx
    qk_blocks = (d // 2) // (n_hp * dqk)
    v_blocks = d // (n_hp * dv)
    return pl.pallas_call(
        functools.partial(_mlstm_scan_kernel, n_chunk=n_chunk, n_ctx_chunk=n_ctx // CHUNK, dqk=dqk, dv=dv,
                          n_hp=n_hp),
        grid=(n_batch, n_head // n_hp),
        in_specs=[
            pl.BlockSpec((seq_all, n_hp * dqk), lambda b, h: (b, h)),
            pl.BlockSpec((seq_all, n_hp * dqk), lambda b, h: (b, qk_blocks + h)),
            pl.BlockSpec((seq_all, n_hp * dv), lambda b, h: (b, v_blocks + h)),
            pl.BlockSpec((seq_all, n_hp * dv), lambda b, h: (b, 2 * v_blocks + h)),
            pl.BlockSpec((seq_all, 6 * n_head), lambda b, h: (b, 0)),
            pl.BlockSpec((1, n_hp, 2, n_chunk, 1, CHUNK), lambda b, h: (b, h, 0, 0, 0, 0)),
            pl.BlockSpec((1, n_hp * dv), lambda b, h: (0, h)),
        ],
        out_specs=pl.BlockSpec((n_lat, n_hp * dv), lambda b, h: (b, h)),
        out_shape=jax.ShapeDtypeStruct((n_batch * n_lat, d), BF16),
        scratch_shapes=[
            pltpu.VMEM((seq_all, n_hp * dv), F32),
            pltpu.VMEM((seq_all, n_hp * dv), F32),
            pltpu.VMEM((2 * n_hp, dqk, dv), F32),
            pltpu.VMEM((n_hp * 6, seq_all, LANES), F32),
        ],
        compiler_params=_cparams("parallel", "parallel"),
        name="mlstm_scan",
    )(p, p, p, p, st, rr, norm_w)


def kernel(x, c, ctx, c_ctx, ada_w, ada_b, norm_mix_w, norm_ffn_w, conv_in_w, conv_dw_w, conv_out_w,
           mlstm_in_w, mlstm_gate_b, mlstm_norm_w, mlstm_out_w, router_w, router_bias,
           exp_gate_w, exp_up_w, exp_down_w, shared_gate_w, shared_up_w, shared_down_w, final_norm_w):
    n_batch, seq, d = x.shape
    n_ctx = ctx.shape[1]
    assert ada_w.shape[0] == 2 and n_ctx == TILE and seq % TILE == 0 and n_batch + 1 <= ADA_ROWS
    seq_all = n_ctx + seq
    tiles_per_b = seq_all // TILE
    lat_tiles_per_b = seq // TILE
    n_head = (mlstm_in_w.shape[2] - 3 * d) // 4

    cond = jnp.zeros((ADA_ROWS, d), F32).at[:n_batch].set(c).at[n_batch].set(c_ctx)
    mod = _ada_mod(cond, ada_w, ada_b)
    mod0 = mod[0].reshape(ADA_ROWS, 1, ADA_CHUNKS * d)
    mod1 = mod[1].reshape(ADA_ROWS, 1, ADA_CHUNKS * d)

    def all_mod_row(i):
        return jnp.where(i % tiles_per_b == 0, n_batch, i // tiles_per_b)

    def lat_mod_row(i):
        return i // lat_tiles_per_b

    def lat_tile(i):
        return (i // lat_tiles_per_b) * tiles_per_b + 1 + i % lat_tiles_per_b

    row = lambda w: w.reshape(1, -1)
    bf = lambda w: w.astype(BF16)
    ctx2 = ctx.reshape(n_batch * n_ctx, d)
    x2 = x.reshape(n_batch * seq, d)

    def is_ctx_tile(i):
        return i % tiles_per_b == 0

    def ctx_tile(i):
        return i // tiles_per_b

    def x_tile(i):
        return (i // tiles_per_b) * lat_tiles_per_b + jnp.maximum(i % tiles_per_b - 1, 0)

    z0 = _conv_in(ctx2, x2, mod0, row(norm_mix_w[0]), bf(conv_in_w[0]), conv_dw_w[0],
                  n_batch=n_batch, tiles_per_b=tiles_per_b, ctx_map=ctx_tile, lat_map=x_tile)
    xn0, h20, eidx0, gate0, rank0, cnt0 = _post(
        z0, ctx2, x2, mod0, row(norm_ffn_w[0]), bf(conv_out_w[0]), router_w[0], row(router_bias[0]),
        a_map=ctx_tile, b_map=x_tile, pick_a=is_ctx_tile, mod_row_map=all_mod_row)
    x1 = _moe(h20, xn0, eidx0, gate0, rank0, cnt0, mod0, exp_gate_w, exp_up_w, exp_down_w,
              bf(shared_gate_w[0]), bf(shared_up_w[0]), bf(shared_down_w[0]), row(final_norm_w),
              layer=0, mod_row_map=all_mod_row, final=False)

    w_in = mlstm_in_w[0]
    p, st, r = _mlstm_in(x1, mod1, row(norm_mix_w[1]), bf(w_in[:, :3 * d]), w_in[:, 3 * d:], row(mlstm_gate_b[0]),
                     n_batch=n_batch, tiles_per_b=tiles_per_b, n_head=n_head)
    n_chunk = seq_all // CHUNK
    rr =r.reshape(n_batch, n_chunk, CHUNK, 2, n_head).transpose(0, 4, 3, 1, 2)
    rr = rr.reshape(n_batch, n_head, 2, n_chunk, 1, CHUNK)
    z1 = _mlstm_scan(p, st, rr, row(mlstm_norm_w[0]), n_batch=n_batch, n_head=n_head, seq_all=seq_all,
                     n_ctx=n_ctx, d=d)
    xn1, h21, eidx1, gate1, rank1, cnt1 = _post(
        z1, x1, x1, mod1, row(norm_ffn_w[1]), bf(mlstm_out_w[0]), router_w[1], row(router_bias[1]),
        a_map=lat_tile, b_map=lambda i: 0, pick_a=lambda i: i >= 0, mod_row_map=lat_mod_row)
    out = _moe(h21, xn1, eidx1, gate1, rank1, cnt1, mod1, exp_gate_w, exp_up_w, exp_down_w,
               bf(shared_gate_w[1]), bf(shared_up_w[1]), bf(shared_down_w[1]), row(final_norm_w),
               layer=1, mod_row_map=lat_mod_row, final=True)
    return out.reshape(n_batch, seq, d)
```

```python
import functools

import jax
import jax.numpy as jnp
from jax import lax
from jax.experimental import pallas as pl
from jax.experimental.pallas import tpu as pltpu

F32 = jnp.float32
BF16 = jnp.bfloat16
I32 = jnp.int32

TILE = 256
GRID_W = 64
CHUNK = 64
TOP_K = 6
MOE_BLK = 256
IDX_W = 8
LANES = 128
SUBLANES = 8
ADA_CHUNKS = 6
ADA_ROWS = 16
EPS = 1e-6
GATE_CAP = 15.0
M_INIT = -1e30
ROUTED_SCALE = 2.5
V7X_VMEM_LIMIT = 56 * 1024 * 1024


def _cparams(*sem):
    return pltpu.CompilerParams(dimension_semantics=sem, vmem_limit_bytes=V7X_VMEM_LIMIT)


def _resident(shape, index_map):
    return pl.BlockSpec(shape, index_map, pipeline_mode=pl.Buffered(1))


def _sigmoid(x):
    return 1.0 / (1.0 + jnp.exp(-x))


def _silu(x):
    return x * _sigmoid(x)


def _split3(a):
    hi = a.astype(BF16)
    r1 = a - hi.astype(F32)
    mid = r1.astype(BF16)
    lo = (r1 - mid.astype(F32)).astype(BF16)
    return hi, mid, lo


def _dot(a, b):
    return jnp.dot(a, b, preferred_element_type=F32)


def _norm_mod(x, w, shift, scale):
    y = x * lax.rsqrt(jnp.mean(x * x, axis=-1, keepdims=True) + EPS)
    return (y * w) * (1.0 + scale) + shift


def _ada_kernel(cond_ref, w_ref, b_ref, o_ref):
    a = _silu(cond_ref[...]).astype(BF16)
    o_ref[0] = _dot(a, w_ref[0].astype(BF16)) + b_ref[0]


def _ada_mod(cond, ada_w, ada_b):
    n_layer, d, n_out = ada_w.shape
    tn = 1024 if n_out % 1024 == 0 else n_out
    return pl.pallas_call(
        _ada_kernel,
        grid=(n_layer, n_out // tn),
        in_specs=[
            pl.BlockSpec((ADA_ROWS, d), lambda l, j: (0, 0)),
            pl.BlockSpec((1, d, tn), lambda l, j: (l, 0, j)),
            pl.BlockSpec((1, 1, tn), lambda l, j: (l, 0, j)),
        ],
        out_specs=pl.BlockSpec((1, ADA_ROWS, tn), lambda l, j: (l, 0, j)),
        out_shape=jax.ShapeDtypeStruct((n_layer, ADA_ROWS, n_out), F32),
        compiler_params=_cparams("parallel", "parallel"),
        name="ada_mod",
    )(cond, ada_w, ada_b.reshape(n_layer, 1, n_out))


def _conv_in_kernel(ctx_ref, x_ref, mod_ref, nw_ref, win_ref, wdw_ref, z_ref, *, d, cn, tiles_per_b):
    is_ctx = (pl.program_id(0) % tiles_per_b) == 0
    mod = mod_ref[0]
    xt = jnp.where(is_ctx, ctx_ref[...], x_ref[...])
    h = _norm_mod(xt, nw_ref[...], mod[:, 0:d], mod[:, d:2 * d]).astype(BF16)
    t = lax.broadcasted_iota(I32, (TILE, 1), 0)
    pos_mask = jnp.where(is_ctx, TILE - 1, GRID_W - 1)
    pos = jnp.bitwise_and(t, pos_mask)
    first = pos == 0
    last = pos == pos_mask
    for j in range(d // cn):
        c0 = j * cn
        bg = _dot(h, win_ref[:, c0:c0 + cn])
        cg = _dot(h, win_ref[:, d + c0:d + c0 + cn])
        hi = _dot(h, win_ref[:, 2 * d + c0:2 * d + c0 + cn])
        u = cg * hi
        u_prev = jnp.where(first, 0.0, pltpu.roll(u, 1, 0))
        u_next = jnp.where(last, 0.0, pltpu.roll(u, TILE - 1, 0))
        w = wdw_ref[:, c0:c0 + cn]
        y = u_prev * w[0:1] + u * w[1:2] + u_next * w[2:3]
        z_ref[:, c0:c0 + cn] = (bg * y).astype(BF16)


def _conv_in(ctx2, x2, mod, nw, w_in, w_dw, *, n_batch, tiles_per_b, ctx_map, lat_map):
    d = x2.shape[1]
    n_tiles = n_batch * tiles_per_b
    n_tok = n_tiles * TILE
    cn = min(512, d)

    def mod_map(i):
        return (jnp.where(i % tiles_per_b == 0, n_batch, i // tiles_per_b), 0, 0)

    return pl.pallas_call(
        functools.partial(_conv_in_kernel, d=d, cn=cn, tiles_per_b=tiles_per_b),
        grid=(n_tiles,),
        in_specs=[
            pl.BlockSpec((TILE, d), lambda i: (ctx_map(i), 0)),
            pl.BlockSpec((TILE, d), lambda i: (lat_map(i), 0)),
            pl.BlockSpec((1, 1, ADA_CHUNKS * d), mod_map),
            pl.BlockSpec((1, d), lambda i: (0, 0)),
            _resident((d, 3 * d), lambda i: (0, 0)),
            pl.BlockSpec((3, d), lambda i: (0, 0)),
        ],
        out_specs=pl.BlockSpec((TILE, d), lambda i: (i, 0)),
        out_shape=jax.ShapeDtypeStruct((n_tok, d), BF16),
        compiler_params=_cparams("parallel"),
        name="conv_in",
    )(ctx2, x2, mod, nw, w_in, w_dw)


def _post_kernel(z_ref, xa_ref, xb_ref, mod_ref, nw_ref, wout_ref, rwcat_ref, rwhi_ref, rb_ref,
                 xn_ref, h2_ref, eidx_ref, gate_ref, rank_ref, cnt_ref, carry_ref, *, d, n_exp, pick_a):
    i = pl.program_id(0)

    @pl.when(i == 0)
    def _():
        carry_ref[...] = jnp.zeros_like(carry_ref)

    mod = mod_ref[0]
    y = _dot(z_ref[...], wout_ref[...])
    xn = jnp.where(pick_a(i), xa_ref[...], xb_ref[...]) + mod[:, 2 * d:3 * d] * y
    xn_ref[...] = xn
    h2 = _norm_mod(xn, nw_ref[...], mod[:, 3 * d:4 * d], mod[:, 4 * d:5 * d])
    h2_ref[...] = h2

    nt = (((1,), (1,)), ((), ()))
    h2_hi = h2.astype(BF16)
    h2_lo = (h2 - h2_hi.astype(F32)).astype(BF16)
    p_hi = lax.dot_general(rwcat_ref[...], h2_hi, nt, preferred_element_type=F32)
    logits = p_hi[:n_exp] + (p_hi[n_exp:] + lax.dot_general(rwhi_ref[...], h2_lo, nt, preferred_element_type=F32))
    scores = _sigmoid(logits)
    row_f = lax.broadcasted_iota(I32, (n_exp, TILE), 0).astype(F32)
    work = scores + rb_ref[...]
    onehots, picks = [], []
    for _ in range(TOP_K):
        mx = jnp.max(work, axis=0, keepdims=True)
        first_max = jnp.min(jnp.where(work == mx, row_f, float(n_exp)), axis=0, keepdims=True)
        oh = row_f == first_max
        onehots.append(oh)
        picks.append(first_max)
        work = jnp.where(oh, -jnp.inf, work)
    sel = onehots[0]
    for oh in onehots[1:]:
        sel = jnp.logical_or(sel, oh)
    picked = jnp.where(sel, scores, 0.0)
    gates = picked / jnp.sum(picked, axis=0, keepdims=True) * ROUTED_SCALE
    sel_f = jnp.where(sel, 1.0, 0.0)
    r_i = lax.broadcasted_iota(I32, (TILE, TILE), 0)
    c_i = lax.broadcasted_iota(I32, (TILE, TILE), 1)
    before = jnp.where(r_i < c_i, 1.0, 0.0).astype(BF16)
    cum = _dot(sel_f.astype(BF16), before) + carry_ref[...]

    eidx_ref[...] = jnp.zeros_like(eidx_ref)
    gate_ref[...] = jnp.zeros_like(gate_ref)
    rank_ref[...] = jnp.zeros_like(rank_ref)
    for k, oh in enumerate(onehots):
        eidx_ref[k:k + 1, :] = picks[k].astype(I32)
        gate_ref[k:k + 1, :] = jnp.sum(jnp.where(oh, gates, 0.0), axis=0, keepdims=True)
        rank_ref[k:k + 1, :] = jnp.sum(jnp.where(oh, cum, 0.0), axis=0, keepdims=True).astype(I32)

    total = carry_ref[...] + jnp.sum(sel_f, axis=1, keepdims=True)
    carry_ref[...] = total
    cnt_ref[...] = total


def _post(z, xa, xb, mod, nw, w_out, router_w, router_b, *, a_map, b_map, pick_a, mod_row_map):
    rw_t = router_w.T
    rw_hi = rw_t.astype(BF16)
    rw_lo = (rw_t - rw_hi.astype(F32)).astype(BF16)
    rw_cat = jnp.concatenate([rw_hi, rw_lo], axis=0)
    n_tok, d = z.shape
    n_tiles = n_tok // TILE
    n_exp = router_w.shape[1]
    outs = pl.pallas_call(
        functools.partial(_post_kernel, d=d, n_exp=n_exp, pick_a=pick_a),
        grid=(n_tiles,),
        in_specs=[
            pl.BlockSpec((TILE, d), lambda i: (i, 0)),
            pl.BlockSpec((TILE, d), lambda i: (a_map(i), 0)),
            pl.BlockSpec((TILE, d), lambda i: (b_map(i), 0)),
            pl.BlockSpec((1, 1, ADA_CHUNKS * d), lambda i: (mod_row_map(i), 0, 0)),
            pl.BlockSpec((1, d), lambda i: (0, 0)),
            _resident((d, d), lambda i: (0, 0)),
            pl.BlockSpec((2 * n_exp, d), lambda i: (0, 0)),
            pl.BlockSpec((n_exp, d), lambda i: (0, 0)),
            pl.BlockSpec((n_exp, 1), lambda i: (0, 0)),
        ],
        out_specs=[
            pl.BlockSpec((TILE, d), lambda i: (i, 0)),
            pl.BlockSpec((TILE, d), lambda i: (i, 0)),
            pl.BlockSpec((IDX_W, TILE), lambda i: (0, i)),
            pl.BlockSpec((IDX_W, TILE), lambda i: (0, i)),
            pl.BlockSpec((IDX_W, TILE), lambda i: (0, i)),
            pl.BlockSpec((n_exp, 1), lambda i: (0, 0)),
        ],
        out_shape=[
            jax.ShapeDtypeStruct((n_tok, d), F32),
            jax.ShapeDtypeStruct((n_tok, d), F32),
            jax.ShapeDtypeStruct((IDX_W, n_tok), I32),
            jax.ShapeDtypeStruct((IDX_W, n_tok), F32),
            jax.ShapeDtypeStruct((IDX_W, n_tok), I32),
            jax.ShapeDtypeStruct((n_exp, 1), F32),
        ],
        scratch_shapes=[pltpu.VMEM((n_exp, 1), F32)],
        compiler_params=_cparams("arbitrary"),
        name="post_mixer",
    )(z, xa, xb, mod, nw, w_out, rw_cat, rw_hi, router_b.reshape(n_exp, 1))
    return outs


def _dispatch_kernel(pstart_ref, cnt_ref, slot_ref, h2_ref, xs_ref, zbuf, sem, zsem, *, n_exp):
    def pad_fill(e, wait):
        rem = cnt_ref[e] % MOE_BLK
        pad = jnp.where(rem == 0, 0, MOE_BLK - rem)
        base = pstart_ref[e] + cnt_ref[e]
        head = jnp.minimum(pad, jnp.bitwise_and(-base, SUBLANES - 1))

        def fill(off, size, cond):
            copy = pltpu.make_async_copy(zbuf.at[pl.ds(0, size)], xs_ref.at[pl.ds(off, size)], zsem)

            @pl.when(cond)
            def _():
                copy.wait() if wait else copy.start()

        for r in range(SUBLANES - 1):
            fill(base + r, 1, r < head)
        off = base + head
        rest = pad - head
        for bit in reversed(range(SUBLANES.bit_length() - 1, MOE_BLK.bit_length() - 1)):
            size = 1 << bit
            take = (rest >> bit) & 1
            fill(pl.multiple_of(off, SUBLANES), size, take == 1)
            off = off + take * size

    @pl.when(pl.program_id(0) == 0)
    def _():
        zbuf[...] = jnp.zeros_like(zbuf)

        def fill(e, c):
            pad_fill(e, False)
            return c

        def fill_wait(e, c):
            pad_fill(e, True)
            return c

        lax.fori_loop(0, n_exp, fill, 0)
        lax.fori_loop(0, n_exp, fill_wait, 0)

    def row_copy(t8, r, k):
        s = slot_ref[0, 0, (t8 * SUBLANES + r) * TOP_K + k]
        return pltpu.make_async_copy(h2_ref.at[t8, pl.ds(r, 1)], xs_ref.at[pl.ds(s, 1)], sem)

    def issue(t8, c):
        for r in range(SUBLANES):
            for k in range(TOP_K):
                row_copy(t8, r, k).start(priority=k % 2)
        return c

    lax.fori_loop(0, TILE // SUBLANES, issue, 0)
    for k in range(TOP_K):
        pltpu.make_async_copy(xs_ref.at[pl.ds(0, TILE)], xs_ref.at[pl.ds(0, TILE)], sem).wait()


def _dispatch(pstart, cnt, slots, h2, n_slots):
    n_tok, d = h2.shape
    n_tiles = n_tok // TILE
    grid_spec = pltpu.PrefetchScalarGridSpec(
        num_scalar_prefetch=2,
        grid=(n_tiles,),
        in_specs=[
            pl.BlockSpec((1, 1, TILE * TOP_K), lambda i, ps, ct: (i, 0, 0), memory_space=pltpu.SMEM),
            pl.BlockSpec((TILE // SUBLANES, SUBLANES, d), lambda i, ps, ct: (i, 0, 0)),
        ],
        out_specs=pl.BlockSpec(memory_space=pl.ANY),
        scratch_shapes=[pltpu.VMEM((MOE_BLK // 2, d), F32), pltpu.SemaphoreType.DMA(()),
                        pltpu.SemaphoreType.DMA(())],
    )
    return pl.pallas_call(
        functools.partial(_dispatch_kernel, n_exp=pstart.shape[0]),
        grid_spec=grid_spec,
        out_shape=jax.ShapeDtypeStruct((n_slots, d), F32),
        compiler_params=_cparams("arbitrary"),
        name="moe_dispatch",
    )(pstart, cnt, slots.reshape(n_tiles, 1, TILE * TOP_K), h2.reshape(n_tok // SUBLANES, SUBLANES, d))


def _grouped_kernel(be_ref, nxt_ref, meta_ref, xs_ref, wg_hbm, wu_hbm, wd_hbm, ys_ref,
                    wg_f, wu_f, wd_f, wg_s, wu_s, wd_s, wsem, *, layer):
    i = pl.program_id(0)
    used = i < meta_ref[0]
    expert = be_ref[i]
    new_expert = jnp.logical_or(i == 0, expert != be_ref[jnp.maximum(i - 1, 0)])

    def fetch(e):
        return (pltpu.make_async_copy(wg_hbm.at[layer, e], wg_f, wsem.at[0]),
                pltpu.make_async_copy(wu_hbm.at[layer, e], wu_f, wsem.at[1]),
                pltpu.make_async_copy(wd_hbm.at[layer, e], wd_f, wsem.at[2]))

    @pl.when(i == 0)
    def _():
        for copy in fetch(expert):
            copy.start()

    @pl.when(jnp.logical_and(used, new_expert))
    def _():
        for copy in fetch(expert):
            copy.wait()
        wg_s[...] = wg_f[...].astype(BF16)
        wu_s[...] = wu_f[...].astype(BF16)
        wd_s[...] = wd_f[...].astype(BF16)

        @pl.when(nxt_ref[i] != expert)
        def _():
            for copy in fetch(nxt_ref[i]):
                copy.start()

    @pl.when(used)
    def _():
        xb = xs_ref[...].astype(BF16)
        a = _silu(_dot(xb, wg_s[...])) * _dot(xb, wu_s[...])
        ys_ref[...] = _dot(a.astype(BF16), wd_s[...])

    @pl.when(jnp.logical_not(used))
    def _():
        ys_ref[...] = jnp.zeros_like(ys_ref)


def _grouped(blk_expert, blk_next, n_used, xs, w_gate, w_up, w_down, layer):
    n_slots, d = xs.shape
    n_blk = n_slots // MOE_BLK
    f = w_gate.shape[3]

    def row_map(i, be, nxt, meta):
        return (jnp.minimum(i, jnp.maximum(meta[0] - 1, 0)), 0)

    grid_spec = pltpu.PrefetchScalarGridSpec(
        num_scalar_prefetch=3,
        grid=(n_blk,),
        in_specs=[
            pl.BlockSpec((MOE_BLK, d), row_map),
            pl.BlockSpec(memory_space=pl.ANY),
            pl.BlockSpec(memory_space=pl.ANY),
            pl.BlockSpec(memory_space=pl.ANY),
        ],
        out_specs=pl.BlockSpec((MOE_BLK, d), lambda i, be, nxt, meta: (i, 0)),
        scratch_shapes=[pltpu.VMEM((d, f), F32), pltpu.VMEM((d, f), F32), pltpu.VMEM((f, d), F32),
                        pltpu.VMEM((d, f), BF16), pltpu.VMEM((d, f), BF16), pltpu.VMEM((f, d), BF16),
                        pltpu.SemaphoreType.DMA((3,))],
    )
    return pl.pallas_call(
        functools.partial(_grouped_kernel, layer=layer),
        grid_spec=grid_spec,
        out_shape=jax.ShapeDtypeStruct((n_slots, d), F32),
        compiler_params=_cparams("arbitrary"),
        name="moe_experts",
    )(blk_expert, blk_next, n_used, xs, w_gate, w_up, w_down)


def _combine_kernel(slot_ref, gate_ref, h2_ref, xn_ref, mod_ref, sg_ref, su_ref, sd_ref, fw_ref, ys_ref,
                    o_ref, gbuf, sem, *, d, final):
    def row_copy(t8, r, k):
        s = slot_ref[0, 0, (t8 * SUBLANES + r) * TOP_K + k]
        return pltpu.make_async_copy(ys_ref.at[pl.ds(s, 1)], gbuf.at[k, t8, pl.ds(r, 1)], sem)

    def issue(t8, c):
        for r in range(SUBLANES):
            for k in range(TOP_K):
                row_copy(t8, r, k).start(priority=k % 2)
        return c

    lax.fori_loop(0, TILE // SUBLANES, issue, 0)
    hb = h2_ref[...].astype(BF16)
    a = _silu(_dot(hb, sg_ref[...])) * _dot(hb, su_ref[...])
    acc = _dot(a.astype(BF16), sd_ref[...])
    for k in range(TOP_K):
        pltpu.make_async_copy(ys_ref.at[pl.ds(0, TILE)], ys_ref.at[pl.ds(0, TILE)], sem).wait()
    gate = gate_ref[...]
    for k in range(TOP_K):
        acc = acc + gate[:, k:k + 1] * gbuf[k].reshape(TILE, d)
    x2 = xn_ref[...] + mod_ref[0][:, 5 * d:6 * d] * acc
    if final:
        x2 = x2 * lax.rsqrt(jnp.mean(x2 * x2, axis=-1, keepdims=True) + EPS) * fw_ref[...]
    o_ref[...] = x2


def _combine(slots, gates, h2, xn, mod, sh_gate, sh_up, sh_down, fw, ys, *, mod_row_map, final):
    n_tok, d = h2.shape
    n_tiles = n_tok // TILE
    f = sh_gate.shape[1]
    return pl.pallas_call(
        functools.partial(_combine_kernel, d=d, final=final),
        grid=(n_tiles,),
        in_specs=[
            pl.BlockSpec((1, 1, TILE * TOP_K), lambda i: (i, 0, 0), memory_space=pltpu.SMEM),
            pl.BlockSpec((TILE, IDX_W), lambda i: (i, 0)),
            pl.BlockSpec((TILE, d), lambda i: (i, 0)),
            pl.BlockSpec((TILE, d), lambda i: (i, 0)),
            pl.BlockSpec((1, 1, ADA_CHUNKS * d), lambda i: (mod_row_map(i), 0, 0)),
            pl.BlockSpec((d, f), lambda i: (0, 0)),
            pl.BlockSpec((d, f), lambda i: (0, 0)),
            pl.BlockSpec((f, d), lambda i: (0, 0)),
            pl.BlockSpec((1, d), lambda i: (0, 0)),
            pl.BlockSpec(memory_space=pl.ANY),
        ],
        out_specs=pl.BlockSpec((TILE, d), lambda i: (i, 0)),
        out_shape=jax.ShapeDtypeStruct((n_tok, d), F32),
        scratch_shapes=[pltpu.VMEM((TOP_K, TILE // SUBLANES, SUBLANES, d), F32), pltpu.SemaphoreType.DMA(())],
        compiler_params=_cparams("arbitrary"),
        name="moe_combine",
    )(slots.reshape(n_tiles, 1, TILE * TOP_K), gates, h2, xn, mod, sh_gate, sh_up, sh_down, fw, ys)


def _moe(h2, xn, eidx, gates, rank, counts, mod, w_gate, w_up, w_down, sh_gate, sh_up, sh_down, fw,
         *, layer, mod_row_map, final):
    n_tok = h2.shape[0]
    n_exp = w_gate.shape[1]
    n_blk = (n_tok * TOP_K + n_exp * (MOE_BLK - 1) + MOE_BLK - 1) // MOE_BLK
    cnt = counts[:, 0].astype(I32)
    padded = (cnt + MOE_BLK - 1) // MOE_BLK * MOE_BLK
    pend = jnp.cumsum(padded)
    pstart = pend - padded
    expert_ids = jnp.arange(n_exp, dtype=I32)
    eidx, rank, gates = eidx[:TOP_K].T, rank[:TOP_K].T, gates.T
    slots = jnp.sum(jnp.where(eidx[:, :, None] == expert_ids, pstart, 0), axis=-1) + rank
    n_used = pend[-1] // MOE_BLK
    blk = jnp.arange(n_blk, dtype=I32)
    be = jnp.sum((pend[None, :] <= blk[:, None] * MOE_BLK).astype(I32), axis=1)
    last_used = jnp.sum(jnp.where(blk == n_used - 1, be, 0))
    be = jnp.minimum(jnp.where(blk < n_used, be, last_used), n_exp - 1)
    xs = _dispatch(pstart, cnt, slots, h2, n_blk * MOE_BLK)
    later = jnp.where((expert_ids[None, :] > expert_ids[:, None]) & (padded[None, :] > 0), expert_ids[None, :], n_exp)
    next_expert = jnp.min(later, axis=1)
    next_expert = jnp.where(next_expert == n_exp, expert_ids, next_expert)
    ys = _grouped(be, next_expert[be], n_used.reshape(1).astype(I32), xs, w_gate, w_up, w_down, layer)
    return _combine(slots, gates, h2, xn, mod, sh_gate, sh_up, sh_down, fw, ys,
                    mod_row_map=mod_row_map, final=final)


def _log_sigmoid(x):
    return jnp.minimum(x, 0.0) - jnp.log1p(jnp.exp(-jnp.abs(x)))


def _mlstm_in_kernel(x_ref, mod_ref, nw_ref, w_ref, wgcat_ref, wghi_ref, gb_ref, p_ref, st_ref, r_ref,
                     *, d, cn, qk_w, n_head, k_scale):
    n_s = 2 * n_head
    mod = mod_ref[0]
    hf = _norm_mod(x_ref[...], nw_ref[...], mod[:, 0:d], mod[:, d:2 * d])
    h = hf.astype(BF16)
    for j in range(3 * d // cn):
        c0 = j * cn
        p = _dot(h, w_ref[:, c0:c0 + cn])
        if qk_w <= c0 < 2 * qk_w:
            p = p * k_scale
        p_ref[:, c0:c0 + cn] = p.astype(BF16)
    h_lo = (hf - h.astype(F32)).astype(BF16)
    p_hi = _dot(h, wgcat_ref[...])
    g = p_hi[:, :2 * n_s] + (p_hi[:, 2 * n_s:] + _dot(h_lo, wghi_ref[...])) + gb_ref[...]
    g = GATE_CAP * jnp.tanh(g / GATE_CAP)
    li = g[:, :n_s]
    lf = _log_sigmoid(g[:, n_s:])
    t_i = lax.broadcasted_iota(I32, (TILE, TILE), 0)
    u_i = lax.broadcasted_iota(I32, (TILE, TILE), 1)
    same = (t_i // CHUNK) == (u_i // CHUNK)
    one = lambda m: jnp.where(m, 1.0, 0.0).astype(BF16)
    m_all, m_pre, m_suf = one(same), one(same & (u_i <= t_i)), one(same & (u_i >= t_i))
    parts = _split3(lf)
    msum = lambda m: _dot(m, parts[0]) + (_dot(m, parts[1]) + _dot(m, parts[2]))
    is_fwd = lax.broadcasted_iota(I32, (TILE, n_s), 1) < n_head
    b = jnp.where(is_fwd, msum(m_pre), msum(m_suf))
    r = li - b
    pos = lax.broadcasted_iota(I32, (TILE, n_s), 0) % CHUNK
    run_pre, run_suf = r, r
    step = 1
    while step < CHUNK:
        run_pre = jnp.where(pos >= step, jnp.maximum(run_pre, pltpu.roll(run_pre, step, 0)), run_pre)
        run_suf = jnp.where(pos < CHUNK - step, jnp.maximum(run_suf, pltpu.roll(run_suf, TILE - step, 0)), run_suf)
        step *= 2
    st_ref[:, 0:n_s] = b
    st_ref[:, n_s:2 * n_s] = b + jnp.where(is_fwd, run_pre, run_suf)
    st_ref[:, 2 * n_s:3 * n_s] = msum(m_all) - b + li
    r_ref[...] = r


def _mlstm_in(xt, mod, nw, w_qkvo, w_g, gate_b, *, n_batch, tiles_per_b, n_head):
    n_tok, d = xt.shape
    n_tiles = n_tok // TILE
    qk_w = d // 2
    cn = min(512, qk_w)
    n_s = 2 * n_head
    k_scale = float((qk_w // n_head) ** -0.5)
    order = jnp.arange(4 * n_head).reshape(2, 2, n_head).transpose(1, 0, 2).reshape(-1)
    w_g = w_g[:, order]
    gate_b = gate_b[:, order]
    wg_hi = w_g.astype(BF16)
    wg_cat = jnp.concatenate([wg_hi, (w_g - wg_hi.astype(F32)).astype(BF16)], axis=1)

    def mod_map(i):
        return (jnp.where(i % tiles_per_b == 0, n_batch, i // tiles_per_b), 0, 0)

    return pl.pallas_call(
        functools.partial(_mlstm_in_kernel, d=d, cn=cn, qk_w=qk_w, n_head=n_head, k_scale=k_scale),
        grid=(n_tiles,),
        in_specs=[
            pl.BlockSpec((TILE, d), lambda i: (i, 0)),
            pl.BlockSpec((1, 1, ADA_CHUNKS * d), mod_map),
            pl.BlockSpec((1, d), lambda i: (0, 0)),
            _resident(w_qkvo.shape, lambda i: (0, 0)),
            pl.BlockSpec((d, 4 * n_s), lambda i: (0, 0)),
            pl.BlockSpec((d, 2 * n_s), lambda i: (0, 0)),
            pl.BlockSpec((1, 2 * n_s), lambda i: (0, 0)),
        ],
        out_specs=[
            pl.BlockSpec((TILE, 3 * d), lambda i: (i, 0)),
            pl.BlockSpec((TILE, 3 * n_s), lambda i: (i, 0)),
            pl.BlockSpec((TILE, n_s), lambda i: (i, 0)),
        ],
        out_shape=[
            jax.ShapeDtypeStruct((n_tok, 3 * d), BF16),
            jax.ShapeDtypeStruct((n_tok, 3 * n_s), F32),
            jax.ShapeDtypeStruct((n_tok, n_s), F32),
        ],
        compiler_params=_cparams("parallel"),
        name="mlstm_in",
    )(xt, mod, nw, w_qkvo, wg_cat, wg_hi, gate_b)


def _lanes(x, width):
    if width <= LANES:
        return x[:, :width]
    return jnp.concatenate([x] * (width // LANES), axis=1)


def _mlstm_chunk_open(q, k, v, b_t, top_t, g_t, r_r, c_st, n_st, m_st, *, backward):
    n_t, dqk = q.shape
    dv = v.shape[1]
    tt = lax.broadcasted_iota(I32, (n_t, n_t), 0)
    ss = lax.broadcasted_iota(I32, (n_t, n_t), 1)
    seen = (ss >= tt) if backward else (ss <= tt)
    a = b_t + m_st
    m_row = jnp.maximum(a, top_t)
    w_intra = jnp.exp(jnp.where(seen, _lanes(b_t - m_row, n_t) + r_r, -jnp.inf))
    w_inter = jnp.exp(a - m_row)
    qk = lax.dot_general(q, k, (((1,), (1,)), ((), ())), preferred_element_type=F32)
    inter = _dot(q, c_st.astype(BF16))
    qn = jnp.sum(q.astype(F32) * n_st, axis=1, keepdims=True)
    b_last = b_t[0:1] if backward else b_t[n_t - 1:n_t]
    m_new = jnp.maximum(b_last + m_st, jnp.max(g_t, axis=0, keepdims=True))
    decay = jnp.exp(b_last + m_st - m_new)
    kw = k.astype(F32) * _lanes(jnp.exp(g_t - m_new), dqk)
    c_new = _lanes(decay, dv) * c_st + lax.dot_general(kw.astype(BF16), v, (((0,), (0,)), ((), ())),
                                                     preferred_element_type=F32)
    n_new = _lanes(decay, dqk) * n_st + jnp.sum(kw, axis=0, keepdims=True)
    return (qk, w_intra, w_inter, inter, qn, m_row, v), (c_new, n_new, m_new)


def _mlstm_chunk_close(qk, w_intra, w_inter, inter, qn, m_row, v):
    s = qk * w_intra
    num = _dot(s.astype(BF16), v) + _lanes(w_inter, v.shape[1]) * inter
    den = jnp.sum(s, axis=1, keepdims=True) + w_inter[:, :1] * qn
    return num / jnp.maximum(jnp.abs(den), jnp.exp(-m_row[:, :1]))


def _mlstm_scan_kernel(q_ref, k_ref, v_ref, o_ref, st_ref, r_ref, nw_ref, z_ref, hf_ref, hb_ref, c_ref, rep_ref,
                       *, n_chunk, n_ctx_chunk, dqk, dv, n_hp):
    c_ref[...] = jnp.zeros_like(c_ref)
    n_stat = 2 * 3
    st = st_ref[...]
    lane = lax.broadcasted_iota(I32, st.shape, 1)
    n_head = st.shape[1] // n_stat
    for hp in range(n_hp):
        head = pl.program_id(1) * n_hp + hp
        for direction in range(2):
            for stat in range(3):
                col = jnp.sum(jnp.where(lane == (stat * 2 + direction) * n_head + head, st, 0.0), axis=1, keepdims=True)
                rep_ref[hp * n_stat + 3 * direction + stat] = jnp.broadcast_to(col, rep_ref.shape[1:])
    n0 = jnp.zeros((1, dqk), F32)
    m0 = jnp.full((1, LANES), M_INIT, F32)

    def step(i, carry):
        jf = i
        jb = jnp.where(i < n_ctx_chunk, n_ctx_chunk - 1 - i, n_chunk - 1 - (i - n_ctx_chunk))
        rows = (pl.ds(pl.multiple_of(jf * CHUNK, CHUNK), CHUNK), pl.ds(pl.multiple_of(jb * CHUNK, CHUNK), CHUNK))
        chunk = (jf, jb)
        scans = [(hp, direction) for hp in range(n_hp) for direction in range(2)]
        loaded = []
        for hp, direction in scans:
            r, j = rows[direction], chunk[direction]
            qc = slice(hp * dqk, (hp + 1) * dqk)
            vc = slice(hp * dv, (hp + 1) * dv)
            sc = hp * n_stat + 3 * direction
            loaded.append((q_ref[r, qc], k_ref[r, qc], v_ref[r, vc],
                           rep_ref[sc, r, :], rep_ref[sc + 1, r, :], rep_ref[sc + 2, r, :],
                           r_ref[0, hp, direction, j], c_ref[2 * hp + direction]))
        opened = [_mlstm_chunk_open(*loaded[s], *carry[s], backward=direction == 1)
                  for s, (hp, direction) in enumerate(scans)]
        for s, (hp, direction) in enumerate(scans):
            c_ref[s] = opened[s][1][0]
            (hf_ref, hb_ref)[direction][rows[direction], hp * dv:(hp + 1) * dv] = _mlstm_chunk_close(*opened[s][0])
        return tuple((n_new, m_new) for _, (_, n_new, m_new) in opened)

    lax.fori_loop(0, n_chunk, step, ((n0, m0),) * (2 * n_hp))
    n_ctx = n_ctx_chunk * CHUNK
    n_lat = (n_chunk - n_ctx_chunk) * CHUNK
    lat = pl.ds(n_ctx, n_lat)
    for hp in range(n_hp):
        vc = slice(hp * dv, (hp + 1) * dv)
        h = hf_ref[lat, vc] + hb_ref[lat, vc]
        hn = h * lax.rsqrt(jnp.mean(h * h, axis=-1, keepdims=True) + EPS)
        y = hn * nw_ref[:, vc] * _sigmoid(o_ref[lat, vc].astype(F32))
        z_ref[:, vc] = y.astype(BF16)


def _mlstm_scan(p, st, rr, norm_w, *, n_batch, n_head, seq_all, n_ctx, d):
    dqk = d // (2 * n_head)
    dv = d // n_head
    n_hp = 2 if n_head % 2 == 0 else 1
    n_chunk = seq_all // CHUNK
    n_lat = seq_all - n_ctx
    qk_blocks = (d // 2) // (n_hp * dqk)
    v_blocks = d // (n_hp * dv)
    return pl.pallas_call(
        functools.partial(_mlstm_scan_kernel, n_chunk=n_chunk, n_ctx_chunk=n_ctx // CHUNK, dqk=dqk, dv=dv,
                          n_hp=n_hp),
        grid=(n_batch, n_head // n_hp),
        in_specs=[
            pl.BlockSpec((seq_all, n_hp * dqk), lambda b, h: (b, h)),
            pl.BlockSpec((seq_all, n_hp * dqk), lambda b, h: (b, qk_blocks + h)),
            pl.BlockSpec((seq_all, n_hp * dv), lambda b, h: (b, v_blocks + h)),
            pl.BlockSpec((seq_all, n_hp * dv), lambda b, h: (b, 2 * v_blocks + h)),
            pl.BlockSpec((seq_all, 6 * n_head), lambda b, h: (b, 0)),
            pl.BlockSpec((1, n_hp, 2, n_chunk, 1, CHUNK), lambda b, h: (b, h, 0, 0, 0, 0)),
            pl.BlockSpec((1, n_hp * dv), lambda b, h: (0, h)),
        ],
        out_specs=pl.BlockSpec((n_lat, n_hp * dv), lambda b, h: (b, h)),
        out_shape=jax.ShapeDtypeStruct((n_batch * n_lat, d), BF16),
        scratch_shapes=[
            pltpu.VMEM((seq_all, n_hp * dv), F32),
            pltpu.VMEM((seq_all, n_hp * dv), F32),
            pltpu.VMEM((2 * n_hp, dqk, dv), F32),
            pltpu.VMEM((n_hp * 6, seq_all, LANES), F32),
        ],
        compiler_params=_cparams("parallel", "parallel"),
        name="mlstm_scan",
    )(p, p, p, p, st, rr, norm_w)


def kernel(x, c, ctx, c_ctx, ada_w, ada_b, norm_mix_w, norm_ffn_w, conv_in_w, conv_dw_w, conv_out_w,
           mlstm_in_w, mlstm_gate_b, mlstm_norm_w, mlstm_out_w, router_w, router_bias,
           exp_gate_w, exp_up_w, exp_down_w, shared_gate_w, shared_up_w, shared_down_w, final_norm_w):
    n_batch, seq, d = x.shape
    n_ctx = ctx.shape[1]
    assert ada_w.shape[0] == 2 and n_ctx == TILE and seq % TILE == 0 and n_batch + 1 <= ADA_ROWS
    seq_all = n_ctx + seq
    tiles_per_b = seq_all // TILE
    lat_tiles_per_b = seq // TILE
    n_head = (mlstm_in_w.shape[2] - 3 * d) // 4

    cond = jnp.zeros((ADA_ROWS, d), F32).at[:n_batch].set(c).at[n_batch].set(c_ctx)
    mod = _ada_mod(cond, ada_w, ada_b)
    mod0 = mod[0].reshape(ADA_ROWS, 1, ADA_CHUNKS * d)
    mod1 = mod[1].reshape(ADA_ROWS, 1, ADA_CHUNKS * d)

    def all_mod_row(i):
        return jnp.where(i % tiles_per_b == 0, n_batch, i // tiles_per_b)

    def lat_mod_row(i):
        return i // lat_tiles_per_b

    def lat_tile(i):
        return (i // lat_tiles_per_b) * tiles_per_b + 1 + i % lat_tiles_per_b

    row = lambda w: w.reshape(1, -1)
    bf = lambda w: w.astype(BF16)
    ctx2 = ctx.reshape(n_batch * n_ctx, d)
    x2 = x.reshape(n_batch * seq, d)

    def is_ctx_tile(i):
        return i % tiles_per_b == 0

    def ctx_tile(i):
        return i // tiles_per_b

    def x_tile(i):
        return (i // tiles_per_b) * lat_tiles_per_b + jnp.maximum(i % tiles_per_b - 1, 0)

    z0 = _conv_in(ctx2, x2, mod0, row(norm_mix_w[0]), bf(conv_in_w[0]), conv_dw_w[0],
                  n_batch=n_batch, tiles_per_b=tiles_per_b, ctx_map=ctx_tile, lat_map=x_tile)
    xn0, h20, eidx0, gate0, rank0, cnt0 = _post(
        z0, ctx2, x2, mod0, row(norm_ffn_w[0]), bf(conv_out_w[0]), router_w[0], row(router_bias[0]),
        a_map=ctx_tile, b_map=x_tile, pick_a=is_ctx_tile, mod_row_map=all_mod_row)
    x1 = _moe(h20, xn0, eidx0, gate0, rank0, cnt0, mod0, exp_gate_w, exp_up_w, exp_down_w,
              bf(shared_gate_w[0]), bf(shared_up_w[0]), bf(shared_down_w[0]), row(final_norm_w),
              layer=0, mod_row_map=all_mod_row, final=False)

    w_in = mlstm_in_w[0]
    p, st, r = _mlstm_in(x1, mod1, row(norm_mix_w[1]), bf(w_in), w_in[:, 3 * d:], row(mlstm_gate_b[0]),
                     n_batch=n_batch, tiles_per_b=tiles_per_b, n_head=n_head)
    n_chunk = seq_all // CHUNK
    rr = r.reshape(n_batch, n_chunk, CHUNK, 2, n_head).transpose(0, 4, 3, 1, 2)
    rr = rr.reshape(n_batch, n_head, 2, n_chunk, 1, CHUNK)
    z1 = _mlstm_scan(p, st, rr, row(mlstm_norm_w[0]), n_batch=n_batch, n_head=n_head, seq_all=seq_all,
                     n_ctx=n_ctx, d=d)
    xn1, h21, eidx1, gate1, rank1, cnt1 = _post(
        z1, x1, x1, mod1, row(norm_ffn_w[1]), bf(mlstm_out_w[0]), router_w[1], row(router_bias[1]),
        a_map=lat_tile, b_map=lambda i: 0, pick_a=lambda i: i >= 0, mod_row_map=lat_mod_row)
    out = _moe(h21, xn1, eidx1, gate1, rank1, cnt1, mod1, exp_gate_w, exp_up_w, exp_down_w,
               bf(shared_gate_w[1]), bf(shared_up_w[1]), bf(shared_down_w[1]), row(final_norm_w),
               layer=1, mod_row_map=lat_mod_row, final=True)
    return out.reshape(n_batch, seq, d)
```

```python
import functools

import jax
import jax.numpy as jnp
from jax import lax
from jax.experimental import pallas as pl
from jax.experimental.pallas import tpu as pltpu

F32 = jnp.float32
BF16 = jnp.bfloat16
I32 = jnp.int32

TILE = 256
GRID_W = 64
CHUNK = 64
TOP_K = 6
MOE_BLK = 256
IDX_W = 8
LANES = 128
SUBLANES = 8
ADA_CHUNKS = 6
ADA_ROWS = 16
EPS = 1e-6
GATE_CAP = 15.0
M_INIT = -1e30
ROUTED_SCALE = 2.5
V7X_VMEM_LIMIT = 56 * 1024 * 1024


def _cparams(*sem):
    return pltpu.CompilerParams(dimension_semantics=sem, vmem_limit_bytes=V7X_VMEM_LIMIT)


def _resident(shape, index_map):
    return pl.BlockSpec(shape, index_map, pipeline_mode=pl.Buffered(1))


def _sigmoid(x):
    return 1.0 / (1.0 + jnp.exp(-x))


def _silu(x):
    return x * _sigmoid(x)


def _split3(a):
    hi = a.astype(BF16)
    r1 = a - hi.astype(F32)
    mid = r1.astype(BF16)
    lo = (r1 - mid.astype(F32)).astype(BF16)
    return hi, mid, lo


def _dot(a, b):
    return jnp.dot(a, b, preferred_element_type=F32)


def _norm_mod(x, w, shift, scale):
    y = x * lax.rsqrt(jnp.mean(x * x, axis=-1, keepdims=True) + EPS)
    return (y * w) * (1.0 + scale) + shift


def _ada_kernel(cond_ref, w_ref, b_ref, o_ref):
    a = _silu(cond_ref[...]).astype(BF16)
    o_ref[0] = _dot(a, w_ref[0].astype(BF16)) + b_ref[0]


def _ada_mod(cond, ada_w, ada_b):
    n_layer, d, n_out = ada_w.shape
    tn = 1024 if n_out % 1024 == 0 else n_out
    return pl.pallas_call(
        _ada_kernel,
        grid=(n_layer, n_out // tn),
        in_specs=[
            pl.BlockSpec((ADA_ROWS, d), lambda l, j: (0, 0)),
            pl.BlockSpec((1, d, tn), lambda l, j: (l, 0, j)),
            pl.BlockSpec((1, 1, tn), lambda l, j: (l, 0, j)),
        ],
        out_specs=pl.BlockSpec((1, ADA_ROWS, tn), lambda l, j: (l, 0, j)),
        out_shape=jax.ShapeDtypeStruct((n_layer, ADA_ROWS, n_out), F32),
        compiler_params=_cparams("parallel", "parallel"),
        name="ada_mod",
    )(cond, ada_w, ada_b.reshape(n_layer, 1, n_out))


def _conv_in_kernel(ctx_ref, x_ref, mod_ref, nw_ref, win_ref, wdw_ref, z_ref, *, d, cn, tiles_per_b):
    is_ctx = (pl.program_id(0) % tiles_per_b) == 0
    mod = mod_ref[0]
    xt = jnp.where(is_ctx, ctx_ref[...], x_ref[...])
    h = _norm_mod(xt, nw_ref[...], mod[:, 0:d], mod[:, d:2 * d]).astype(BF16)
    t = lax.broadcasted_iota(I32, (TILE, 1), 0)
    pos_mask = jnp.where(is_ctx, TILE - 1, GRID_W - 1)
    pos = jnp.bitwise_and(t, pos_mask)
    first = pos == 0
    last = pos == pos_mask
    for j in range(d // cn):
        c0 = j * cn
        bg = _dot(h, win_ref[:, c0:c0 + cn])
        cg = _dot(h, win_ref[:, d + c0:d + c0 + cn])
        hi = _dot(h, win_ref[:, 2 * d + c0:2 * d + c0 + cn])
        u = cg * hi
        u_prev = jnp.where(first, 0.0, pltpu.roll(u, 1, 0))
        u_next = jnp.where(last, 0.0, pltpu.roll(u, TILE - 1, 0))
        w = wdw_ref[:, c0:c0 + cn]
        y = u_prev * w[0:1] + u * w[1:2] + u_next * w[2:3]
        z_ref[:, c0:c0 + cn] = (bg * y).astype(BF16)


def _conv_in(ctx2, x2, mod, nw, w_in, w_dw, *, n_batch, tiles_per_b, ctx_map, lat_map):
    d = x2.shape[1]
    n_tiles = n_batch * tiles_per_b
    n_tok = n_tiles * TILE
    cn = min(512, d)

    def mod_map(i):
        return (jnp.where(i % tiles_per_b == 0, n_batch, i // tiles_per_b), 0, 0)

    return pl.pallas_call(
        functools.partial(_conv_in_kernel, d=d, cn=cn, tiles_per_b=tiles_per_b),
        grid=(n_tiles,),
        in_specs=[
            pl.BlockSpec((TILE, d), lambda i: (ctx_map(i), 0)),
            pl.BlockSpec((TILE, d), lambda i: (lat_map(i), 0)),
            pl.BlockSpec((1, 1, ADA_CHUNKS * d), mod_map),
            pl.BlockSpec((1, d), lambda i: (0, 0)),
            _resident((d, 3 * d), lambda i: (0, 0)),
            pl.BlockSpec((3, d), lambda i: (0, 0)),
        ],
        out_specs=pl.BlockSpec((TILE, d), lambda i: (i, 0)),
        out_shape=jax.ShapeDtypeStruct((n_tok, d), BF16),
        compiler_params=_cparams("parallel"),
        name="conv_in",
    )(ctx2, x2, mod, nw, w_in, w_dw)


def _post_kernel(z_ref, xa_ref, xb_ref, mod_ref, nw_ref, wout_ref, rwcat_ref, rwhi_ref, rb_ref,
                 xn_ref, h2_ref, eidx_ref, gate_ref, rank_ref, cnt_ref, carry_ref, *, d, n_exp, pick_a):
    i = pl.program_id(0)

    @pl.when(i == 0)
    def _():
        carry_ref[...] = jnp.zeros_like(carry_ref)

    mod = mod_ref[0]
    y = _dot(z_ref[...], wout_ref[...])
    xn = jnp.where(pick_a(i), xa_ref[...], xb_ref[...]) + mod[:, 2 * d:3 * d] * y
    xn_ref[...] = xn
    h2 = _norm_mod(xn, nw_ref[...], mod[:, 3 * d:4 * d], mod[:, 4 * d:5 * d])
    h2_ref[...] = h2

    nt = (((1,), (1,)), ((), ()))
    h2_hi = h2.astype(BF16)
    h2_lo = (h2 - h2_hi.astype(F32)).astype(BF16)
    p_hi = lax.dot_general(rwcat_ref[...], h2_hi, nt, preferred_element_type=F32)
    logits = p_hi[:n_exp] + (p_hi[n_exp:] + lax.dot_general(rwhi_ref[...], h2_lo, nt, preferred_element_type=F32))
    scores = _sigmoid(logits)
    row_f = lax.broadcasted_iota(I32, (n_exp, TILE), 0).astype(F32)
    work = scores + rb_ref[...]
    onehots, picks = [], []
    for _ in range(TOP_K):
        mx = jnp.max(work, axis=0, keepdims=True)
        first_max = jnp.min(jnp.where(work == mx, row_f, float(n_exp)), axis=0, keepdims=True)
        oh = row_f == first_max
        onehots.append(oh)
        picks.append(first_max)
        work = jnp.where(oh, -jnp.inf, work)
    sel = onehots[0]
    for oh in onehots[1:]:
        sel = jnp.logical_or(sel, oh)
    picked = jnp.where(sel, scores, 0.0)
    gates = picked / jnp.sum(picked, axis=0, keepdims=True) * ROUTED_SCALE
    sel_f = jnp.where(sel, 1.0, 0.0)
    r_i = lax.broadcasted_iota(I32, (TILE, TILE), 0)
    c_i = lax.broadcasted_iota(I32, (TILE, TILE), 1)
    before = jnp.where(r_i < c_i, 1.0, 0.0).astype(BF16)
    cum = _dot(sel_f.astype(BF16), before) + carry_ref[...]

    eidx_ref[...] = jnp.zeros_like(eidx_ref)
    gate_ref[...] = jnp.zeros_like(gate_ref)
    rank_ref[...] = jnp.zeros_like(rank_ref)
    for k, oh in enumerate(onehots):
        eidx_ref[k:k + 1, :] = picks[k].astype(I32)
        gate_ref[k:k + 1, :] = jnp.sum(jnp.where(oh, gates, 0.0), axis=0, keepdims=True)
        rank_ref[k:k + 1, :] = jnp.sum(jnp.where(oh, cum, 0.0), axis=0, keepdims=True).astype(I32)

    total = carry_ref[...] + jnp.sum(sel_f, axis=1, keepdims=True)
    carry_ref[...] = total
    cnt_ref[...] = total


def _post(z, xa, xb, mod, nw, w_out, router_w, router_b, *, a_map, b_map, pick_a, mod_row_map):
    rw_t = router_w.T
    rw_hi = rw_t.astype(BF16)
    rw_lo = (rw_t - rw_hi.astype(F32)).astype(BF16)
    rw_cat = jnp.concatenate([rw_hi, rw_lo], axis=0)
    n_tok, d = z.shape
    n_tiles = n_tok // TILE
    n_exp = router_w.shape[1]
    outs = pl.pallas_call(
        functools.partial(_post_kernel, d=d, n_exp=n_exp, pick_a=pick_a),
        grid=(n_tiles,),
        in_specs=[
            pl.BlockSpec((TILE, d), lambda i: (i, 0)),
            pl.BlockSpec((TILE, d), lambda i: (a_map(i), 0)),
            pl.BlockSpec((TILE, d), lambda i: (b_map(i), 0)),
            pl.BlockSpec((1, 1, ADA_CHUNKS * d), lambda i: (mod_row_map(i), 0, 0)),
            pl.BlockSpec((1, d), lambda i: (0, 0)),
            _resident((d, d), lambda i: (0, 0)),
            pl.BlockSpec((2 * n_exp, d), lambda i: (0, 0)),
            pl.BlockSpec((n_exp, d), lambda i: (0, 0)),
            pl.BlockSpec((n_exp, 1), lambda i: (0, 0)),
        ],
        out_specs=[
            pl.BlockSpec((TILE, d), lambda i: (i, 0)),
            pl.BlockSpec((TILE, d), lambda i: (i, 0)),
            pl.BlockSpec((IDX_W, TILE), lambda i: (0, i)),
            pl.BlockSpec((IDX_W, TILE), lambda i: (0, i)),
            pl.BlockSpec((IDX_W, TILE), lambda i: (0, i)),
            pl.BlockSpec((n_exp, 1), lambda i: (0, 0)),
        ],
        out_shape=[
            jax.ShapeDtypeStruct((n_tok, d), F32),
            jax.ShapeDtypeStruct((n_tok, d), F32),
            jax.ShapeDtypeStruct((IDX_W, n_tok), I32),
            jax.ShapeDtypeStruct((IDX_W, n_tok), F32),
            jax.ShapeDtypeStruct((IDX_W, n_tok), I32),
            jax.ShapeDtypeStruct((n_exp, 1), F32),
        ],
        scratch_shapes=[pltpu.VMEM((n_exp, 1), F32)],
        compiler_params=_cparams("arbitrary"),
        name="post_mixer",
    )(z, xa, xb, mod, nw, w_out, rw_cat, rw_hi, router_b.reshape(n_exp, 1))
    return outs


def _dispatch_kernel(pstart_ref, cnt_ref, slot_ref, h2_ref, xs_ref, zbuf, sem, zsem, *, n_exp):
    def pad_fill(e, wait):
        rem = cnt_ref[e] % MOE_BLK
        pad = jnp.where(rem == 0, 0, MOE_BLK - rem)
        base = pstart_ref[e] + cnt_ref[e]
        head = jnp.minimum(pad, jnp.bitwise_and(-base, SUBLANES - 1))

        def fill(off, size, cond):
            copy = pltpu.make_async_copy(zbuf.at[pl.ds(0, size)], xs_ref.at[pl.ds(off, size)], zsem)

            @pl.when(cond)
            def _():
                copy.wait() if wait else copy.start()

        for r in range(SUBLANES - 1):
            fill(base + r, 1, r < head)
        off = base + head
        rest = pad - head
        for bit in reversed(range(SUBLANES.bit_length() - 1, MOE_BLK.bit_length() - 1)):
            size = 1 << bit
            take = (rest >> bit) & 1
            fill(pl.multiple_of(off, SUBLANES), size, take == 1)
            off = off + take * size

    @pl.when(pl.program_id(0) == 0)
    def _():
        zbuf[...] = jnp.zeros_like(zbuf)

        def fill(e, c):
            pad_fill(e, False)
            return c

        def fill_wait(e, c):
            pad_fill(e, True)
            return c

        lax.fori_loop(0, n_exp, fill, 0)
        lax.fori_loop(0, n_exp, fill_wait, 0)

    def row_copy(t8, r, k):
        s = slot_ref[0, 0, (t8 * SUBLANES + r) * TOP_K + k]
        return pltpu.make_async_copy(h2_ref.at[t8, pl.ds(r, 1)], xs_ref.at[pl.ds(s, 1)], sem)

    def issue(t8, c):
        for r in range(SUBLANES):
            for k in range(TOP_K):
                row_copy(t8, r, k).start(priority=k % 2)
        return c

    lax.fori_loop(0, TILE // SUBLANES, issue, 0)
    for k in range(TOP_K):
        pltpu.make_async_copy(xs_ref.at[pl.ds(0, TILE)], xs_ref.at[pl.ds(0, TILE)], sem).wait()


def _dispatch(pstart, cnt, slots, h2, n_slots):
    n_tok, d = h2.shape
    n_tiles = n_tok // TILE
    grid_spec = pltpu.PrefetchScalarGridSpec(
        num_scalar_prefetch=2,
        grid=(n_tiles,),
        in_specs=[
            pl.BlockSpec((1, 1, TILE * TOP_K), lambda i, ps, ct: (i, 0, 0), memory_space=pltpu.SMEM),
            pl.BlockSpec((TILE // SUBLANES, SUBLANES, d), lambda i, ps, ct: (i, 0, 0)),
        ],
        out_specs=pl.BlockSpec(memory_space=pl.ANY),
        scratch_shapes=[pltpu.VMEM((MOE_BLK // 2, d), F32), pltpu.SemaphoreType.DMA(()),
                        pltpu.SemaphoreType.DMA(())],
    )
    return pl.pallas_call(
        functools.partial(_dispatch_kernel, n_exp=pstart.shape[0]),
        grid_spec=grid_spec,
        out_shape=jax.ShapeDtypeStruct((n_slots, d), F32),
        compiler_params=_cparams("arbitrary"),
        name="moe_dispatch",
    )(pstart, cnt, slots.reshape(n_tiles, 1, TILE * TOP_K), h2.reshape(n_tok // SUBLANES, SUBLANES, d))


def _grouped_kernel(be_ref, after_ref, meta_ref, xs_ref, wg_hbm, wu_hbm, wd_hbm, ys_ref,
                    wg_f, wu_f, wd_f, wg_s, wu_s, wd_s, wsem, *, layer):
    i = pl.program_id(0)
    used = i < meta_ref[0]
    expert = be_ref[i]
    new_expert = jnp.logical_or(i == 0, expert != be_ref[jnp.maximum(i - 1, 0)])

    def fetch(e):
        return (pltpu.make_async_copy(wg_hbm.at[layer, e], wg_f, wsem.at[0]),
                pltpu.make_async_copy(wu_hbm.at[layer, e], wu_f, wsem.at[1]),
                pltpu.make_async_copy(wd_hbm.at[layer, e], wd_f, wsem.at[2]))

    @pl.when(i == 0)
    def _():
        for copy in fetch(expert):
            copy.start()

    @pl.when(jnp.logical_and(used, new_expert))
    def _():
        for copy in fetch(expert):
            copy.wait()
        wg_s[...] = wg_f[...].astype(BF16)
        wu_s[...] = wu_f[...].astype(BF16)
        wd_s[...] = wd_f[...].astype(BF16)

        follower = after_ref[expert]

        @pl.when(follower < meta_ref[0])
        def _():
            for copy in fetch(be_ref[jnp.minimum(follower, meta_ref[0] - 1)]):
                copy.start()

    @pl.when(used)
    def _():
        xb = xs_ref[...].astype(BF16)
        a = _silu(_dot(xb, wg_s[...])) * _dot(xb, wu_s[...])
        ys_ref[...] = _dot(a.astype(BF16), wd_s[...])

    @pl.when(jnp.logical_not(used))
    def _():
        ys_ref[...] = jnp.zeros_like(ys_ref)


def _grouped(blk_expert, blk_after, n_used, xs, w_gate, w_up, w_down, layer):
    n_slots, d = xs.shape
    n_blk = n_slots // MOE_BLK
    f = w_gate.shape[3]

    def row_map(i, be, nxt, meta):
        return (jnp.minimum(i, jnp.maximum(meta[0] - 1, 0)), 0)

    grid_spec = pltpu.PrefetchScalarGridSpec(
        num_scalar_prefetch=3,
        grid=(n_blk,),
        in_specs=[
            pl.BlockSpec((MOE_BLK, d), row_map),
            pl.BlockSpec(memory_space=pl.ANY),
            pl.BlockSpec(memory_space=pl.ANY),
            pl.BlockSpec(memory_space=pl.ANY),
        ],
        out_specs=pl.BlockSpec((MOE_BLK, d), lambda i, be, nxt, meta: (i, 0)),
        scratch_shapes=[pltpu.VMEM((d, f), F32), pltpu.VMEM((d, f), F32), pltpu.VMEM((f, d), F32),
                        pltpu.VMEM((d, f), BF16), pltpu.VMEM((d, f), BF16), pltpu.VMEM((f, d), BF16),
                        pltpu.SemaphoreType.DMA((3,))],
    )
    return pl.pallas_call(
        functools.partial(_grouped_kernel, layer=layer),
        grid_spec=grid_spec,
        out_shape=jax.ShapeDtypeStruct((n_slots, d), F32),
        compiler_params=_cparams("arbitrary"),
        name="moe_experts",
    )(blk_expert, blk_after, n_used, xs, w_gate, w_up, w_down)


def _combine_kernel(slot_ref, gate_ref, h2_ref, xn_ref, mod_ref, sg_ref, su_ref, sd_ref, fw_ref, ys_ref,
                    o_ref, gbuf, sem, *, d, final):
    def row_copy(t8, r, k):
        s = slot_ref[0, 0, (t8 * SUBLANES + r) * TOP_K + k]
        return pltpu.make_async_copy(ys_ref.at[pl.ds(s, 1)], gbuf.at[k, t8, pl.ds(r, 1)], sem)

    def issue(t8, c):
        for r in range(SUBLANES):
            for k in range(TOP_K):
                row_copy(t8, r, k).start(priority=k % 2)
        return c

    lax.fori_loop(0, TILE // SUBLANES, issue, 0)
    hb = h2_ref[...].astype(BF16)
    a = _silu(_dot(hb, sg_ref[...])) * _dot(hb, su_ref[...])
    acc = _dot(a.astype(BF16), sd_ref[...])
    for k in range(TOP_K):
        pltpu.make_async_copy(ys_ref.at[pl.ds(0, TILE)], ys_ref.at[pl.ds(0, TILE)], sem).wait()
    gate = gate_ref[...]
    for k in range(TOP_K):
        acc = acc + gate[:, k:k + 1] * gbuf[k].reshape(TILE, d)
    x2 = xn_ref[...] + mod_ref[0][:, 5 * d:6 * d] * acc
    if final:
        x2 = x2 * lax.rsqrt(jnp.mean(x2 * x2, axis=-1, keepdims=True) + EPS) * fw_ref[...]
    o_ref[...] = x2


def _combine(slots, gates, h2, xn, mod, sh_gate, sh_up, sh_down, fw, ys, *, mod_row_map, final):
    n_tok, d = h2.shape
    n_tiles = n_tok // TILE
    f = sh_gate.shape[1]
    return pl.pallas_call(
        functools.partial(_combine_kernel, d=d, final=final),
        grid=(n_tiles,),
        in_specs=[
            pl.BlockSpec((1, 1, TILE * TOP_K), lambda i: (i, 0, 0), memory_space=pltpu.SMEM),
            pl.BlockSpec((TILE, IDX_W), lambda i: (i, 0)),
            pl.BlockSpec((TILE, d), lambda i: (i, 0)),
            pl.BlockSpec((TILE, d), lambda i: (i, 0)),
            pl.BlockSpec((1, 1, ADA_CHUNKS * d), lambda i: (mod_row_map(i), 0, 0)),
            pl.BlockSpec((d, f), lambda i: (0, 0)),
            pl.BlockSpec((d, f), lambda i: (0, 0)),
            pl.BlockSpec((f, d), lambda i: (0, 0)),
            pl.BlockSpec((1, d), lambda i: (0, 0)),
            pl.BlockSpec(memory_space=pl.ANY),
        ],
        out_specs=pl.BlockSpec((TILE, d), lambda i: (i, 0)),
        out_shape=jax.ShapeDtypeStruct((n_tok, d), F32),
        scratch_shapes=[pltpu.VMEM((TOP_K, TILE // SUBLANES, SUBLANES, d), F32), pltpu.SemaphoreType.DMA(())],
        compiler_params=_cparams("arbitrary"),
        name="moe_combine",
    )(slots.reshape(n_tiles, 1, TILE * TOP_K), gates, h2, xn, mod, sh_gate, sh_up, sh_down, fw, ys)


def _moe(h2, xn, eidx, gates, rank, counts, mod, w_gate, w_up, w_down, sh_gate, sh_up, sh_down, fw,
         *, layer, mod_row_map, final):
    n_tok = h2.shape[0]
    n_exp = w_gate.shape[1]
    n_blk = (n_tok * TOP_K + n_exp * (MOE_BLK - 1) + MOE_BLK - 1) // MOE_BLK
    cnt = counts[:, 0].astype(I32)
    padded = (cnt + MOE_BLK - 1) // MOE_BLK * MOE_BLK
    pend = jnp.cumsum(padded)
    pstart = pend - padded
    expert_ids = jnp.arange(n_exp, dtype=I32)
    eidx, rank, gates = eidx[:TOP_K].T, rank[:TOP_K].T, gates.T
    slots = jnp.sum(jnp.where(eidx[:, :, None] == expert_ids, pstart, 0), axis=-1) + rank
    n_used = pend[-1] // MOE_BLK
    blk = jnp.arange(n_blk, dtype=I32)
    be = jnp.sum((pend[None, :] <= blk[:, None] * MOE_BLK).astype(I32), axis=1)
    last_used = jnp.sum(jnp.where(blk == n_used - 1, be, 0))
    be = jnp.minimum(jnp.where(blk < n_used, be, last_used), n_exp - 1)
    xs = _dispatch(pstart, cnt, slots, h2, n_blk * MOE_BLK)
    ys = _grouped(be, pend // MOE_BLK, n_used.reshape(1).astype(I32), xs, w_gate, w_up, w_down, layer)
    return _combine(slots, gates, h2, xn, mod, sh_gate, sh_up, sh_down, fw, ys,
                    mod_row_map=mod_row_map, final=final)


def _log_sigmoid(x):
    return jnp.minimum(x, 0.0) - jnp.log1p(jnp.exp(-jnp.abs(x)))


def _mlstm_in_kernel(x_ref, mod_ref, nw_ref, w_ref, wgcat_ref, wghi_ref, gb_ref, p_ref, st_ref, r_ref,
                     *, d, cn, qk_w, n_head, k_scale):
    n_s = 2 * n_head
    mod = mod_ref[0]
    hf = _norm_mod(x_ref[...], nw_ref[...], mod[:, 0:d], mod[:, d:2 * d])
    h = hf.astype(BF16)
    for j in range(3 * d // cn):
        c0 = j * cn
        p = _dot(h, w_ref[:, c0:c0 + cn])
        if qk_w <= c0 < 2 * qk_w:
            p = p * k_scale
        p_ref[:, c0:c0 + cn] = p.astype(BF16)
    h_lo = (hf - h.astype(F32)).astype(BF16)
    p_hi = _dot(h, wgcat_ref[...])
    g = p_hi[:, :2 * n_s] + (p_hi[:, 2 * n_s:] + _dot(h_lo, wghi_ref[...])) + gb_ref[...]
    g = GATE_CAP * jnp.tanh(g / GATE_CAP)
    li = g[:, :n_s]
    lf = _log_sigmoid(g[:, n_s:])
    t_i = lax.broadcasted_iota(I32, (TILE, TILE), 0)
    u_i = lax.broadcasted_iota(I32, (TILE, TILE), 1)
    same = (t_i // CHUNK) == (u_i // CHUNK)
    one = lambda m: jnp.where(m, 1.0, 0.0).astype(BF16)
    m_all, m_pre, m_suf = one(same), one(same & (u_i <= t_i)), one(same & (u_i >= t_i))
    parts = _split3(lf)
    msum = lambda m: _dot(m, parts[0]) + (_dot(m, parts[1]) + _dot(m, parts[2]))
    is_fwd = lax.broadcasted_iota(I32, (TILE, n_s), 1) < n_head
    b = jnp.where(is_fwd, msum(m_pre), msum(m_suf))
    r = li - b
    pos = lax.broadcasted_iota(I32, (TILE, n_s), 0) % CHUNK
    run_pre, run_suf = r, r
    step = 1
    while step < CHUNK:
        run_pre = jnp.where(pos >= step, jnp.maximum(run_pre, pltpu.roll(run_pre, step, 0)), run_pre)
        run_suf = jnp.where(pos < CHUNK - step, jnp.maximum(run_suf, pltpu.roll(run_suf, TILE - step, 0)), run_suf)
        step *= 2
    st_ref[:, 0:n_s] = b
    st_ref[:, n_s:2 * n_s] = b + jnp.where(is_fwd, run_pre, run_suf)
    st_ref[:, 2 * n_s:3 * n_s] = msum(m_all) - b + li
    r_ref[...] = r


def _mlstm_in(xt, mod, nw, w_qkvo, w_g, gate_b, *, n_batch, tiles_per_b, n_head):
    n_tok, d = xt.shape
    n_tiles = n_tok // TILE
    qk_w = d // 2
    cn = min(512, qk_w)
    n_s = 2 * n_head
    k_scale = float((qk_w // n_head) ** -0.5)
    order = jnp.arange(4 * n_head).reshape(2, 2, n_head).transpose(1, 0, 2).reshape(-1)
    w_g = w_g[:, order]
    gate_b = gate_b[:, order]
    wg_hi = w_g.astype(BF16)
    wg_cat = jnp.concatenate([wg_hi, (w_g - wg_hi.astype(F32)).astype(BF16)], axis=1)

    def mod_map(i):
        return (jnp.where(i % tiles_per_b == 0, n_batch, i // tiles_per_b), 0, 0)

    return pl.pallas_call(
        functools.partial(_mlstm_in_kernel, d=d, cn=cn, qk_w=qk_w, n_head=n_head, k_scale=k_scale),
        grid=(n_tiles,),
        in_specs=[
            pl.BlockSpec((TILE, d), lambda i: (i, 0)),
            pl.BlockSpec((1, 1, ADA_CHUNKS * d), mod_map),
            pl.BlockSpec((1, d), lambda i: (0, 0)),
            _resident(w_qkvo.shape, lambda i: (0, 0)),
            pl.BlockSpec((d, 4 * n_s), lambda i: (0, 0)),
            pl.BlockSpec((d, 2 * n_s), lambda i: (0, 0)),
            pl.BlockSpec((1, 2 * n_s), lambda i: (0, 0)),
        ],
        out_specs=[
            pl.BlockSpec((TILE, 3 * d), lambda i: (i, 0)),
            pl.BlockSpec((TILE, 3 * n_s), lambda i: (i, 0)),
            pl.BlockSpec((TILE, n_s), lambda i: (i, 0)),
        ],
        out_shape=[
            jax.ShapeDtypeStruct((n_tok, 3 * d), BF16),
            jax.ShapeDtypeStruct((n_tok, 3 * n_s), F32),
            jax.ShapeDtypeStruct((n_tok, n_s), F32),
        ],
        compiler_params=_cparams("parallel"),
        name="mlstm_in",
    )(xt, mod, nw, w_qkvo, wg_cat, wg_hi, gate_b)


def _lanes(x, width):
    if width <= LANES:
        return x[:, :width]
    return jnp.concatenate([x] * (width // LANES), axis=1)


def _mlstm_chunk_open(q, k, v, b_t, top_t, g_t, r_r, c_st, n_st, m_st, *, backward):
    n_t, dqk = q.shape
    dv = v.shape[1]
    tt = lax.broadcasted_iota(I32, (n_t, n_t), 0)
    ss = lax.broadcasted_iota(I32, (n_t, n_t), 1)
    seen = (ss >= tt) if backward else (ss <= tt)
    a = b_t + m_st
    m_row = jnp.maximum(a, top_t)
    w_intra = jnp.exp(jnp.where(seen, _lanes(b_t - m_row, n_t) + r_r, -jnp.inf))
    w_inter = jnp.exp(a - m_row)
    qk = lax.dot_general(q, k, (((1,), (1,)), ((), ())), preferred_element_type=F32)
    inter = _dot(q, c_st.astype(BF16))
    qn = jnp.sum(q.astype(F32) * n_st, axis=1, keepdims=True)
    b_last = b_t[0:1] if backward else b_t[n_t - 1:n_t]
    m_new = jnp.maximum(b_last + m_st, jnp.max(g_t, axis=0, keepdims=True))
    decay = jnp.exp(b_last + m_st - m_new)
    kw = k.astype(F32) * _lanes(jnp.exp(g_t - m_new), dqk)
    c_new = _lanes(decay, dv) * c_st + lax.dot_general(kw.astype(BF16), v, (((0,), (0,)), ((), ())),
                                                     preferred_element_type=F32)
    n_new = _lanes(decay, dqk) * n_st + jnp.sum(kw, axis=0, keepdims=True)
    return (qk, w_intra, w_inter, inter, qn, m_row, v), (c_new, n_new, m_new)


def _mlstm_chunk_close(qk, w_intra, w_inter, inter, qn, m_row, v):
    s = qk * w_intra
    num = _dot(s.astype(BF16), v) + _lanes(w_inter, v.shape[1]) * inter
    den = jnp.sum(s, axis=1, keepdims=True) + w_inter[:, :1] * qn
    return num / jnp.maximum(jnp.abs(den), jnp.exp(-m_row[:, :1]))


def _mlstm_scan_kernel(q_ref, k_ref, v_ref, o_ref, st_ref, r_ref, nw_ref, z_ref, hf_ref, hb_ref, c_ref, rep_ref,
                       *, n_chunk, n_ctx_chunk, dqk, dv, n_hp):
    c_ref[...] = jnp.zeros_like(c_ref)
    n_stat = 2 * 3
    st = st_ref[...]
    lane = lax.broadcasted_iota(I32, st.shape, 1)
    n_head = st.shape[1] // n_stat
    for hp in range(n_hp):
        head = pl.program_id(1) * n_hp + hp
        for direction in range(2):
            for stat in range(3):
                col = jnp.sum(jnp.where(lane == (stat * 2 + direction) * n_head + head, st, 0.0), axis=1, keepdims=True)
                rep_ref[hp * n_stat + 3 * direction + stat] = jnp.broadcast_to(col, rep_ref.shape[1:])
    n0 = jnp.zeros((1, dqk), F32)
    m0 = jnp.full((1, LANES), M_INIT, F32)

    def step(i, carry):
        jf = i
        jb = jnp.where(i < n_ctx_chunk, n_ctx_chunk - 1 - i, n_chunk - 1 - (i - n_ctx_chunk))
        rows = (pl.ds(pl.multiple_of(jf * CHUNK, CHUNK), CHUNK), pl.ds(pl.multiple_of(jb * CHUNK, CHUNK), CHUNK))
        chunk = (jf, jb)
        scans = [(hp, direction) for hp in range(n_hp) for direction in range(2)]
        loaded = []
        for hp, direction in scans:
            r, j = rows[direction], chunk[direction]
            qc = slice(hp * dqk, (hp + 1) * dqk)
            vc = slice(hp * dv, (hp + 1) * dv)
            sc = hp * n_stat + 3 * direction
            loaded.append((q_ref[r, qc], k_ref[r, qc], v_ref[r, vc],
                           rep_ref[sc, r, :], rep_ref[sc + 1, r, :], rep_ref[sc + 2, r, :],
                           r_ref[0, hp, direction, j], c_ref[2 * hp + direction]))
        opened = [_mlstm_chunk_open(*loaded[s], *carry[s], backward=direction == 1)
                  for s, (hp, direction) in enumerate(scans)]
        for s, (hp, direction) in enumerate(scans):
            c_ref[s] = opened[s][1][0]
            (hf_ref, hb_ref)[direction][rows[direction], hp * dv:(hp + 1) * dv] = _mlstm_chunk_close(*opened[s][0])
        return tuple((n_new, m_new) for _, (_, n_new, m_new) in opened)

    lax.fori_loop(0, n_chunk, step, ((n0, m0),) * (2 * n_hp))
    n_ctx = n_ctx_chunk * CHUNK
    n_lat = (n_chunk - n_ctx_chunk) * CHUNK
    lat = pl.ds(n_ctx, n_lat)
    for hp in range(n_hp):
        vc = slice(hp * dv, (hp + 1) * dv)
        h = hf_ref[lat, vc] + hb_ref[lat, vc]
        hn = h * lax.rsqrt(jnp.mean(h * h, axis=-1, keepdims=True) + EPS)
        y = hn * nw_ref[:, vc] * _sigmoid(o_ref[lat, vc].astype(F32))
        z_ref[:, vc] = y.astype(BF16)


def _mlstm_scan(p, st, rr, norm_w, *, n_batch, n_head, seq_all, n_ctx, d):
    dqk = d // (2 * n_head)
    dv = d // n_head
    n_hp = 2 if n_head % 2 == 0 else 1
    n_chunk = seq_all // CHUNK
    n_lat = seq_all - n_ctx
    qk_blocks = (d // 2) // (n_hp * dqk)
    v_blocks = d // (n_hp * dv)
    return pl.pallas_call(
        functools.partial(_mlstm_scan_kernel, n_chunk=n_chunk, n_ctx_chunk=n_ctx // CHUNK, dqk=dqk, dv=dv,
                          n_hp=n_hp),
        grid=(n_batch, n_head // n_hp),
        in_specs=[
            pl.BlockSpec((seq_all, n_hp * dqk), lambda b, h: (b, h)),
            pl.BlockSpec((seq_all, n_hp * dqk), lambda b, h: (b, qk_blocks + h)),
            pl.BlockSpec((seq_all, n_hp * dv), lambda b, h: (b, v_blocks + h)),
            pl.BlockSpec((seq_all, n_hp * dv), lambda b, h: (b, 2 * v_blocks + h)),
            pl.BlockSpec((seq_all, 6 * n_head), lambda b, h: (b, 0)),
            pl.BlockSpec((1, n_hp, 2, n_chunk, 1, CHUNK), lambda b, h: (b, h, 0, 0, 0, 0)),
            pl.BlockSpec((1, n_hp * dv), lambda b, h: (0, h)),
        ],
        out_specs=pl.BlockSpec((n_lat, n_hp * dv), lambda b, h: (b, h)),
        out_shape=jax.ShapeDtypeStruct((n_batch * n_lat, d), BF16),
        scratch_shapes=[
            pltpu.VMEM((seq_all, n_hp * dv), F32),
            pltpu.VMEM((seq_all, n_hp * dv), F32),
            pltpu.VMEM((2 * n_hp, dqk, dv), F32),
            pltpu.VMEM((n_hp * 6, seq_all, LANES), F32),
        ],
        compiler_params=_cparams("parallel", "parallel"),
        name="mlstm_scan",
    )(p, p, p, p, st, rr, norm_w)


def kernel(x, c, ctx, c_ctx, ada_w, ada_b, norm_mix_w, norm_ffn_w, conv_in_w, conv_dw_w, conv_out_w,
           mlstm_in_w, mlstm_gate_b, mlstm_norm_w, mlstm_out_w, router_w, router_bias,
           exp_gate_w, exp_up_w, exp_down_w, shared_gate_w, shared_up_w, shared_down_w, final_norm_w):
    n_batch, seq, d = x.shape
    n_ctx = ctx.shape[1]
    assert ada_w.shape[0] == 2 and n_ctx == TILE and seq % TILE == 0 and n_batch + 1 <= ADA_ROWS
    seq_all = n_ctx + seq
    tiles_per_b = seq_all // TILE
    lat_tiles_per_b = seq // TILE
    n_head = (mlstm_in_w.shape[2] - 3 * d) // 4

    cond = jnp.zeros((ADA_ROWS, d), F32).at[:n_batch].set(c).at[n_batch].set(c_ctx)
    mod = _ada_mod(cond, ada_w, ada_b)
    mod0 = mod[0].reshape(ADA_ROWS, 1, ADA_CHUNKS * d)
    mod1 = mod[1].reshape(ADA_ROWS, 1, ADA_CHUNKS * d)

    def all_mod_row(i):
        return jnp.where(i % tiles_per_b == 0, n_batch, i // tiles_per_b)

    def lat_mod_row(i):
        return i // lat_tiles_per_b

    def lat_tile(i):
        return (i // lat_tiles_per_b) * tiles_per_b + 1 + i % lat_tiles_per_b

    row = lambda w: w.reshape(1, -1)
    bf = lambda w: w.astype(BF16)
    ctx2 = ctx.reshape(n_batch * n_ctx, d)
    x2 = x.reshape(n_batch * seq, d)

    def is_ctx_tile(i):
        return i % tiles_per_b == 0

    def ctx_tile(i):
        return i // tiles_per_b

    def x_tile(i):
        return (i // tiles_per_b) * lat_tiles_per_b + jnp.maximum(i % tiles_per_b - 1, 0)

    z0 = _conv_in(ctx2, x2, mod0, row(norm_mix_w[0]), bf(conv_in_w[0]), conv_dw_w[0],
                  n_batch=n_batch, tiles_per_b=tiles_per_b, ctx_map=ctx_tile, lat_map=x_tile)
    xn0, h20, eidx0, gate0, rank0, cnt0 = _post(
        z0, ctx2, x2, mod0, row(norm_ffn_w[0]), bf(conv_out_w[0]), router_w[0], row(router_bias[0]),
        a_map=ctx_tile, b_map=x_tile, pick_a=is_ctx_tile, mod_row_map=all_mod_row)
    x1 = _moe(h20, xn0, eidx0, gate0, rank0, cnt0, mod0, exp_gate_w, exp_up_w, exp_down_w,
              bf(shared_gate_w[0]), bf(shared_up_w[0]), bf(shared_down_w[0]), row(final_norm_w),
              layer=0, mod_row_map=all_mod_row, final=False)

    w_in = mlstm_in_w[0]
    p, st, r = _mlstm_in(x1, mod1, row(norm_mix_w[1]), bf(w_in), w_in[:, 3 * d:], row(mlstm_gate_b[0]),
                     n_batch=n_batch, tiles_per_b=tiles_per_b, n_head=n_head)
    n_chunk = seq_all // CHUNK
    rr = r.reshape(n_batch, n_chunk, CHUNK, 2, n_head).transpose(0, 4, 3, 1, 2)
    rr = rr.reshape(n_batch, n_head, 2, n_chunk, 1, CHUNK)
    z1 = _mlstm_scan(p, st, rr, row(mlstm_norm_w[0]), n_batch=n_batch, n_head=n_head, seq_all=seq_all,
                     n_ctx=n_ctx, d=d)
    xn1, h21, eidx1, gate1, rank1, cnt1 = _post(
        z1, x1, x1, mod1, row(norm_ffn_w[1]), bf(mlstm_out_w[0]), router_w[1], row(router_bias[1]),
        a_map=lat_tile, b_map=lambda i: 0, pick_a=lambda i: i >= 0, mod_row_map=lat_mod_row)
    out = _moe(h21, xn1, eidx1, gate1, rank1, cnt1, mod1, exp_gate_w, exp_up_w, exp_down_w,
               bf(shared_gate_w[1]), bf(shared_up_w[1]), bf(shared_down_w[1]), row(final_norm_w),
               layer=1, mod_row_map=lat_mod_row, final=True)
    return out.reshape(n_batch, seq, d)
```

```python
import functools

import jax
import jax.numpy as jnp
from jax import lax
from jax.experimental import pallas as pl
from jax.experimental.pallas import tpu as pltpu

F32 = jnp.float32
BF16 = jnp.bfloat16
I32 = jnp.int32

TILE = 256
GRID_W = 64
CHUNK = 128
TOP_K = 6
MOE_BLK = 256
IDX_W = 8
LANES = 128
SUBLANES = 8
ADA_CHUNKS = 6
ADA_ROWS = 16
EPS = 1e-6
GATE_CAP = 15.0
M_INIT = -1e30
ROUTED_SCALE = 2.5
V7X_VMEM_LIMIT = 56 * 1024 * 1024


def _cparams(*sem):
    return pltpu.CompilerParams(dimension_semantics=sem, vmem_limit_bytes=V7X_VMEM_LIMIT)


def _resident(shape, index_map):
    return pl.BlockSpec(shape, index_map, pipeline_mode=pl.Buffered(1))


def _sigmoid(x):
    return 1.0 / (1.0 + jnp.exp(-x))


def _silu(x):
    return x * _sigmoid(x)


def _split3(a):
    hi = a.astype(BF16)
    r1 = a - hi.astype(F32)
    mid = r1.astype(BF16)
    lo = (r1 - mid.astype(F32)).astype(BF16)
    return hi, mid, lo


def _dot(a, b):
    return jnp.dot(a, b, preferred_element_type=F32)


def _norm_mod(x, w, shift, scale):
    y = x * lax.rsqrt(jnp.mean(x * x, axis=-1, keepdims=True) + EPS)
    return (y * w) * (1.0 + scale) + shift


def _ada_kernel(cond_ref, w_ref, b_ref, o_ref):
    a = _silu(cond_ref[...]).astype(BF16)
    o_ref[0] = _dot(a, w_ref[0].astype(BF16)) + b_ref[0]


def _ada_mod(cond, ada_w, ada_b):
    n_layer, d, n_out = ada_w.shape
    tn = 1024 if n_out % 1024 == 0 else n_out
    return pl.pallas_call(
        _ada_kernel,
        grid=(n_layer, n_out // tn),
        in_specs=[
            pl.BlockSpec((ADA_ROWS, d), lambda l, j: (0, 0)),
            pl.BlockSpec((1, d, tn), lambda l, j: (l, 0, j)),
            pl.BlockSpec((1, 1, tn), lambda l, j: (l, 0, j)),
        ],
        out_specs=pl.BlockSpec((1, ADA_ROWS, tn), lambda l, j: (l, 0, j)),
        out_shape=jax.ShapeDtypeStruct((n_layer, ADA_ROWS, n_out), F32),
        compiler_params=_cparams("parallel", "parallel"),
        name="ada_mod",
    )(cond, ada_w, ada_b.reshape(n_layer, 1, n_out))


def _conv_in_kernel(ctx_ref, x_ref, mod_ref, nw_ref, win_ref, wdw_ref, z_ref, *, d, cn, tiles_per_b):
    is_ctx = (pl.program_id(0) % tiles_per_b) == 0
    mod = mod_ref[0]
    xt = jnp.where(is_ctx, ctx_ref[...], x_ref[...])
    h = _norm_mod(xt, nw_ref[...], mod[:, 0:d], mod[:, d:2 * d]).astype(BF16)
    t = lax.broadcasted_iota(I32, (TILE, 1), 0)
    pos_mask = jnp.where(is_ctx, TILE - 1, GRID_W - 1)
    pos = jnp.bitwise_and(t, pos_mask)
    first = pos == 0
    last = pos == pos_mask
    for j in range(d // cn):
        c0 = j * cn
        bg = _dot(h, win_ref[:, c0:c0 + cn])
        cg = _dot(h, win_ref[:, d + c0:d + c0 + cn])
        hi = _dot(h, win_ref[:, 2 * d + c0:2 * d + c0 + cn])
        u = cg * hi
        u_prev = jnp.where(first, 0.0, pltpu.roll(u, 1, 0))
        u_next = jnp.where(last, 0.0, pltpu.roll(u, TILE - 1, 0))
        w = wdw_ref[:, c0:c0 + cn]
        y = u_prev * w[0:1] + u * w[1:2] + u_next * w[2:3]
        z_ref[:, c0:c0 + cn] = (bg * y).astype(BF16)


def _conv_in(ctx2, x2, mod, nw, w_in, w_dw, *, n_batch, tiles_per_b, ctx_map, lat_map):
    d = x2.shape[1]
    n_tiles = n_batch * tiles_per_b
    n_tok = n_tiles * TILE
    cn = min(512, d)

    def mod_map(i):
        return (jnp.where(i % tiles_per_b == 0, n_batch, i // tiles_per_b), 0, 0)

    return pl.pallas_call(
        functools.partial(_conv_in_kernel, d=d, cn=cn, tiles_per_b=tiles_per_b),
        grid=(n_tiles,),
        in_specs=[
            pl.BlockSpec((TILE, d), lambda i: (ctx_map(i), 0)),
            pl.BlockSpec((TILE, d), lambda i: (lat_map(i), 0)),
            pl.BlockSpec((1, 1, ADA_CHUNKS * d), mod_map),
            pl.BlockSpec((1, d), lambda i: (0, 0)),
            _resident((d, 3 * d), lambda i: (0, 0)),
            pl.BlockSpec((3, d), lambda i: (0, 0)),
        ],
        out_specs=pl.BlockSpec((TILE, d), lambda i: (i, 0)),
        out_shape=jax.ShapeDtypeStruct((n_tok, d), BF16),
        compiler_params=_cparams("parallel"),
        name="conv_in",
    )(ctx2, x2, mod, nw, w_in, w_dw)


def _post_kernel(z_ref, xa_ref, xb_ref, mod_ref, nw_ref, wout_ref, rwcat_ref, rwhi_ref, rb_ref,
                 xn_ref, h2_ref, eidx_ref, gate_ref, rank_ref, cnt_ref, carry_ref, *, d, n_exp, pick_a):
    i = pl.program_id(0)

    @pl.when(i == 0)
    def _():
        carry_ref[...] = jnp.zeros_like(carry_ref)

    mod = mod_ref[0]
    y = _dot(z_ref[...], wout_ref[...])
    xn = jnp.where(pick_a(i), xa_ref[...], xb_ref[...]) + mod[:, 2 * d:3 * d] * y
    xn_ref[...] = xn
    h2 = _norm_mod(xn, nw_ref[...], mod[:, 3 * d:4 * d], mod[:, 4 * d:5 * d])
    h2_ref[...] = h2

    nt = (((1,), (1,)), ((), ()))
    h2_hi = h2.astype(BF16)
    h2_lo = (h2 - h2_hi.astype(F32)).astype(BF16)
    p_hi = lax.dot_general(rwcat_ref[...], h2_hi, nt, preferred_element_type=F32)
    logits = p_hi[:n_exp] + (p_hi[n_exp:] + lax.dot_general(rwhi_ref[...], h2_lo, nt, preferred_element_type=F32))
    scores = _sigmoid(logits)
    row_f = lax.broadcasted_iota(I32, (n_exp, TILE), 0).astype(F32)
    work = scores + rb_ref[...]
    onehots, picks = [], []
    for _ in range(TOP_K):
        mx = jnp.max(work, axis=0, keepdims=True)
        first_max = jnp.min(jnp.where(work == mx, row_f, float(n_exp)), axis=0, keepdims=True)
        oh = row_f == first_max
        onehots.append(oh)
        picks.append(first_max)
        work = jnp.where(oh, -jnp.inf, work)
    sel = onehots[0]
    for oh in onehots[1:]:
        sel = jnp.logical_or(sel, oh)
    picked = jnp.where(sel, scores, 0.0)
    gates = picked / jnp.sum(picked, axis=0, keepdims=True) * ROUTED_SCALE
    sel_f = jnp.where(sel, 1.0, 0.0)
    r_i = lax.broadcasted_iota(I32, (TILE, TILE), 0)
    c_i = lax.broadcasted_iota(I32, (TILE, TILE), 1)
    before = jnp.where(r_i < c_i, 1.0, 0.0).astype(BF16)
    cum = _dot(sel_f.astype(BF16), before) + carry_ref[...]

    eidx_ref[...] = jnp.zeros_like(eidx_ref)
    gate_ref[...] = jnp.zeros_like(gate_ref)
    rank_ref[...] = jnp.zeros_like(rank_ref)
    for k, oh in enumerate(onehots):
        eidx_ref[k:k + 1, :] = picks[k].astype(I32)
        gate_ref[k:k + 1, :] = jnp.sum(jnp.where(oh, gates, 0.0), axis=0, keepdims=True)
        rank_ref[k:k + 1, :] = jnp.sum(jnp.where(oh, cum, 0.0), axis=0, keepdims=True).astype(I32)

    total = carry_ref[...] + jnp.sum(sel_f, axis=1, keepdims=True)
    carry_ref[...] = total
    cnt_ref[...] = total


def _post(z, xa, xb, mod, nw, w_out, router_w, router_b, *, a_map, b_map, pick_a, mod_row_map):
    rw_t = router_w.T
    rw_hi = rw_t.astype(BF16)
    rw_lo = (rw_t - rw_hi.astype(F32)).astype(BF16)
    rw_cat = jnp.concatenate([rw_hi, rw_lo], axis=0)
    n_tok, d = z.shape
    n_tiles = n_tok // TILE
    n_exp = router_w.shape[1]
    outs = pl.pallas_call(
        functools.partial(_post_kernel, d=d, n_exp=n_exp, pick_a=pick_a),
        grid=(n_tiles,),
        in_specs=[
            pl.BlockSpec((TILE, d), lambda i: (i, 0)),
            pl.BlockSpec((TILE, d), lambda i: (a_map(i), 0)),
            pl.BlockSpec((TILE, d), lambda i: (b_map(i), 0)),
            pl.BlockSpec((1, 1, ADA_CHUNKS * d), lambda i: (mod_row_map(i), 0, 0)),
            pl.BlockSpec((1, d), lambda i: (0, 0)),
            _resident((d, d), lambda i: (0, 0)),
            pl.BlockSpec((2 * n_exp, d), lambda i: (0, 0)),
            pl.BlockSpec((n_exp, d), lambda i: (0, 0)),
            pl.BlockSpec((n_exp, 1), lambda i: (0, 0)),
        ],
        out_specs=[
            pl.BlockSpec((TILE, d), lambda i: (i, 0)),
            pl.BlockSpec((TILE, d), lambda i: (i, 0)),
            pl.BlockSpec((IDX_W, TILE), lambda i: (0, i)),
            pl.BlockSpec((IDX_W, TILE), lambda i: (0, i)),
            pl.BlockSpec((IDX_W, TILE), lambda i: (0, i)),
            pl.BlockSpec((n_exp, 1), lambda i: (0, 0)),
        ],
        out_shape=[
            jax.ShapeDtypeStruct((n_tok, d), F32),
            jax.ShapeDtypeStruct((n_tok, d), F32),
            jax.ShapeDtypeStruct((IDX_W, n_tok), I32),
            jax.ShapeDtypeStruct((IDX_W, n_tok), F32),
            jax.ShapeDtypeStruct((IDX_W, n_tok), I32),
            jax.ShapeDtypeStruct((n_exp, 1), F32),
        ],
        scratch_shapes=[pltpu.VMEM((n_exp, 1), F32)],
        compiler_params=_cparams("arbitrary"),
        name="post_mixer",
    )(z, xa, xb, mod, nw, w_out, rw_cat, rw_hi, router_b.reshape(n_exp, 1))
    return outs


def _dispatch_kernel(pstart_ref, cnt_ref, slot_ref, h2_ref, xs_ref, zbuf, sem, zsem, *, n_exp):
    def pad_fill(e, wait):
        rem = cnt_ref[e] % MOE_BLK
        pad = jnp.where(rem == 0, 0, MOE_BLK - rem)
        base = pstart_ref[e] + cnt_ref[e]
        head = jnp.minimum(pad, jnp.bitwise_and(-base, SUBLANES - 1))

        def fill(off, size, cond):
            copy = pltpu.make_async_copy(zbuf.at[pl.ds(0, size)], xs_ref.at[pl.ds(off, size)], zsem)

            @pl.when(cond)
            def _():
                copy.wait() if wait else copy.start()

        for r in range(SUBLANES - 1):
            fill(base + r, 1, r < head)
        off = base + head
        rest = pad - head
        for bit in reversed(range(SUBLANES.bit_length() - 1, MOE_BLK.bit_length() - 1)):
            size = 1 << bit
            take = (rest >> bit) & 1
            fill(pl.multiple_of(off, SUBLANES), size, take == 1)
            off = off + take * size

    @pl.when(pl.program_id(0) == 0)
    def _():
        zbuf[...] = jnp.zeros_like(zbuf)

        def fill(e, c):
            pad_fill(e, False)
            return c

        def fill_wait(e, c):
            pad_fill(e, True)
            return c

        lax.fori_loop(0, n_exp, fill, 0)
        lax.fori_loop(0, n_exp, fill_wait, 0)

    def row_copy(t8, r, k):
        s = slot_ref[0, 0, (t8 * SUBLANES + r) * TOP_K + k]
        return pltpu.make_async_copy(h2_ref.at[t8, pl.ds(r, 1)], xs_ref.at[pl.ds(s, 1)], sem)

    def issue(t8, c):
        for r in range(SUBLANES):
            for k in range(TOP_K):
                row_copy(t8, r, k).start(priority=k % 2)
        return c

    lax.fori_loop(0, TILE // SUBLANES, issue, 0)
    for k in range(TOP_K):
        pltpu.make_async_copy(xs_ref.at[pl.ds(0, TILE)], xs_ref.at[pl.ds(0, TILE)], sem).wait()


def _dispatch(pstart, cnt, slots, h2, n_slots):
    n_tok, d = h2.shape
    n_tiles = n_tok // TILE
    grid_spec = pltpu.PrefetchScalarGridSpec(
        num_scalar_prefetch=2,
        grid=(n_tiles,),
        in_specs=[
            pl.BlockSpec((1, 1, TILE * TOP_K), lambda i, ps, ct: (i, 0, 0), memory_space=pltpu.SMEM),
            pl.BlockSpec((TILE // SUBLANES, SUBLANES, d), lambda i, ps, ct: (i, 0, 0)),
        ],
        out_specs=pl.BlockSpec(memory_space=pl.ANY),
        scratch_shapes=[pltpu.VMEM((MOE_BLK // 2, d), F32), pltpu.SemaphoreType.DMA(()),
                        pltpu.SemaphoreType.DMA(())],
    )
    return pl.pallas_call(
        functools.partial(_dispatch_kernel, n_exp=pstart.shape[0]),
        grid_spec=grid_spec,
        out_shape=jax.ShapeDtypeStruct((n_slots, d), F32),
        compiler_params=_cparams("arbitrary"),
        name="moe_dispatch",
    )(pstart, cnt, slots.reshape(n_tiles, 1, TILE * TOP_K), h2.reshape(n_tok // SUBLANES, SUBLANES, d))


def _grouped_kernel(be_ref, after_ref, meta_ref, xs_ref, wg_hbm, wu_hbm, wd_hbm, ys_ref,
                    wg_f, wu_f, wd_f, wg_s, wu_s, wd_s, wsem, *, layer):
    i = pl.program_id(0)
    used = i < meta_ref[0]
    expert = be_ref[i]
    new_expert = jnp.logical_or(i == 0, expert != be_ref[jnp.maximum(i - 1, 0)])

    def fetch(e):
        return (pltpu.make_async_copy(wg_hbm.at[layer, e], wg_f, wsem.at[0]),
                pltpu.make_async_copy(wu_hbm.at[layer, e], wu_f, wsem.at[1]),
                pltpu.make_async_copy(wd_hbm.at[layer, e], wd_f, wsem.at[2]))

    @pl.when(i == 0)
    def _():
        for copy in fetch(expert):
            copy.start()

    @pl.when(jnp.logical_and(used, new_expert))
    def _():
        for copy in fetch(expert):
            copy.wait()
        wg_s[...] = wg_f[...].astype(BF16)
        wu_s[...] = wu_f[...].astype(BF16)
        wd_s[...] = wd_f[...].astype(BF16)

        follower = after_ref[expert]

        @pl.when(follower < meta_ref[0])
        def _():
            for copy in fetch(be_ref[jnp.minimum(follower, meta_ref[0] - 1)]):
                copy.start()

    @pl.when(used)
    def _():
        xb = xs_ref[...].astype(BF16)
        a = _silu(_dot(xb, wg_s[...])) * _dot(xb, wu_s[...])
        ys_ref[...] = _dot(a.astype(BF16), wd_s[...])

    @pl.when(jnp.logical_not(used))
    def _():
        ys_ref[...] = jnp.zeros_like(ys_ref)


def _grouped(blk_expert, blk_after, n_used, xs, w_gate, w_up, w_down, layer):
    n_slots, d = xs.shape
    n_blk = n_slots // MOE_BLK
    f = w_gate.shape[3]

    def row_map(i, be, nxt, meta):
        return (jnp.minimum(i, jnp.maximum(meta[0] - 1, 0)), 0)

    grid_spec = pltpu.PrefetchScalarGridSpec(
        num_scalar_prefetch=3,
        grid=(n_blk,),
        in_specs=[
            pl.BlockSpec((MOE_BLK, d), row_map),
            pl.BlockSpec(memory_space=pl.ANY),
            pl.BlockSpec(memory_space=pl.ANY),
            pl.BlockSpec(memory_space=pl.ANY),
        ],
        out_specs=pl.BlockSpec((MOE_BLK, d), lambda i, be, nxt, meta: (i, 0)),
        scratch_shapes=[pltpu.VMEM((d, f), F32), pltpu.VMEM((d, f), F32), pltpu.VMEM((f, d), F32),
                        pltpu.VMEM((d, f), BF16), pltpu.VMEM((d, f), BF16), pltpu.VMEM((f, d), BF16),
                        pltpu.SemaphoreType.DMA((3,))],
    )
    return pl.pallas_call(
        functools.partial(_grouped_kernel, layer=layer),
        grid_spec=grid_spec,
        out_shape=jax.ShapeDtypeStruct((n_slots, d), F32),
        compiler_params=_cparams("arbitrary"),
        name="moe_experts",
    )(blk_expert, blk_after, n_used, xs, w_gate, w_up, w_down)


def _combine_kernel(slot_ref, gate_ref, h2_ref, xn_ref, mod_ref, sg_ref, su_ref, sd_ref, fw_ref, ys_ref,
                    o_ref, gbuf, sem, *, d, final):
    def row_copy(t8, r, k):
        s = slot_ref[0, 0, (t8 * SUBLANES + r) * TOP_K + k]
        return pltpu.make_async_copy(ys_ref.at[pl.ds(s, 1)], gbuf.at[k, t8, pl.ds(r, 1)], sem)

    def issue(t8, c):
        for r in range(SUBLANES):
            for k in range(TOP_K):
                row_copy(t8, r, k).start(priority=k % 2)
        return c

    lax.fori_loop(0, TILE // SUBLANES, issue, 0)
    hb = h2_ref[...].astype(BF16)
    a = _silu(_dot(hb, sg_ref[...])) * _dot(hb, su_ref[...])
    acc = _dot(a.astype(BF16), sd_ref[...])
    for k in range(TOP_K):
        pltpu.make_async_copy(ys_ref.at[pl.ds(0, TILE)], ys_ref.at[pl.ds(0, TILE)], sem).wait()
    gate = gate_ref[...]
    for k in range(TOP_K):
        acc = acc + gate[:, k:k + 1] * gbuf[k].reshape(TILE, d)
    x2 = xn_ref[...] + mod_ref[0][:, 5 * d:6 * d] * acc
    if final:
        x2 = x2 * lax.rsqrt(jnp.mean(x2 * x2, axis=-1, keepdims=True) + EPS) * fw_ref[...]
    o_ref[...] = x2


def _combine(slots, gates, h2, xn, mod, sh_gate, sh_up, sh_down, fw, ys, *, mod_row_map, final):
    n_tok, d = h2.shape
    n_tiles = n_tok // TILE
    f = sh_gate.shape[1]
    return pl.pallas_call(
        functools.partial(_combine_kernel, d=d, final=final),
        grid=(n_tiles,),
        in_specs=[
            pl.BlockSpec((1, 1, TILE * TOP_K), lambda i: (i, 0, 0), memory_space=pltpu.SMEM),
            pl.BlockSpec((TILE, IDX_W), lambda i: (i, 0)),
            pl.BlockSpec((TILE, d), lambda i: (i, 0)),
            pl.BlockSpec((TILE, d), lambda i: (i, 0)),
            pl.BlockSpec((1, 1, ADA_CHUNKS * d), lambda i: (mod_row_map(i), 0, 0)),
            pl.BlockSpec((d, f), lambda i: (0, 0)),
            pl.BlockSpec((d, f), lambda i: (0, 0)),
            pl.BlockSpec((f, d), lambda i: (0, 0)),
            pl.BlockSpec((1, d), lambda i: (0, 0)),
            pl.BlockSpec(memory_space=pl.ANY),
        ],
        out_specs=pl.BlockSpec((TILE, d), lambda i: (i, 0)),
        out_shape=jax.ShapeDtypeStruct((n_tok, d), F32),
        scratch_shapes=[pltpu.VMEM((TOP_K, TILE // SUBLANES, SUBLANES, d), F32), pltpu.SemaphoreType.DMA(())],
        compiler_params=_cparams("arbitrary"),
        name="moe_combine",
    )(slots.reshape(n_tiles, 1, TILE * TOP_K), gates, h2, xn, mod, sh_gate, sh_up, sh_down, fw, ys)


def _moe(h2, xn, eidx, gates, rank, counts, mod, w_gate, w_up, w_down, sh_gate, sh_up, sh_down, fw,
         *, layer, mod_row_map, final):
    n_tok = h2.shape[0]
    n_exp = w_gate.shape[1]
    n_blk = (n_tok * TOP_K + n_exp * (MOE_BLK - 1) + MOE_BLK - 1) // MOE_BLK
    cnt = counts[:, 0].astype(I32)
    padded = (cnt + MOE_BLK - 1) // MOE_BLK * MOE_BLK
    pend = jnp.cumsum(padded)
    pstart = pend - padded
    expert_ids = jnp.arange(n_exp, dtype=I32)
    eidx, rank, gates = eidx[:TOP_K].T, rank[:TOP_K].T, gates.T
    slots = jnp.sum(jnp.where(eidx[:, :, None] == expert_ids, pstart, 0), axis=-1) + rank
    n_used = pend[-1] // MOE_BLK
    blk = jnp.arange(n_blk, dtype=I32)
    be = jnp.sum((pend[None, :] <= blk[:, None] * MOE_BLK).astype(I32), axis=1)
    last_used = jnp.sum(jnp.where(blk == n_used - 1, be, 0))
    be = jnp.minimum(jnp.where(blk < n_used, be, last_used), n_exp - 1)
    xs = _dispatch(pstart, cnt, slots, h2, n_blk * MOE_BLK)
    ys = _grouped(be, pend // MOE_BLK, n_used.reshape(1).astype(I32), xs, w_gate, w_up, w_down, layer)
    return _combine(slots, gates, h2, xn, mod, sh_gate, sh_up, sh_down, fw, ys,
                    mod_row_map=mod_row_map, final=final)


def _log_sigmoid(x):
    return jnp.minimum(x, 0.0) - jnp.log1p(jnp.exp(-jnp.abs(x)))


def _mlstm_in_kernel(x_ref, mod_ref, nw_ref, w_ref, wgcat_ref, wghi_ref, gb_ref, p_ref, st_ref, r_ref,
                     *, d, cn, qk_w, n_head, k_scale):
    n_s = 2 * n_head
    mod = mod_ref[0]
    hf = _norm_mod(x_ref[...], nw_ref[...], mod[:, 0:d], mod[:, d:2 * d])
    h = hf.astype(BF16)
    for j in range(3 * d // cn):
        c0 = j * cn
        p = _dot(h, w_ref[:, c0:c0 + cn])
        if qk_w <= c0 < 2 * qk_w:
            p = p * k_scale
        p_ref[:, c0:c0 + cn] = p.astype(BF16)
    h_lo = (hf - h.astype(F32)).astype(BF16)
    p_hi = _dot(h, wgcat_ref[...])
    g = p_hi[:, :2 * n_s] + (p_hi[:, 2 * n_s:] + _dot(h_lo, wghi_ref[...])) + gb_ref[...]
    g = GATE_CAP * jnp.tanh(g / GATE_CAP)
    li = g[:, :n_s]
    lf = _log_sigmoid(g[:, n_s:])
    t_i = lax.broadcasted_iota(I32, (TILE, TILE), 0)
    u_i = lax.broadcasted_iota(I32, (TILE, TILE), 1)
    same = (t_i // CHUNK) == (u_i // CHUNK)
    one = lambda m: jnp.where(m, 1.0, 0.0).astype(BF16)
    m_all, m_pre, m_suf = one(same), one(same & (u_i <= t_i)), one(same & (u_i >= t_i))
    parts = _split3(lf)
    msum = lambda m: _dot(m, parts[0]) + (_dot(m, parts[1]) + _dot(m, parts[2]))
    is_fwd = lax.broadcasted_iota(I32, (TILE, n_s), 1) < n_head
    b = jnp.where(is_fwd, msum(m_pre), msum(m_suf))
    r = li - b
    pos = lax.broadcasted_iota(I32, (TILE, n_s), 0) % CHUNK
    run_pre, run_suf = r, r
    step = 1
    while step < CHUNK:
        run_pre = jnp.where(pos >= step, jnp.maximum(run_pre, pltpu.roll(run_pre, step, 0)), run_pre)
        run_suf = jnp.where(pos < CHUNK - step, jnp.maximum(run_suf, pltpu.roll(run_suf, TILE - step, 0)), run_suf)
        step *= 2
    st_ref[:, 0:n_s] = b
    st_ref[:, n_s:2 * n_s] = b + jnp.where(is_fwd, run_pre, run_suf)
    st_ref[:, 2 * n_s:3 * n_s] = msum(m_all) - b + li
    r_ref[...] = r


def _mlstm_in(xt, mod, nw, w_qkvo, w_g, gate_b, *, n_batch, tiles_per_b, n_head):
    n_tok, d = xt.shape
    n_tiles = n_tok // TILE
    qk_w = d // 2
    cn = min(512, qk_w)
    n_s = 2 * n_head
    k_scale = float((qk_w // n_head) ** -0.5)
    order = jnp.arange(4 * n_head).reshape(2, 2, n_head).transpose(1, 0, 2).reshape(-1)
    w_g = w_g[:, order]
    gate_b = gate_b[:, order]
    wg_hi = w_g.astype(BF16)
    wg_cat = jnp.concatenate([wg_hi, (w_g - wg_hi.astype(F32)).astype(BF16)], axis=1)

    def mod_map(i):
        return (jnp.where(i % tiles_per_b == 0, n_batch, i // tiles_per_b), 0, 0)

    return pl.pallas_call(
        functools.partial(_mlstm_in_kernel, d=d, cn=cn, qk_w=qk_w, n_head=n_head, k_scale=k_scale),
        grid=(n_tiles,),
        in_specs=[
            pl.BlockSpec((TILE, d), lambda i: (i, 0)),
            pl.BlockSpec((1, 1, ADA_CHUNKS * d), mod_map),
            pl.BlockSpec((1, d), lambda i: (0, 0)),
            _resident(w_qkvo.shape, lambda i: (0, 0)),
            pl.BlockSpec((d, 4 * n_s), lambda i: (0, 0)),
            pl.BlockSpec((d, 2 * n_s), lambda i: (0, 0)),
            pl.BlockSpec((1, 2 * n_s), lambda i: (0, 0)),
        ],
        out_specs=[
            pl.BlockSpec((TILE, 3 * d), lambda i: (i, 0)),
            pl.BlockSpec((TILE, 3 * n_s), lambda i: (i, 0)),
            pl.BlockSpec((TILE, n_s), lambda i: (i, 0)),
        ],
        out_shape=[
            jax.ShapeDtypeStruct((n_tok, 3 * d), BF16),
            jax.ShapeDtypeStruct((n_tok, 3 * n_s), F32),
            jax.ShapeDtypeStruct((n_tok, n_s), F32),
        ],
        compiler_params=_cparams("parallel"),
        name="mlstm_in",
    )(xt, mod, nw, w_qkvo, wg_cat, wg_hi, gate_b)


def _lanes(x, width):
    if width <= LANES:
        return x[:, :width]
    return jnp.concatenate([x] * (width // LANES), axis=1)


def _mlstm_chunk_open(q, k, v, b_t, top_t, g_t, r_r, c_st, n_st, m_st, *, backward):
    n_t, dqk = q.shape
    dv = v.shape[1]
    tt = lax.broadcasted_iota(I32, (n_t, n_t), 0)
    ss = lax.broadcasted_iota(I32, (n_t, n_t), 1)
    seen = (ss >= tt) if backward else (ss <= tt)
    a = b_t + m_st
    m_row = jnp.maximum(a, top_t)
    w_intra = jnp.exp(jnp.where(seen, _lanes(b_t - m_row, n_t) + r_r, -jnp.inf))
    w_inter = jnp.exp(a - m_row)
    qk = lax.dot_general(q, k, (((1,), (1,)), ((), ())), preferred_element_type=F32)
    inter = _dot(q, c_st.astype(BF16))
    qn = jnp.sum(q.astype(F32) * n_st, axis=1, keepdims=True)
    b_last = b_t[0:1] if backward else b_t[n_t - 1:n_t]
    m_new = jnp.maximum(b_last + m_st, jnp.max(g_t, axis=0, keepdims=True))
    decay = jnp.exp(b_last + m_st - m_new)
    kw = k.astype(F32) * _lanes(jnp.exp(g_t - m_new), dqk)
    c_new = _lanes(decay, dv) * c_st + lax.dot_general(kw.astype(BF16), v, (((0,), (0,)), ((), ())),
                                                     preferred_element_type=F32)
    n_new = _lanes(decay, dqk) * n_st + jnp.sum(kw, axis=0, keepdims=True)
    return (qk, w_intra, w_inter, inter, qn, m_row, v), (c_new, n_new, m_new)


def _mlstm_chunk_close(qk, w_intra, w_inter, inter, qn, m_row, v):
    s = qk * w_intra
    num = _dot(s.astype(BF16), v) + _lanes(w_inter, v.shape[1]) * inter
    den = jnp.sum(s, axis=1, keepdims=True) + w_inter[:, :1] * qn
    return num / jnp.maximum(jnp.abs(den), jnp.exp(-m_row[:, :1]))


def _mlstm_scan_kernel(q_ref, k_ref, v_ref, o_ref, st_ref, r_ref, nw_ref, z_ref, hf_ref, hb_ref, c_ref, rep_ref,
                       *, n_chunk, n_ctx_chunk, dqk, dv, n_hp):
    c_ref[...] = jnp.zeros_like(c_ref)
    n_stat = 2 * 3
    st = st_ref[...]
    lane = lax.broadcasted_iota(I32, st.shape, 1)
    n_head = st.shape[1] // n_stat
    for hp in range(n_hp):
        head = pl.program_id(1) * n_hp + hp
        for direction in range(2):
            for stat in range(3):
                col = jnp.sum(jnp.where(lane == (stat * 2 + direction) * n_head + head, st, 0.0), axis=1, keepdims=True)
                rep_ref[hp * n_stat + 3 * direction + stat] = jnp.broadcast_to(col, rep_ref.shape[1:])
    n0 = jnp.zeros((1, dqk), F32)
    m0 = jnp.full((1, LANES), M_INIT, F32)

    def step(i, carry):
        jf = i
        jb = jnp.where(i < n_ctx_chunk, n_ctx_chunk - 1 - i, n_chunk - 1 - (i - n_ctx_chunk))
        rows = (pl.ds(pl.multiple_of(jf * CHUNK, CHUNK), CHUNK), pl.ds(pl.multiple_of(jb * CHUNK, CHUNK), CHUNK))
        chunk = (jf, jb)
        scans = [(hp, direction) for hp in range(n_hp) for direction in range(2)]
        loaded = []
        for hp, direction in scans:
            r, j = rows[direction], chunk[direction]
            qc = slice(hp * dqk, (hp + 1) * dqk)
            vc = slice(hp * dv, (hp + 1) * dv)
            sc = hp * n_stat + 3 * direction
            loaded.append((q_ref[r, qc], k_ref[r, qc], v_ref[r, vc],
                           rep_ref[sc, r, :], rep_ref[sc + 1, r, :], rep_ref[sc + 2, r, :],
                           r_ref[0, hp, direction, j], c_ref[2 * hp + direction]))
        opened = [_mlstm_chunk_open(*loaded[s], *carry[s], backward=direction == 1)
                  for s, (hp, direction) in enumerate(scans)]
        for s, (hp, direction) in enumerate(scans):
            c_ref[s] = opened[s][1][0]
            (hf_ref, hb_ref)[direction][rows[direction], hp * dv:(hp + 1) * dv] = _mlstm_chunk_close(*opened[s][0])
        return tuple((n_new, m_new) for _, (_, n_new, m_new) in opened)

    lax.fori_loop(0, n_chunk, step, ((n0, m0),) * (2 * n_hp))
    n_ctx = n_ctx_chunk * CHUNK
    n_lat = (n_chunk - n_ctx_chunk) * CHUNK
    lat = pl.ds(n_ctx, n_lat)
    for hp in range(n_hp):
        vc = slice(hp * dv, (hp + 1) * dv)
        h = hf_ref[lat, vc] + hb_ref[lat, vc]
        hn = h * lax.rsqrt(jnp.mean(h * h, axis=-1, keepdims=True) + EPS)
        y = hn * nw_ref[:, vc] * _sigmoid(o_ref[lat, vc].astype(F32))
        z_ref[:, vc] = y.astype(BF16)


def _mlstm_scan(p, st, rr, norm_w, *, n_batch, n_head, seq_all, n_ctx, d):
    dqk = d // (2 * n_head)
    dv = d // n_head
    n_hp = 2 if n_head % 2 == 0 else 1
    n_chunk = seq_all // CHUNK
    n_lat = seq_all - n_ctx
    qk_blocks = (d // 2) // (n_hp * dqk)
    v_blocks = d // (n_hp * dv)
    return pl.pallas_call(
        functools.partial(_mlstm_scan_kernel, n_chunk=n_chunk, n_ctx_chunk=n_ctx // CHUNK, dqk=dqk, dv=dv,
                          n_hp=n_hp),
        grid=(n_batch, n_head // n_hp),
        in_specs=[
            pl.BlockSpec((seq_all, n_hp * dqk), lambda b, h: (b, h)),
            pl.BlockSpec((seq_all, n_hp * dqk), lambda b, h: (b, qk_blocks + h)),
            pl.BlockSpec((seq_all, n_hp * dv), lambda b, h: (b, v_blocks + h)),
            pl.BlockSpec((seq_all, n_hp * dv), lambda b, h: (b, 2 * v_blocks + h)),
            pl.BlockSpec((seq_all, 6 * n_head), lambda b, h: (b, 0)),
            pl.BlockSpec((1, n_hp, 2, n_chunk, 1, CHUNK), lambda b, h: (b, h, 0, 0, 0, 0)),
            pl.BlockSpec((1, n_hp * dv), lambda b, h: (0, h)),
        ],
        out_specs=pl.BlockSpec((n_lat, n_hp * dv), lambda b, h: (b, h)),
        out_shape=jax.ShapeDtypeStruct((n_batch * n_lat, d), BF16),
        scratch_shapes=[
            pltpu.VMEM((seq_all, n_hp * dv), F32),
            pltpu.VMEM((seq_all, n_hp * dv), F32),
            pltpu.VMEM((2 * n_hp, dqk, dv), F32),
            pltpu.VMEM((n_hp * 6, seq_all, LANES), F32),
        ],
        compiler_params=_cparams("parallel", "parallel"),
        name="mlstm_scan",
    )(p, p, p, p, st, rr, norm_w)


def kernel(x, c, ctx, c_ctx, ada_w, ada_b, norm_mix_w, norm_ffn_w, conv_in_w, conv_dw_w, conv_out_w,
           mlstm_in_w, mlstm_gate_b, mlstm_norm_w, mlstm_out_w, router_w, router_bias,
           exp_gate_w, exp_up_w, exp_down_w, shared_gate_w, shared_up_w, shared_down_w, final_norm_w):
    n_batch, seq, d = x.shape
    n_ctx = ctx.shape[1]
    assert ada_w.shape[0] == 2 and n_ctx == TILE and seq % TILE == 0 and n_batch + 1 <= ADA_ROWS
    seq_all = n_ctx + seq
    tiles_per_b = seq_all // TILE
    lat_tiles_per_b = seq // TILE
    n_head = (mlstm_in_w.shape[2] - 3 * d) // 4

    cond = jnp.zeros((ADA_ROWS, d), F32).at[:n_batch].set(c).at[n_batch].set(c_ctx)
    mod = _ada_mod(cond, ada_w, ada_b)
    mod0 = mod[0].reshape(ADA_ROWS, 1, ADA_CHUNKS * d)
    mod1 = mod[1].reshape(ADA_ROWS, 1, ADA_CHUNKS * d)

    def all_mod_row(i):
        return jnp.where(i % tiles_per_b == 0, n_batch, i // tiles_per_b)

    def lat_mod_row(i):
        return i // lat_tiles_per_b

    def lat_tile(i):
        return (i // lat_tiles_per_b) * tiles_per_b + 1 + i % lat_tiles_per_b

    row = lambda w: w.reshape(1, -1)
    bf = lambda w: w.astype(BF16)
    ctx2 = ctx.reshape(n_batch * n_ctx, d)
    x2 = x.reshape(n_batch * seq, d)

    def is_ctx_tile(i):
        return i % tiles_per_b == 0

    def ctx_tile(i):
        return i // tiles_per_b

    def x_tile(i):
        return (i // tiles_per_b) * lat_tiles_per_b + jnp.maximum(i % tiles_per_b - 1, 0)

    z0 = _conv_in(ctx2, x2, mod0, row(norm_mix_w[0]), bf(conv_in_w[0]), conv_dw_w[0],
                  n_batch=n_batch, tiles_per_b=tiles_per_b, ctx_map=ctx_tile, lat_map=x_tile)
    xn0, h20, eidx0, gate0, rank0, cnt0 = _post(
        z0, ctx2, x2, mod0, row(norm_ffn_w[0]), bf(conv_out_w[0]), router_w[0], row(router_bias[0]),
        a_map=ctx_tile, b_map=x_tile, pick_a=is_ctx_tile, mod_row_map=all_mod_row)
    x1 = _moe(h20, xn0, eidx0, gate0, rank0, cnt0, mod0, exp_gate_w, exp_up_w, exp_down_w,
              bf(shared_gate_w[0]), bf(shared_up_w[0]), bf(shared_down_w[0]), row(final_norm_w),
              layer=0, mod_row_map=all_mod_row, final=False)

    w_in = mlstm_in_w[0]
    p, st, r = _mlstm_in(x1, mod1, row(norm_mix_w[1]), bf(w_in), w_in[:, 3 * d:], row(mlstm_gate_b[0]),
                     n_batch=n_batch, tiles_per_b=tiles_per_b, n_head=n_head)
    n_chunk = seq_all // CHUNK
    rr = r.reshape(n_batch, n_chunk, CHUNK, 2, n_head).transpose(0, 4, 3, 1, 2)
    rr = rr.reshape(n_batch, n_head, 2, n_chunk, 1, CHUNK)
    z1 = _mlstm_scan(p, st, rr, row(mlstm_norm_w[0]), n_batch=n_batch, n_head=n_head, seq_all=seq_all,
                     n_ctx=n_ctx, d=d)
    xn1, h21, eidx1, gate1, rank1, cnt1 = _post(
        z1, x1, x1, mod1, row(norm_ffn_w[1]), bf(mlstm_out_w[0]), router_w[1], row(router_bias[1]),
        a_map=lat_tile, b_map=lambda i: 0, pick_a=lambda i: i >= 0, mod_row_map=lat_mod_row)
    out = _moe(h21, xn1, eidx1, gate1, rank1, cnt1, mod1, exp_gate_w, exp_up_w, exp_down_w,
               bf(shared_gate_w[1]), bf(shared_up_w[1]), bf(shared_down_w[1]), row(final_norm_w),
               layer=1, mod_row_map=lat_mod_row, final=True)
    return out.reshape(n_batch, seq, d)
```

```python
import functools

import jax
import jax.numpy as jnp
from jax import lax
from jax.experimental import pallas as pl
from jax.experimental.pallas import tpu as pltpu

F32 = jnp.float32
BF16 = jnp.bfloat16
I32 = jnp.int32

TILE = 256
GRID_W = 64
CHUNK = 128
TOP_K = 6
MOE_BLK = 256
IDX_W = 8
LANES = 128
SUBLANES = 8
ADA_CHUNKS = 6
ADA_ROWS = 16
EPS = 1e-6
GATE_CAP = 15.0
M_INIT = -1e30
ROUTED_SCALE = 2.5
V7X_VMEM_LIMIT = 56 * 1024 * 1024


def _cparams(*sem):
    return pltpu.CompilerParams(dimension_semantics=sem, vmem_limit_bytes=V7X_VMEM_LIMIT)


def _resident(shape, index_map):
    return pl.BlockSpec(shape, index_map, pipeline_mode=pl.Buffered(1))


def _sigmoid(x):
    return 1.0 / (1.0 + jnp.exp(-x))


def _silu(x):
    return x * _sigmoid(x)


def _split3(a):
    hi = a.astype(BF16)
    r1 = a - hi.astype(F32)
    mid = r1.astype(BF16)
    lo = (r1 - mid.astype(F32)).astype(BF16)
    return hi, mid, lo


def _dot(a, b):
    return jnp.dot(a, b, preferred_element_type=F32)


def _norm_mod(x, w, shift, scale):
    y = x * lax.rsqrt(jnp.mean(x * x, axis=-1, keepdims=True) + EPS)
    return (y * w) * (1.0 + scale) + shift


def _ada_kernel(cond_ref, w_ref, b_ref, o_ref):
    a = _silu(cond_ref[...]).astype(BF16)
    o_ref[0] = _dot(a, w_ref[0].astype(BF16)) + b_ref[0]


def _ada_mod(cond, ada_w, ada_b):
    n_layer, d, n_out = ada_w.shape
    tn = 1024 if n_out % 1024 == 0 else n_out
    return pl.pallas_call(
        _ada_kernel,
        grid=(n_layer, n_out // tn),
        in_specs=[
            pl.BlockSpec((ADA_ROWS, d), lambda l, j: (0, 0)),
            pl.BlockSpec((1, d, tn), lambda l, j: (l, 0, j)),
            pl.BlockSpec((1, 1, tn), lambda l, j: (l, 0, j)),
        ],
        out_specs=pl.BlockSpec((1, ADA_ROWS, tn), lambda l, j: (l, 0, j)),
        out_shape=jax.ShapeDtypeStruct((n_layer, ADA_ROWS, n_out), F32),
        compiler_params=_cparams("parallel", "parallel"),
        name="ada_mod",
    )(cond, ada_w, ada_b.reshape(n_layer, 1, n_out))


def _conv_in_kernel(ctx_ref, x_ref, mod_ref, nw_ref, win_ref, wdw_ref, z_ref, *, d, cn, tiles_per_b):
    is_ctx = (pl.program_id(0) % tiles_per_b) == 0
    mod = mod_ref[0]
    xt = jnp.where(is_ctx, ctx_ref[...], x_ref[...])
    h = _norm_mod(xt, nw_ref[...], mod[:, 0:d], mod[:, d:2 * d]).astype(BF16)
    t = lax.broadcasted_iota(I32, (TILE, 1), 0)
    pos_mask = jnp.where(is_ctx, TILE - 1, GRID_W - 1)
    pos = jnp.bitwise_and(t, pos_mask)
    first = pos == 0
    last = pos == pos_mask
    for j in range(d // cn):
        c0 = j * cn
        bg = _dot(h, win_ref[:, c0:c0 + cn])
        cg = _dot(h, win_ref[:, d + c0:d + c0 + cn])
        hi = _dot(h, win_ref[:, 2 * d + c0:2 * d + c0 + cn])
        u = cg * hi
        u_prev = jnp.where(first, 0.0, pltpu.roll(u, 1, 0))
        u_next = jnp.where(last, 0.0, pltpu.roll(u, TILE - 1, 0))
        w = wdw_ref[:, c0:c0 + cn]
        y = u_prev * w[0:1] + u * w[1:2] + u_next * w[2:3]
        z_ref[:, c0:c0 + cn] = (bg * y).astype(BF16)


def _conv_in(ctx2, x2, mod, nw, w_in, w_dw, *, n_batch, tiles_per_b, ctx_map, lat_map):
    d = x2.shape[1]
    n_tiles = n_batch * tiles_per_b
    n_tok = n_tiles * TILE
    cn = min(512, d)

    def mod_map(i):
        return (jnp.where(i % tiles_per_b == 0, n_batch, i // tiles_per_b), 0, 0)

    return pl.pallas_call(
        functools.partial(_conv_in_kernel, d=d, cn=cn, tiles_per_b=tiles_per_b),
        grid=(n_tiles,),
        in_specs=[
            pl.BlockSpec((TILE, d), lambda i: (ctx_map(i), 0)),
            pl.BlockSpec((TILE, d), lambda i: (lat_map(i), 0)),
            pl.BlockSpec((1, 1, ADA_CHUNKS * d), mod_map),
            pl.BlockSpec((1, d), lambda i: (0, 0)),
            _resident((d, 3 * d), lambda i: (0, 0)),
            pl.BlockSpec((3, d), lambda i: (0, 0)),
        ],
        out_specs=pl.BlockSpec((TILE, d), lambda i: (i, 0)),
        out_shape=jax.ShapeDtypeStruct((n_tok, d), BF16),
        compiler_params=_cparams("parallel"),
        name="conv_in",
    )(ctx2, x2, mod, nw, w_in, w_dw)


def _post_kernel(z_ref, xa_ref, xb_ref, mod_ref, nw_ref, wout_ref, rwcat_ref, rwhi_ref, rb_ref,
                 xn_ref, h2_ref, eidx_ref, gate_ref, rank_ref, cnt_ref, carry_ref, *, d, n_exp, pick_a):
    i = pl.program_id(0)

    @pl.when(i == 0)
    def _():
        carry_ref[...] = jnp.zeros_like(carry_ref)

    mod = mod_ref[0]
    y = _dot(z_ref[...], wout_ref[...])
    xn = jnp.where(pick_a(i), xa_ref[...], xb_ref[...]) + mod[:, 2 * d:3 * d] * y
    xn_ref[...] = xn
    h2 = _norm_mod(xn, nw_ref[...], mod[:, 3 * d:4 * d], mod[:, 4 * d:5 * d])
    h2_ref[...] = h2

    nt = (((1,), (1,)), ((), ()))
    h2_hi = h2.astype(BF16)
    h2_lo = (h2 - h2_hi.astype(F32)).astype(BF16)
    p_hi = lax.dot_general(rwcat_ref[...], h2_hi, nt, preferred_element_type=F32)
    logits = p_hi[:n_exp] + (p_hi[n_exp:] + lax.dot_general(rwhi_ref[...], h2_lo, nt, preferred_element_type=F32))
    scores = _sigmoid(logits)
    row_f = lax.broadcasted_iota(I32, (n_exp, TILE), 0).astype(F32)
    work = scores + rb_ref[...]
    onehots, picks = [], []
    for _ in range(TOP_K):
        mx = jnp.max(work, axis=0, keepdims=True)
        first_max = jnp.min(jnp.where(work == mx, row_f, float(n_exp)), axis=0, keepdims=True)
        oh = row_f == first_max
        onehots.append(oh)
        picks.append(first_max)
        work = jnp.where(oh, -jnp.inf, work)
    sel = onehots[0]
    for oh in onehots[1:]:
        sel = jnp.logical_or(sel, oh)
    picked = jnp.where(sel, scores, 0.0)
    gates = picked / jnp.sum(picked, axis=0, keepdims=True) * ROUTED_SCALE
    sel_f = jnp.where(sel, 1.0, 0.0)
    r_i = lax.broadcasted_iota(I32, (TILE, TILE), 0)
    c_i = lax.broadcasted_iota(I32, (TILE, TILE), 1)
    before = jnp.where(r_i < c_i, 1.0, 0.0).astype(BF16)
    cum = _dot(sel_f.astype(BF16), before) + carry_ref[...]

    eidx_ref[...] = jnp.zeros_like(eidx_ref)
    gate_ref[...] = jnp.zeros_like(gate_ref)
    rank_ref[...] = jnp.zeros_like(rank_ref)
    for k, oh in enumerate(onehots):
        eidx_ref[k:k + 1, :] = picks[k].astype(I32)
        gate_ref[k:k + 1, :] = jnp.sum(jnp.where(oh, gates, 0.0), axis=0, keepdims=True)
        rank_ref[k:k + 1, :] = jnp.sum(jnp.where(oh, cum, 0.0), axis=0, keepdims=True).astype(I32)

    total = carry_ref[...] + jnp.sum(sel_f, axis=1, keepdims=True)
    carry_ref[...] = total
    cnt_ref[...] = total


def _post(z, xa, xb, mod, nw, w_out, router_w, router_b, *, a_map, b_map, pick_a, mod_row_map):
    rw_t = router_w.T
    rw_hi = rw_t.astype(BF16)
    rw_lo = (rw_t - rw_hi.astype(F32)).astype(BF16)
    rw_cat = jnp.concatenate([rw_hi, rw_lo], axis=0)
    n_tok, d = z.shape
    n_tiles = n_tok // TILE
    n_exp = router_w.shape[1]
    outs = pl.pallas_call(
        functools.partial(_post_kernel, d=d, n_exp=n_exp, pick_a=pick_a),
        grid=(n_tiles,),
        in_specs=[
            pl.BlockSpec((TILE, d), lambda i: (i, 0)),
            pl.BlockSpec((TILE, d), lambda i: (a_map(i), 0)),
            pl.BlockSpec((TILE, d), lambda i: (b_map(i), 0)),
            pl.BlockSpec((1, 1, ADA_CHUNKS * d), lambda i: (mod_row_map(i), 0, 0)),
            pl.BlockSpec((1, d), lambda i: (0, 0)),
            _resident((d, d), lambda i: (0, 0)),
            pl.BlockSpec((2 * n_exp, d), lambda i: (0, 0)),
            pl.BlockSpec((n_exp, d), lambda i: (0, 0)),
            pl.BlockSpec((n_exp, 1), lambda i: (0, 0)),
        ],
        out_specs=[
            pl.BlockSpec((TILE, d), lambda i: (i, 0)),
            pl.BlockSpec((TILE, d), lambda i: (i, 0)),
            pl.BlockSpec((IDX_W, TILE), lambda i: (0, i)),
            pl.BlockSpec((IDX_W, TILE), lambda i: (0, i)),
            pl.BlockSpec((IDX_W, TILE), lambda i: (0, i)),
            pl.BlockSpec((n_exp, 1), lambda i: (0, 0)),
        ],
        out_shape=[
            jax.ShapeDtypeStruct((n_tok, d), F32),
            jax.ShapeDtypeStruct((n_tok, d), F32),
            jax.ShapeDtypeStruct((IDX_W, n_tok), I32),
            jax.ShapeDtypeStruct((IDX_W, n_tok), F32),
            jax.ShapeDtypeStruct((IDX_W, n_tok), I32),
            jax.ShapeDtypeStruct((n_exp, 1), F32),
        ],
        scratch_shapes=[pltpu.VMEM((n_exp, 1), F32)],
        compiler_params=_cparams("arbitrary"),
        name="post_mixer",
    )(z, xa, xb, mod, nw, w_out, rw_cat, rw_hi, router_b.reshape(n_exp, 1))
    return outs


def _dispatch_kernel(pstart_ref, cnt_ref, slot_ref, h2_ref, xs_ref, zbuf, sem, zsem, *, n_exp):
    def pad_fill(e, wait):
        rem = cnt_ref[e] % MOE_BLK
        pad = jnp.where(rem == 0, 0, MOE_BLK - rem)
        base = pstart_ref[e] + cnt_ref[e]
        head = jnp.minimum(pad, jnp.bitwise_and(-base, SUBLANES - 1))

        def fill(off, size, cond):
            copy = pltpu.make_async_copy(zbuf.at[pl.ds(0, size)], xs_ref.at[pl.ds(off, size)], zsem)

            @pl.when(cond)
            def _():
                copy.wait() if wait else copy.start()

        for r in range(SUBLANES - 1):
            fill(base + r, 1, r < head)
        off = base + head
        rest = pad - head
        for bit in reversed(range(SUBLANES.bit_length() - 1, MOE_BLK.bit_length() - 1)):
            size = 1 << bit
            take = (rest >> bit) & 1
            fill(pl.multiple_of(off, SUBLANES), size, take == 1)
            off = off + take * size

    @pl.when(pl.program_id(0) == 0)
    def _():
        zbuf[...] = jnp.zeros_like(zbuf)

        def fill(e, c):
            pad_fill(e, False)
            return c

        def fill_wait(e, c):
            pad_fill(e, True)
            return c

        lax.fori_loop(0, n_exp, fill, 0)
        lax.fori_loop(0, n_exp, fill_wait, 0)

    def row_copy(t8, r, k):
        s = slot_ref[0, 0, (t8 * SUBLANES + r) * TOP_K + k]
        return pltpu.make_async_copy(h2_ref.at[t8, pl.ds(r, 1)], xs_ref.at[pl.ds(s, 1)], sem)

    def issue(t8, c):
        for r in range(SUBLANES):
            for k in range(TOP_K):
                row_copy(t8, r, k).start(priority=k % 2)
        return c

    lax.fori_loop(0, TILE // SUBLANES, issue, 0)
    for k in range(TOP_K):
        pltpu.make_async_copy(xs_ref.at[pl.ds(0, TILE)], xs_ref.at[pl.ds(0, TILE)], sem).wait()


def _dispatch(pstart, cnt, slots, h2, n_slots):
    n_tok, d = h2.shape
    n_tiles = n_tok // TILE
    grid_spec = pltpu.PrefetchScalarGridSpec(
        num_scalar_prefetch=2,
        grid=(n_tiles,),
        in_specs=[
            pl.BlockSpec((1, 1, TILE * TOP_K), lambda i, ps, ct: (i, 0, 0), memory_space=pltpu.SMEM),
            pl.BlockSpec((TILE // SUBLANES, SUBLANES, d), lambda i, ps, ct: (i, 0, 0)),
        ],
        out_specs=pl.BlockSpec(memory_space=pl.ANY),
        scratch_shapes=[pltpu.VMEM((MOE_BLK // 2, d), F32), pltpu.SemaphoreType.DMA(()),
                        pltpu.SemaphoreType.DMA(())],
    )
    return pl.pallas_call(
        functools.partial(_dispatch_kernel, n_exp=pstart.shape[0]),
        grid_spec=grid_spec,
        out_shape=jax.ShapeDtypeStruct((n_slots, d), F32),
        compiler_params=_cparams("arbitrary"),
        name="moe_dispatch",
    )(pstart, cnt, slots.reshape(n_tiles, 1, TILE * TOP_K), h2.reshape(n_tok // SUBLANES, SUBLANES, d))


def _grouped_kernel(be_ref, after_ref, meta_ref, xs_ref, wg_hbm, wu_hbm, wd_hbm, ys_ref,
                    wg_f, wu_f, wd_f, wg_s, wu_s, wd_s, wsem, *, layer):
    i = pl.program_id(0)
    used = i < meta_ref[0]
    expert = be_ref[i]
    new_expert = jnp.logical_or(i == 0, expert != be_ref[jnp.maximum(i - 1, 0)])

    def fetch(e):
        return (pltpu.make_async_copy(wg_hbm.at[layer, e], wg_f, wsem.at[0]),
                pltpu.make_async_copy(wu_hbm.at[layer, e], wu_f, wsem.at[1]),
                pltpu.make_async_copy(wd_hbm.at[layer, e], wd_f, wsem.at[2]))

    @pl.when(i == 0)
    def _():
        for copy in fetch(expert):
            copy.start()

    @pl.when(jnp.logical_and(used, new_expert))
    def _():
        for copy in fetch(expert):
            copy.wait()
        wg_s[...] = wg_f[...].astype(BF16)
        wu_s[...] = wu_f[...].astype(BF16)
        wd_s[...] = wd_f[...].astype(BF16)

        follower = after_ref[expert]

        @pl.when(follower < meta_ref[0])
        def _():
            for copy in fetch(be_ref[jnp.minimum(follower, meta_ref[0] - 1)]):
                copy.start()

    @pl.when(used)
    def _():
        xb = xs_ref[...].astype(BF16)
        a = _silu(_dot(xb, wg_s[...])) * _dot(xb, wu_s[...])
        ys_ref[...] = _dot(a.astype(BF16), wd_s[...])

    @pl.when(jnp.logical_not(used))
    def _():
        ys_ref[...] = jnp.zeros_like(ys_ref)


def _grouped(blk_expert, blk_after, n_used, xs, w_gate, w_up, w_down, layer):
    n_slots, d = xs.shape
    n_blk = n_slots // MOE_BLK
    f = w_gate.shape[3]

    def row_map(i, be, nxt, meta):
        return (jnp.minimum(i, jnp.maximum(meta[0] - 1, 0)), 0)

    grid_spec = pltpu.PrefetchScalarGridSpec(
        num_scalar_prefetch=3,
        grid=(n_blk,),
        in_specs=[
            pl.BlockSpec((MOE_BLK, d), row_map),
            pl.BlockSpec(memory_space=pl.ANY),
            pl.BlockSpec(memory_space=pl.ANY),
            pl.BlockSpec(memory_space=pl.ANY),
        ],
        out_specs=pl.BlockSpec((MOE_BLK, d), lambda i, be, nxt, meta: (i, 0)),
        scratch_shapes=[pltpu.VMEM((d, f), F32), pltpu.VMEM((d, f), F32), pltpu.VMEM((f, d), F32),
                        pltpu.VMEM((d, f), BF16), pltpu.VMEM((d, f), BF16), pltpu.VMEM((f, d), BF16),
                        pltpu.SemaphoreType.DMA((3,))],
    )
    return pl.pallas_call(
        functools.partial(_grouped_kernel, layer=layer),
        grid_spec=grid_spec,
        out_shape=jax.ShapeDtypeStruct((n_slots, d), F32),
        compiler_params=_cparams("arbitrary"),
        name="moe_experts",
    )(blk_expert, blk_after, n_used, xs, w_gate, w_up, w_down)


def _combine_kernel(slot_ref, gate_ref, h2_ref, xn_ref, mod_ref, sg_ref, su_ref, sd_ref, fw_ref, ys_ref,
                    o_ref, gbuf, sem, *, d, final):
    def row_copy(t8, r, k):
        s = slot_ref[0, 0, (t8 * SUBLANES + r) * TOP_K + k]
        return pltpu.make_async_copy(ys_ref.at[pl.ds(s, 1)], gbuf.at[k, t8, pl.ds(r, 1)], sem)

    def issue(t8, c):
        for r in range(SUBLANES):
            for k in range(TOP_K):
                row_copy(t8, r, k).start(priority=1)
        return c

    lax.fori_loop(0, TILE // SUBLANES, issue, 0)
    hb = h2_ref[...].astype(BF16)
    a = _silu(_dot(hb, sg_ref[...])) * _dot(hb, su_ref[...])
    acc = _dot(a.astype(BF16), sd_ref[...])
    for k in range(TOP_K):
        pltpu.make_async_copy(ys_ref.at[pl.ds(0, TILE)], ys_ref.at[pl.ds(0, TILE)], sem).wait()
    gate = gate_ref[...]
    for k in range(TOP_K):
        acc = acc + gate[:, k:k + 1] * gbuf[k].reshape(TILE, d)
    x2 = xn_ref[...] + mod_ref[0][:, 5 * d:6 * d] * acc
    if final:
        x2 = x2 * lax.rsqrt(jnp.mean(x2 * x2, axis=-1, keepdims=True) + EPS) * fw_ref[...]
    o_ref[...] = x2


def _combine(slots, gates, h2, xn, mod, sh_gate, sh_up, sh_down, fw, ys, *, mod_row_map, final):
    n_tok, d = h2.shape
    n_tiles = n_tok // TILE
    f = sh_gate.shape[1]
    return pl.pallas_call(
        functools.partial(_combine_kernel, d=d, final=final),
        grid=(n_tiles,),
        in_specs=[
            pl.BlockSpec((1, 1, TILE * TOP_K), lambda i: (i, 0, 0), memory_space=pltpu.SMEM),
            pl.BlockSpec((TILE, IDX_W), lambda i: (i, 0)),
            pl.BlockSpec((TILE, d), lambda i: (i, 0)),
            pl.BlockSpec((TILE, d), lambda i: (i, 0)),
            pl.BlockSpec((1, 1, ADA_CHUNKS * d), lambda i: (mod_row_map(i), 0, 0)),
            pl.BlockSpec((d, f), lambda i: (0, 0)),
            pl.BlockSpec((d, f), lambda i: (0, 0)),
            pl.BlockSpec((f, d), lambda i: (0, 0)),
            pl.BlockSpec((1, d), lambda i: (0, 0)),
            pl.BlockSpec(memory_space=pl.ANY),
        ],
        out_specs=pl.BlockSpec((TILE, d), lambda i: (i, 0)),
        out_shape=jax.ShapeDtypeStruct((n_tok, d), F32),
        scratch_shapes=[pltpu.VMEM((TOP_K, TILE // SUBLANES, SUBLANES, d), F32), pltpu.SemaphoreType.DMA(())],
        compiler_params=_cparams("arbitrary"),
        name="moe_combine",
    )(slots.reshape(n_tiles, 1, TILE * TOP_K), gates, h2, xn, mod, sh_gate, sh_up, sh_down, fw, ys)


def _moe(h2, xn, eidx, gates, rank, counts, mod, w_gate, w_up, w_down, sh_gate, sh_up, sh_down, fw,
         *, layer, mod_row_map, final):
    n_tok = h2.shape[0]
    n_exp = w_gate.shape[1]
    n_blk = (n_tok * TOP_K + n_exp * (MOE_BLK - 1) + MOE_BLK - 1) // MOE_BLK
    cnt = counts[:, 0].astype(I32)
    padded = (cnt + MOE_BLK - 1) // MOE_BLK * MOE_BLK
    pend = jnp.cumsum(padded)
    pstart = pend - padded
    expert_ids = jnp.arange(n_exp, dtype=I32)
    eidx, rank, gates = eidx[:TOP_K].T, rank[:TOP_K].T, gates.T
    slots = jnp.sum(jnp.where(eidx[:, :, None] == expert_ids, pstart, 0), axis=-1) + rank
    n_used = pend[-1] // MOE_BLK
    blk = jnp.arange(n_blk, dtype=I32)
    be = jnp.sum((pend[None, :] <= blk[:, None] * MOE_BLK).astype(I32), axis=1)
    last_used = jnp.sum(jnp.where(blk == n_used - 1, be, 0))
    be = jnp.minimum(jnp.where(blk < n_used, be, last_used), n_exp - 1)
    xs = _dispatch(pstart, cnt, slots, h2, n_blk * MOE_BLK)
    ys = _grouped(be, pend // MOE_BLK, n_used.reshape(1).astype(I32), xs, w_gate, w_up, w_down, layer)
    return _combine(slots, gates, h2, xn, mod, sh_gate, sh_up, sh_down, fw, ys,
                    mod_row_map=mod_row_map, final=final)


def _log_sigmoid(x):
    return jnp.minimum(x, 0.0) - jnp.log1p(jnp.exp(-jnp.abs(x)))


def _mlstm_in_kernel(x_ref, mod_ref, nw_ref, w_ref, wgcat_ref, wghi_ref, gb_ref, p_ref, st_ref, r_ref,
                     *, d, cn, qk_w, n_head, k_scale):
    n_s = 2 * n_head
    mod = mod_ref[0]
    hf = _norm_mod(x_ref[...], nw_ref[...], mod[:, 0:d], mod[:, d:2 * d])
    h = hf.astype(BF16)
    for j in range(3 * d // cn):
        c0 = j * cn
        p = _dot(h, w_ref[:, c0:c0 + cn])
        if qk_w <= c0 < 2 * qk_w:
            p = p * k_scale
        p_ref[:, c0:c0 + cn] = p.astype(BF16)
    h_lo = (hf - h.astype(F32)).astype(BF16)
    p_hi = _dot(h, wgcat_ref[...])
    g = p_hi[:, :2 * n_s] + (p_hi[:, 2 * n_s:] + _dot(h_lo, wghi_ref[...])) + gb_ref[...]
    g = GATE_CAP * jnp.tanh(g / GATE_CAP)
    li = g[:, :n_s]
    lf = _log_sigmoid(g[:, n_s:])
    t_i = lax.broadcasted_iota(I32, (TILE, TILE), 0)
    u_i = lax.broadcasted_iota(I32, (TILE, TILE), 1)
    same = (t_i // CHUNK) == (u_i // CHUNK)
    one = lambda m: jnp.where(m, 1.0, 0.0).astype(BF16)
    m_all, m_pre, m_suf = one(same), one(same & (u_i <= t_i)), one(same & (u_i >= t_i))
    parts = _split3(lf)
    msum = lambda m: _dot(m, parts[0]) + (_dot(m, parts[1]) + _dot(m, parts[2]))
    is_fwd = lax.broadcasted_iota(I32, (TILE, n_s), 1) < n_head
    b = jnp.where(is_fwd, msum(m_pre), msum(m_suf))
    r = li - b
    pos = lax.broadcasted_iota(I32, (TILE, n_s), 0) % CHUNK
    run_pre, run_suf = r, r
    step = 1
    while step < CHUNK:
        run_pre = jnp.where(pos >= step, jnp.maximum(run_pre, pltpu.roll(run_pre, step, 0)), run_pre)
        run_suf = jnp.where(pos < CHUNK - step, jnp.maximum(run_suf, pltpu.roll(run_suf, TILE - step, 0)), run_suf)
        step *= 2
    st_ref[:, 0:n_s] = b
    st_ref[:, n_s:2 * n_s] = b + jnp.where(is_fwd, run_pre, run_suf)
    st_ref[:, 2 * n_s:3 * n_s] = msum(m_all) - b + li
    r_ref[...] = r


def _mlstm_in(xt, mod, nw, w_qkvo, w_g, gate_b, *, n_batch, tiles_per_b, n_head):
    n_tok, d = xt.shape
    n_tiles = n_tok // TILE
    qk_w = d // 2
    cn = min(512, qk_w)
    n_s = 2 * n_head
    k_scale = float((qk_w // n_head) ** -0.5)
    order = jnp.arange(4 * n_head).reshape(2, 2, n_head).transpose(1, 0, 2).reshape(-1)
    w_g = w_g[:, order]
    gate_b = gate_b[:, order]
    wg_hi = w_g.astype(BF16)
    wg_cat = jnp.concatenate([wg_hi, (w_g - wg_hi.astype(F32)).astype(BF16)], axis=1)

    def mod_map(i):
        return (jnp.where(i % tiles_per_b == 0, n_batch, i // tiles_per_b), 0, 0)

    return pl.pallas_call(
        functools.partial(_mlstm_in_kernel, d=d, cn=cn, qk_w=qk_w, n_head=n_head, k_scale=k_scale),
        grid=(n_tiles,),
        in_specs=[
            pl.BlockSpec((TILE, d), lambda i: (i, 0)),
            pl.BlockSpec((1, 1, ADA_CHUNKS * d), mod_map),
            pl.BlockSpec((1, d), lambda i: (0, 0)),
            _resident(w_qkvo.shape, lambda i: (0, 0)),
            pl.BlockSpec((d, 4 * n_s), lambda i: (0, 0)),
            pl.BlockSpec((d, 2 * n_s), lambda i: (0, 0)),
            pl.BlockSpec((1, 2 * n_s), lambda i: (0, 0)),
        ],
        out_specs=[
            pl.BlockSpec((TILE, 3 * d), lambda i: (i, 0)),
            pl.BlockSpec((TILE, 3 * n_s), lambda i: (i, 0)),
            pl.BlockSpec((TILE, n_s), lambda i: (i, 0)),
        ],
        out_shape=[
            jax.ShapeDtypeStruct((n_tok, 3 * d), BF16),
            jax.ShapeDtypeStruct((n_tok, 3 * n_s), F32),
            jax.ShapeDtypeStruct((n_tok, n_s), F32),
        ],
        compiler_params=_cparams("parallel"),
        name="mlstm_in",
    )(xt, mod, nw, w_qkvo, wg_cat, wg_hi, gate_b)


def _lanes(x, width):
    if width <= LANES:
        return x[:, :width]
    return jnp.concatenate([x] * (width // LANES), axis=1)


def _mlstm_chunk_open(q, k, v, b_t, top_t, g_t, r_r, c_st, n_st, m_st, *, backward):
    n_t, dqk = q.shape
    dv = v.shape[1]
    tt = lax.broadcasted_iota(I32, (n_t, n_t), 0)
    ss = lax.broadcasted_iota(I32, (n_t, n_t), 1)
    seen = (ss >= tt) if backward else (ss <= tt)
    a = b_t + m_st
    m_row = jnp.maximum(a, top_t)
    w_intra = jnp.exp(jnp.where(seen, _lanes(b_t - m_row, n_t) + r_r, -jnp.inf))
    w_inter = jnp.exp(a - m_row)
    qk = lax.dot_general(q, k, (((1,), (1,)), ((), ())), preferred_element_type=F32)
    inter = _dot(q, c_st.astype(BF16))
    qn = jnp.sum(q.astype(F32) * n_st, axis=1, keepdims=True)
    b_last = b_t[0:1] if backward else b_t[n_t - 1:n_t]
    m_new = jnp.maximum(b_last + m_st, jnp.max(g_t, axis=0, keepdims=True))
    decay = jnp.exp(b_last + m_st - m_new)
    kw = k.astype(F32) * _lanes(jnp.exp(g_t - m_new), dqk)
    c_new = _lanes(decay, dv) * c_st + lax.dot_general(kw.astype(BF16), v, (((0,), (0,)), ((), ())),
                                                     preferred_element_type=F32)
    n_new = _lanes(decay, dqk) * n_st + jnp.sum(kw, axis=0, keepdims=True)
    return (qk, w_intra, w_inter, inter, qn, m_row, v), (c_new, n_new, m_new)


def _mlstm_chunk_close(qk, w_intra, w_inter, inter, qn, m_row, v):
    s = qk * w_intra
    num = _dot(s.astype(BF16), v) + _lanes(w_inter, v.shape[1]) * inter
    den = jnp.sum(s, axis=1, keepdims=True) + w_inter[:, :1] * qn
    return num / jnp.maximum(jnp.abs(den), jnp.exp(-m_row[:, :1]))


def _mlstm_scan_kernel(q_ref, k_ref, v_ref, o_ref, st_ref, r_ref, nw_ref, z_ref, hf_ref, hb_ref, c_ref, rep_ref,
                       *, n_chunk, n_ctx_chunk, dqk, dv, n_hp):
    c_ref[...] = jnp.zeros_like(c_ref)
    n_stat = 2 * 3
    st = st_ref[...]
    lane = lax.broadcasted_iota(I32, st.shape, 1)
    n_head = st.shape[1] // n_stat
    for hp in range(n_hp):
        head = pl.program_id(1) * n_hp + hp
        for direction in range(2):
            for stat in range(3):
                col = jnp.sum(jnp.where(lane == (stat * 2 + direction) * n_head + head, st, 0.0), axis=1, keepdims=True)
                rep_ref[hp * n_stat + 3 * direction + stat] = jnp.broadcast_to(col, rep_ref.shape[1:])
    n0 = jnp.zeros((1, dqk), F32)
    m0 = jnp.full((1, LANES), M_INIT, F32)

    def step(i, carry):
        jf = i
        jb = jnp.where(i < n_ctx_chunk, n_ctx_chunk - 1 - i, n_chunk - 1 - (i - n_ctx_chunk))
        rows = (pl.ds(pl.multiple_of(jf * CHUNK, CHUNK), CHUNK), pl.ds(pl.multiple_of(jb * CHUNK, CHUNK), CHUNK))
        chunk = (jf, jb)
        scans = [(hp, direction) for hp in range(n_hp) for direction in range(2)]
        loaded = []
        for hp, direction in scans:
            r, j = rows[direction], chunk[direction]
            qc = slice(hp * dqk, (hp + 1) * dqk)
            vc = slice(hp * dv, (hp + 1) * dv)
            sc = hp * n_stat + 3 * direction
            loaded.append((q_ref[r, qc], k_ref[r, qc], v_ref[r, vc],
                           rep_ref[sc, r, :], rep_ref[sc + 1, r, :], rep_ref[sc + 2, r, :],
                           r_ref[0, hp, direction, j], c_ref[2 * hp + direction]))
        opened = [_mlstm_chunk_open(*loaded[s], *carry[s], backward=direction == 1)
                  for s, (hp, direction) in enumerate(scans)]
        for s, (hp, direction) in enumerate(scans):
            c_ref[s] = opened[s][1][0]
            (hf_ref, hb_ref)[direction][rows[direction], hp * dv:(hp + 1) * dv] = _mlstm_chunk_close(*opened[s][0])
        return tuple((n_new, m_new) for _, (_, n_new, m_new) in opened)

    lax.fori_loop(0, n_chunk, step, ((n0, m0),) * (2 * n_hp))
    n_ctx = n_ctx_chunk * CHUNK
    n_lat = (n_chunk - n_ctx_chunk) * CHUNK
    lat = pl.ds(n_ctx, n_lat)
    for hp in range(n_hp):
        vc = slice(hp * dv, (hp + 1) * dv)
        h = hf_ref[lat, vc] + hb_ref[lat, vc]
        hn = h * lax.rsqrt(jnp.mean(h * h, axis=-1, keepdims=True) + EPS)
        y = hn * nw_ref[:, vc] * _sigmoid(o_ref[lat, vc].astype(F32))
        z_ref[:, vc] = y.astype(BF16)


def _mlstm_scan(p, st, rr, norm_w, *, n_batch, n_head, seq_all, n_ctx, d):
    dqk = d // (2 * n_head)
    dv = d // n_head
    n_hp = 2 if n_head % 2 == 0 else 1
    n_chunk = seq_all // CHUNK
    n_lat = seq_all - n_ctx
    qk_blocks = (d // 2) // (n_hp * dqk)
    v_blocks = d // (n_hp * dv)
    return pl.pallas_call(
        functools.partial(_mlstm_scan_kernel, n_chunk=n_chunk, n_ctx_chunk=n_ctx // CHUNK, dqk=dqk, dv=dv,
                          n_hp=n_hp),
        grid=(n_batch, n_head // n_hp),
        in_specs=[
            pl.BlockSpec((seq_all, n_hp * dqk), lambda b, h: (b, h)),
            pl.BlockSpec((seq_all, n_hp * dqk), lambda b, h: (b, qk_blocks + h)),
            pl.BlockSpec((seq_all, n_hp * dv), lambda b, h: (b, v_blocks + h)),
            pl.BlockSpec((seq_all, n_hp * dv), lambda b, h: (b, 2 * v_blocks + h)),
            pl.BlockSpec((seq_all, 6 * n_head), lambda b, h: (b, 0)),
            pl.BlockSpec((1, n_hp, 2, n_chunk, 1, CHUNK), lambda b, h: (b, h, 0, 0, 0, 0)),
            pl.BlockSpec((1, n_hp * dv), lambda b, h: (0, h)),
        ],
        out_specs=pl.BlockSpec((n_lat, n_hp * dv), lambda b, h: (b, h)),
        out_shape=jax.ShapeDtypeStruct((n_batch * n_lat, d), BF16),
        scratch_shapes=[
            pltpu.VMEM((seq_all, n_hp * dv), F32),
            pltpu.VMEM((seq_all, n_hp * dv), F32),
            pltpu.VMEM((2 * n_hp, dqk, dv), F32),
            pltpu.VMEM((n_hp * 6, seq_all, LANES), F32),
        ],
        compiler_params=_cparams("parallel", "parallel"),
        name="mlstm_scan",
    )(p, p, p, p, st, rr, norm_w)


def kernel(x, c, ctx, c_ctx, ada_w, ada_b, norm_mix_w, norm_ffn_w, conv_in_w, conv_dw_w, conv_out_w,
           mlstm_in_w, mlstm_gate_b, mlstm_norm_w, mlstm_out_w, router_w, router_bias,
           exp_gate_w, exp_up_w, exp_down_w, shared_gate_w, shared_up_w, shared_down_w, final_norm_w):
    n_batch, seq, d = x.shape
    n_ctx = ctx.shape[1]
    assert ada_w.shape[0] == 2 and n_ctx == TILE and seq % TILE == 0 and n_batch + 1 <= ADA_ROWS
    seq_all = n_ctx + seq
    tiles_per_b = seq_all // TILE
    lat_tiles_per_b = seq // TILE
    n_head = (mlstm_in_w.shape[2] - 3 * d) // 4

    cond = jnp.zeros((ADA_ROWS, d), F32).at[:n_batch].set(c).at[n_batch].set(c_ctx)
    mod = _ada_mod(cond, ada_w, ada_b)
    mod0 = mod[0].reshape(ADA_ROWS, 1, ADA_CHUNKS * d)
    mod1 = mod[1].reshape(ADA_ROWS, 1, ADA_CHUNKS * d)

    def all_mod_row(i):
        return jnp.where(i % tiles_per_b == 0, n_batch, i // tiles_per_b)

    def lat_mod_row(i):
        return i // lat_tiles_per_b

    def lat_tile(i):
        return (i // lat_tiles_per_b) * tiles_per_b + 1 + i % lat_tiles_per_b

    row = lambda w: w.reshape(1, -1)
    bf = lambda w: w.astype(BF16)
    ctx2 = ctx.reshape(n_batch * n_ctx, d)
    x2 = x.reshape(n_batch * seq, d)

    def is_ctx_tile(i):
        return i % tiles_per_b == 0

    def ctx_tile(i):
        return i // tiles_per_b

    def x_tile(i):
        return (i // tiles_per_b) * lat_tiles_per_b + jnp.maximum(i % tiles_per_b - 1, 0)

    z0 = _conv_in(ctx2, x2, mod0, row(norm_mix_w[0]), bf(conv_in_w[0]), conv_dw_w[0],
                  n_batch=n_batch, tiles_per_b=tiles_per_b, ctx_map=ctx_tile, lat_map=x_tile)
    xn0, h20, eidx0, gate0, rank0, cnt0 = _post(
        z0, ctx2, x2, mod0, row(norm_ffn_w[0]), bf(conv_out_w[0]), router_w[0], row(router_bias[0]),
        a_map=ctx_tile, b_map=x_tile, pick_a=is_ctx_tile, mod_row_map=all_mod_row)
    x1 = _moe(h20, xn0, eidx0, gate0, rank0, cnt0, mod0, exp_gate_w, exp_up_w, exp_down_w,
              bf(shared_gate_w[0]), bf(shared_up_w[0]), bf(shared_down_w[0]), row(final_norm_w),
              layer=0, mod_row_map=all_mod_row, final=False)

    w_in = mlstm_in_w[0]
    p, st, r = _mlstm_in(x1, mod1, row(norm_mix_w[1]), bf(w_in), w_in[:, 3 * d:], row(mlstm_gate_b[0]),
                     n_batch=n_batch, tiles_per_b=tiles_per_b, n_head=n_head)
    n_chunk = seq_all // CHUNK
    rr = r.reshape(n_batch, n_chunk, CHUNK, 2, n_head).transpose(0, 4, 3, 1, 2)
    rr = rr.reshape(n_batch, n_head, 2, n_chunk, 1, CHUNK)
    z1 = _mlstm_scan(p, st, rr, row(mlstm_norm_w[0]), n_batch=n_batch, n_head=n_head, seq_all=seq_all,
                     n_ctx=n_ctx, d=d)
    xn1, h21, eidx1, gate1, rank1, cnt1 = _post(
        z1, x1, x1, mod1, row(norm_ffn_w[1]), bf(mlstm_out_w[0]), router_w[1], row(router_bias[1]),
        a_map=lat_tile, b_map=lambda i: 0, pick_a=lambda i: i >= 0, mod_row_map=lat_mod_row)
    out = _moe(h21, xn1, eidx1, gate1, rank1, cnt1, mod1, exp_gate_w, exp_up_w, exp_down_w,
               bf(shared_gate_w[1]), bf(shared_up_w[1]), bf(shared_down_w[1]), row(final_norm_w),
               layer=1, mod_row_map=lat_mod_row, final=True)
    return out.reshape(n_batch, seq, d)
```
